```python
import jax, jax.numpy as jnp
from jax import lax
import numpy as np

D_MODEL = 2048
BATCH = 8
SEQ = 4096
DEPTH = 1
DEC_BATCH = 1
DEC_SEQ = 8192
PAST_LEN = 128

N_META = 16
GRID_W = 64
NA_HEADS = 16
NA_HEAD_DIM = 64
NA_WIDTH = NA_HEADS * NA_HEAD_DIM
NA_WIN_H_MAX = 8
NA_WIN_W = 16
RET_HEADS = 8
RET_HEAD_DIM = 128
RET_WIDTH = RET_HEADS * RET_HEAD_DIM
RET_CHUNK = 128
ROPE_BASE = 10000.0
MIX_WIDTH = NA_WIDTH + RET_WIDTH
IN_WIDTH = 3 * NA_WIDTH + 4 * RET_WIDTH
D_FF = -(-(8 * D_MODEL) // (3 * 256)) * 256
LN_EPS = 1e-5
DEEPNORM_ALPHA = (2.0 * DEPTH) ** 0.25
DEEPNORM_BETA = (8.0 * DEPTH) ** -0.25

kernel_name = "hybrid_natten_retention_encoder"


def layer_norm(x, g, b):
    xf = x.astype(jnp.float32)
    mu = jnp.mean(xf, axis=-1, keepdims=True)
    var = jnp.mean(jnp.square(xf - mu), axis=-1, keepdims=True)
    y = (xf - mu) * lax.rsqrt(var + LN_EPS)
    return (y * g.astype(jnp.float32) + b.astype(jnp.float32)).astype(x.dtype)


def rope(x, pos):
    half = x.shape[-1] // 2
    inv = ROPE_BASE ** (-jnp.arange(half, dtype=jnp.float32) / half)
    ang = pos[:, None] * inv[None, :]
    c = jnp.cos(ang)[None, :, None, :]
    s = jnp.sin(ang)[None, :, None, :]
    xf = x.astype(jnp.float32)
    x1, x2 = xf[..., :half], xf[..., half:]
    return jnp.concatenate([x1 * c - x2 * s, x1 * s + x2 * c], axis=-1).astype(x.dtype)


def neighborhood_attention(q, k, v, rpb):
    B, H, L, d = q.shape
    n = L - N_META
    rows = n // GRID_W
    win_h = min(NA_WIN_H_MAX, rows)
    q = q * (d ** -0.5)
    qm, km, vm = q[:, :, :N_META], k[:, :, :N_META], v[:, :, :N_META]
    qg = q[:, :, N_META:].reshape(B, H, rows, GRID_W, d)
    kg = k[:, :, N_META:].reshape(B, H, rows, GRID_W, d)
    vg = v[:, :, N_META:].reshape(B, H, rows, GRID_W, d)

    s_mm = jnp.einsum('bhmd,bhnd->bhmn', qm, km).astype(jnp.float32)
    o_meta = jnp.einsum('bhmn,bhnd->bhmd', jax.nn.softmax(s_mm, axis=-1).astype(v.dtype), vm)

    cols = np.arange(GRID_W)
    cs = np.clip(cols - NA_WIN_W // 2, 0, GRID_W - NA_WIN_W)
    col_idx = cs[:, None] + np.arange(NA_WIN_W)[None, :]
    dc_idx = col_idx - cols[:, None] + (NA_WIN_W - 1)
    rpb_cols = rpb[:, :, dc_idx]

    def one_row(r):
        rs = jnp.clip(r - win_h // 2, 0, rows - win_h)
        q_row = lax.dynamic_index_in_dim(qg, r, axis=2, keepdims=False)
        k_band = lax.dynamic_slice_in_dim(kg, rs, win_h, axis=2)
        v_band = lax.dynamic_slice_in_dim(vg, rs, win_h, axis=2)
        k_win = k_band[:, :, :, col_idx]
        v_win = v_band[:, :, :, col_idx]
        dr_idx = rs + jnp.arange(win_h) - r + (NA_WIN_H_MAX - 1)
        bias = jnp.take(rpb_cols, dr_idx, axis=1)
        bias = jnp.transpose(bias, (0, 2, 1, 3)).astype(jnp.float32)
        s_win = jnp.einsum('bhcd,bhicjd->bhcij', q_row, k_win).astype(jnp.float32) + bias
        s_win = s_win.reshape(B, H, GRID_W, win_h * NA_WIN_W)
        s_meta = jnp.einsum('bhcd,bhmd->bhcm', q_row, km).astype(jnp.float32)
        p = jax.nn.softmax(jnp.concatenate([s_meta, s_win], axis=-1), axis=-1).astype(v.dtype)
        p_meta = p[..., :N_META]
        p_win = p[..., N_META:].reshape(B, H, GRID_W, win_h, NA_WIN_W)
        return (jnp.einsum('bhcm,bhmd->bhcd', p_meta, vm)
                + jnp.einsum('bhcij,bhicjd->bhcd', p_win, v_win))

    o_rows = lax.map(one_row, jnp.arange(rows))
    o_grid = jnp.transpose(o_rows, (1, 2, 0, 3, 4)).reshape(B, H, n, d)
    return jnp.concatenate([o_meta, o_grid], axis=2)


def retention_direction(q, k, v, log_gamma, include_diag):
    B, H, Lp, dk = q.shape
    dv = v.shape[-1]
    C = RET_CHUNK
    nC = Lp // C

    def chunks(t):
        return jnp.transpose(t.reshape(B, H, nC, C, t.shape[-1]), (2, 0, 1, 3, 4))

    posn = np.arange(C)
    diffn = posn[:, None] - posn[None, :]
    mask = (diffn >= 0) if include_diag else (diffn > 0)
    diff_pos = jnp.asarray(np.maximum(diffn, 0), dtype=jnp.float32)
    decay_intra = jnp.where(mask[None], jnp.exp(diff_pos[None] * log_gamma[:, None, None]), 0.0)
    pos = jnp.arange(C, dtype=jnp.float32)
    xi = jnp.exp((pos[None, :] + 1.0) * log_gamma[:, None])[..., None]
    zeta = jnp.exp((C - 1.0 - pos[None, :]) * log_gamma[:, None])[..., None]
    chunk_decay = jnp.exp(C * log_gamma)[:, None, None]

    def step(S, qkv):
        qc, kc, vc = qkv
        qf, kf, vf = qc.astype(jnp.float32), kc.astype(jnp.float32), vc.astype(jnp.float32)
        s = jnp.einsum('bhid,bhjd->bhij', qf, kf) * decay_intra
        o = jnp.einsum('bhij,bhje->bhie', s, vf) + jnp.einsum('bhid,bhde->bhie', qf * xi, S)
        S = S * chunk_decay + jnp.einsum('bhjd,bhje->bhde', kf * zeta, vf)
        return S, o

    S0 = jnp.zeros((B, H, dk, dv), jnp.float32)
    _, o = lax.scan(step, S0, (chunks(q), chunks(k), chunks(v)))
    return jnp.transpose(o, (1, 2, 0, 3, 4)).reshape(B, H, Lp, dv)


def bidirectional_retention(q, k, v, lg_f, lg_b):
    L = q.shape[2]
    pad = (-L) % RET_CHUNK
    cfg = ((0, 0), (0, 0), (pad, 0), (0, 0))
    qp, kp, vp = jnp.pad(q, cfg), jnp.pad(k, cfg), jnp.pad(v, cfg)
    o_f = retention_direction(qp, kp, vp, lg_f, True)
    o_b = retention_direction(qp[:, :, ::-1], kp[:, :, ::-1], vp[:, :, ::-1], lg_b, False)[:, :, ::-1]
    return (o_f + o_b)[:, :, pad:]


def encoder_layer(h, w_in, na_rpb, ret_decay_f, ret_decay_b, ret_gn_g, w_out,
                  ln1_g, ln1_b, w_ffn_gate, w_ffn_up, w_ffn_down, ln2_g, ln2_b):
    B, L, _ = h.shape
    proj = h @ w_in
    sizes = [NA_WIDTH] * 3 + [RET_WIDTH] * 4
    splits = [int(s) for s in np.cumsum(sizes)[:-1]]
    qa, ka, va, qr, kr, vr, gr = jnp.split(proj, splits, axis=-1)

    def na_heads(t):
        return jnp.transpose(t.reshape(B, L, NA_HEADS, NA_HEAD_DIM), (0, 2, 1, 3))
    o_na = neighborhood_attention(na_heads(qa), na_heads(ka), na_heads(va), na_rpb)
    o_na = jnp.transpose(o_na, (0, 2, 1, 3)).reshape(B, L, NA_WIDTH)

    pos = jnp.arange(L, dtype=jnp.float32)
    qh = jnp.transpose(rope(qr.reshape(B, L, RET_HEADS, RET_HEAD_DIM), pos), (0, 2, 1, 3))
    kh = jnp.transpose(rope(kr.reshape(B, L, RET_HEADS, RET_HEAD_DIM), pos), (0, 2, 1, 3)) * (RET_HEAD_DIM ** -0.5)
    vh = jnp.transpose(vr.reshape(B, L, RET_HEADS, RET_HEAD_DIM), (0, 2, 1, 3))
    lg_f = jax.nn.log_sigmoid(ret_decay_f.astype(jnp.float32))
    lg_b = jax.nn.log_sigmoid(ret_decay_b.astype(jnp.float32))
    o_ret = jnp.transpose(bidirectional_retention(qh, kh, vh, lg_f, lg_b), (0, 2, 1, 3))
    mu = jnp.mean(o_ret, axis=-1, keepdims=True)
    var = jnp.mean(jnp.square(o_ret - mu), axis=-1, keepdims=True)
    o_ret = ((o_ret - mu) * lax.rsqrt(var + LN_EPS)).reshape(B, L, RET_WIDTH) * ret_gn_g.astype(jnp.float32)
    o_ret = (jax.nn.silu(gr.astype(jnp.float32)) * o_ret).astype(h.dtype)

    mix = jnp.concatenate([o_na, o_ret], axis=-1) @ w_out
    h = layer_norm(DEEPNORM_ALPHA * h + mix, ln1_g, ln1_b)
    ffn = (jax.nn.silu(h @ w_ffn_gate) * (h @ w_ffn_up)) @ w_ffn_down
    return layer_norm(DEEPNORM_ALPHA * h + ffn, ln2_g, ln2_b)


def trunk(x, meta_tokens, ln_in_g, ln_in_b, w_in, na_rpb, ret_decay_f, ret_decay_b, ret_gn_g,
          w_out, ln1_g, ln1_b, w_ffn_gate, w_ffn_up, w_ffn_down, ln2_g, ln2_b):
    B = x.shape[0]
    meta = jnp.broadcast_to(meta_tokens[None].astype(x.dtype), (B, N_META, x.shape[-1]))
    h = layer_norm(jnp.concatenate([meta, x], axis=1), ln_in_g, ln_in_b)
    for l in range(DEPTH):
        h = encoder_layer(h, w_in[l], na_rpb[l], ret_decay_f[l], ret_decay_b[l], ret_gn_g[l], w_out[l],
                          ln1_g[l], ln1_b[l], w_ffn_gate[l], w_ffn_up[l], w_ffn_down[l], ln2_g[l], ln2_b[l])
    return h[:, N_META:]


def setup_inputs(seed: int = 0) -> dict:
    key = jax.random.key(seed)
    ks = jax.random.split(key, 20)
    f32 = jnp.float32
    nrm = lambda k, shape: jax.random.normal(k, shape, f32)

    col_scale = np.ones((IN_WIDTH,), np.float32)
    col_scale[2 * NA_WIDTH:3 * NA_WIDTH] = DEEPNORM_BETA
    col_scale[3 * NA_WIDTH + 2 * RET_WIDTH:3 * NA_WIDTH + 3 * RET_WIDTH] = DEEPNORM_BETA
    w_in = nrm(ks[3], (DEPTH, D_MODEL, IN_WIDTH)) * (D_MODEL ** -0.5) * jnp.asarray(col_scale)

    decay_base = jnp.asarray(np.log(2.0 ** (5.0 + np.arange(RET_HEADS)) - 1.0), dtype=f32)
    return {
        "x_prompt": nrm(ks[0], (BATCH, SEQ, D_MODEL)),
        "x_sample": nrm(ks[1], (DEC_BATCH, DEC_SEQ, D_MODEL)),
        "meta_tokens": nrm(ks[2], (N_META, D_MODEL)),
        "ln_in_g": 1.0 + 0.02 * nrm(ks[4], (D_MODEL,)),
        "ln_in_b": 0.02 * nrm(ks[5], (D_MODEL,)),
        "w_in": w_in,
        "na_rpb": 0.1 * nrm(ks[6], (DEPTH, NA_HEADS, 2 * NA_WIN_H_MAX - 1, 2 * NA_WIN_W - 1)),
        "ret_decay_f": decay_base[None] + 0.01 * nrm(ks[7], (DEPTH, RET_HEADS)),
        "ret_decay_b": decay_base[None] + 0.01 * nrm(ks[8], (DEPTH, RET_HEADS)),
        "ret_gn_g": 1.0 + 0.02 * nrm(ks[9], (DEPTH, RET_WIDTH)),
        "w_out": nrm(ks[10], (DEPTH, MIX_WIDTH, D_MODEL)) * (MIX_WIDTH ** -0.5) * DEEPNORM_BETA,
        "ln1_g": 1.0 + 0.02 * nrm(ks[11], (DEPTH, D_MODEL)),
        "ln1_b": 0.02 * nrm(ks[12], (DEPTH, D_MODEL)),
        "w_ffn_gate": nrm(ks[13], (DEPTH, D_MODEL, D_FF)) * (D_MODEL ** -0.5) * DEEPNORM_BETA,
        "w_ffn_up": nrm(ks[14], (DEPTH, D_MODEL, D_FF)) * (D_MODEL ** -0.5) * DEEPNORM_BETA,
        "w_ffn_down": nrm(ks[15], (DEPTH, D_FF, D_MODEL)) * (D_FF ** -0.5) * DEEPNORM_BETA,
        "ln2_g": 1.0 + 0.02 * nrm(ks[16], (DEPTH, D_MODEL)),
        "ln2_b": 0.02 * nrm(ks[17], (DEPTH, D_MODEL)),
    }


def reference(x_prompt, x_sample, meta_tokens, ln_in_g, ln_in_b, w_in, na_rpb, ret_decay_f, ret_decay_b,
              ret_gn_g, w_out, ln1_g, ln1_b, w_ffn_gate, w_ffn_up, w_ffn_down, ln2_g, ln2_b):
    y_prompt = trunk(x_prompt, meta_tokens, ln_in_g, ln_in_b, w_in, na_rpb, ret_decay_f, ret_decay_b,
                     ret_gn_g, w_out, ln1_g, ln1_b, w_ffn_gate, w_ffn_up, w_ffn_down, ln2_g, ln2_b)
    y_sample = trunk(x_sample, meta_tokens, ln_in_g, ln_in_b, w_in, na_rpb, ret_decay_f, ret_decay_b,
                     ret_gn_g, w_out, ln1_g, ln1_b, w_ffn_gate, w_ffn_up, w_ffn_down, ln2_g, ln2_b)
    return (y_prompt, y_sample)
```

```python
import functools

import jax
import jax.numpy as jnp
import numpy as np
from jax import lax
from jax.experimental import pallas as pl
from jax.experimental.pallas import tpu as pltpu

D_MODEL = 2048
N_META = 16
GRID_W = 64
NA_HEADS = 16
NA_HEAD_DIM = 64
NA_WIDTH = NA_HEADS * NA_HEAD_DIM
NA_WIN_H = 8
NA_WIN_W = 16
NA_BAND = NA_WIN_H * GRID_W
RET_HEADS = 8
RET_HEAD_DIM = 128
RET_WIDTH = RET_HEADS * RET_HEAD_DIM
CHUNK = 128
ROPE_BASE = 10000.0
IN_WIDTH = 3 * NA_WIDTH + 4 * RET_WIDTH
D_FF = 5632
LN_EPS = 1e-5
ALPHA = 2.0 ** 0.25

COL_QA, COL_KA, COL_VA = 0, NA_WIDTH, 2 * NA_WIDTH
COL_QR = 3 * NA_WIDTH
COL_KR = COL_QR + RET_WIDTH
COL_VR = COL_KR + RET_WIDTH
COL_GR = COL_VR + RET_WIDTH

LANES = 128
NEG = -1e30
VMEM_LIMIT = 56 * 1024 * 1024

F32 = jnp.float32
BF16 = jnp.bfloat16


def _layer_norm_rows(x, g, b):
    mu = jnp.mean(x, axis=-1, keepdims=True)
    xc = x - mu
    var = jnp.mean(xc * xc, axis=-1, keepdims=True)
    return xc * lax.rsqrt(var + LN_EPS) * g + b


def _inproj_kernel(x_ref, g_ref, b_ref, w_ref, cq_ref, sq_ref, ck_ref, sk_ref, o_ref, hb_ref, *, tn, rb):
    j = pl.program_id(1)
    tm = x_ref.shape[0]

    @pl.when(j == 0)
    def _():
        def ln_rows(i, c):
            sl = pl.ds(pl.multiple_of(i * rb, rb), rb)
            hb_ref[sl, :] = _layer_norm_rows(x_ref[sl, :], g_ref[...], b_ref[...]).astype(BF16)
            return c

        lax.fori_loop(0, tm // rb, ln_rows, 0)

    acc = jnp.dot(hb_ref[...], w_ref[...], preferred_element_type=F32)
    col0 = j * tn
    is_q = jnp.logical_and(col0 >= COL_QR, col0 < COL_KR)
    is_k = jnp.logical_and(col0 >= COL_KR, col0 < COL_VR)

    def rope(c_ref, s_ref):
        c = c_ref[...]
        s = s_ref[...]
        for t in range(tn // LANES):
            xs = acc[:, t * LANES:(t + 1) * LANES]
            o_ref[:, t * LANES:(t + 1) * LANES] = (xs * c + pltpu.roll(xs, LANES // 2, 1) * s).astype(BF16)

    @pl.when(is_q)
    def _():
        rope(cq_ref, sq_ref)

    @pl.when(is_k)
    def _():
        rope(ck_ref, sk_ref)

    @pl.when(jnp.logical_not(jnp.logical_or(is_q, is_k)))
    def _():
        o_ref[...] = acc.astype(BF16)


def _inproj(x2d, ln_g, ln_b, w_bf, tabs, n_seq, tm, tn=512):
    T = x2d.shape[0]
    assert T % tm == 0 and n_seq % tm == 0 and IN_WIDTH % tn == 0 and RET_WIDTH % tn == 0
    blocks_per_seq = n_seq // tm
    tab_spec = pl.BlockSpec((tm, LANES), lambda i, j: (i % blocks_per_seq, 0))
    rb = min(tm, 32)
    return pl.pallas_call(
        functools.partial(_inproj_kernel, tn=tn, rb=rb),
        grid=(T // tm, IN_WIDTH // tn),
        in_specs=[
            pl.BlockSpec((tm, D_MODEL), lambda i, j: (i, 0)),
            pl.BlockSpec((1, D_MODEL), lambda i, j: (0, 0)),
            pl.BlockSpec((1, D_MODEL), lambda i, j: (0, 0)),
            pl.BlockSpec((D_MODEL, tn), lambda i, j: (0, j)),
            tab_spec, tab_spec, tab_spec, tab_spec,
        ],
        out_specs=pl.BlockSpec((tm, tn), lambda i, j: (i, j)),
        out_shape=jax.ShapeDtypeStruct((T, IN_WIDTH), BF16),
        scratch_shapes=[pltpu.VMEM((tm, D_MODEL), BF16)],
        compiler_params=pltpu.CompilerParams(
            dimension_semantics=("arbitrary", "arbitrary"), vmem_limit_bytes=VMEM_LIMIT),
        name="inproj",
    )(x2d, ln_g, ln_b, w_bf, *tabs)


def _na_kernel(q_ref, k_ref, v_ref, km_ref, vm_ref, bias_ref, o_ref, *, rows):
    lane = lax.broadcasted_iota(jnp.int32, (1, LANES), 1)
    scale = NA_HEAD_DIM ** -0.5
    qmask = [jnp.where((lane // NA_HEAD_DIM) == hh, scale, 0.0).astype(BF16) for hh in range(2)]
    meta_mask = jnp.where(lane < N_META, 0.0, NEG).astype(F32)
    km = km_ref[...]
    vm = vm_ref[...]
    nt = (((1,), (1,)), ((), ()))

    def row(r, c):
        rs = jnp.clip(r - NA_WIN_H // 2, 0, rows - NA_WIN_H)
        off = r - rs
        qsl = pl.ds(pl.multiple_of(r * GRID_W, GRID_W), GRID_W)
        bsl = pl.ds(pl.multiple_of(rs * GRID_W, GRID_W), NA_BAND)
        q = q_ref[qsl, :]
        kb = k_ref[bsl, :]
        vb = v_ref[bsl, :]
        outs = []
        for hh in range(2):
            qh = q * qmask[hh]
            s = lax.dot_general(qh, kb, nt, preferred_element_type=F32) + bias_ref[hh, pl.ds(off, 1)][0]
            sm = lax.dot_general(qh, km, nt, preferred_element_type=F32) + meta_mask
            m = jnp.maximum(jnp.max(s, axis=-1, keepdims=True), jnp.max(sm, axis=-1, keepdims=True))
            p = jnp.exp(s - m)
            pm = jnp.exp(sm - m)
            den = jnp.sum(p, axis=-1, keepdims=True) + jnp.sum(pm, axis=-1, keepdims=True)
            o = (jnp.dot(p.astype(BF16), vb, preferred_element_type=F32)
                 + jnp.dot(pm.astype(BF16), vm, preferred_element_type=F32))
            outs.append(o / den)
        o_ref[qsl, :] = jnp.where(lane < NA_HEAD_DIM, outs[0], outs[1]).astype(BF16)
        return c

    lax.fori_loop(0, rows, row, 0)


def _na_bias(rpb):
    cols = np.arange(GRID_W)
    cs = np.clip(cols - NA_WIN_W // 2, 0, GRID_W - NA_WIN_W)
    kc = np.arange(GRID_W)
    valid = (kc[None, :] >= cs[:, None]) & (kc[None, :] < cs[:, None] + NA_WIN_W)
    dc = np.clip(kc[None, :] - cols[:, None] + (NA_WIN_W - 1), 0, 2 * NA_WIN_W - 2)
    off = np.arange(NA_WIN_H)
    ki = np.arange(NA_WIN_H)
    dr = ki[None, :] - off[:, None] + (NA_WIN_H - 1)
    b = rpb[:, dr][:, :, :, dc]
    b = jnp.where(valid[None, None, None], b.astype(F32), NEG)
    b = jnp.transpose(b, (0, 1, 3, 2, 4))
    return b.reshape(NA_HEADS, NA_WIN_H, GRID_W, NA_BAND)


def _na(proj, proj_meta_pad, bias, B, n):
    rows = n // GRID_W
    assert rows >= NA_WIN_H
    pairs = NA_WIDTH // LANES
    seq_spec = lambda c0: pl.BlockSpec((n, LANES), lambda hp, b: (b, c0 // LANES + hp))
    meta_spec = lambda c0: pl.BlockSpec((LANES, LANES), lambda hp, b: (0, c0 // LANES + hp))
    return pl.pallas_call(
        functools.partial(_na_kernel, rows=rows),
        grid=(pairs, B),
        in_specs=[
            seq_spec(COL_QA), seq_spec(COL_KA), seq_spec(COL_VA),
            meta_spec(COL_KA), meta_spec(COL_VA),
            pl.BlockSpec((2, NA_WIN_H, GRID_W, NA_BAND), lambda hp, b: (hp, 0, 0, 0)),
        ],
        out_specs=pl.BlockSpec((n, LANES), lambda hp, b: (b, hp)),
        out_shape=jax.ShapeDtypeStruct((B * n, NA_WIDTH), BF16),
        compiler_params=pltpu.CompilerParams(
            dimension_semantics=("arbitrary", "arbitrary"), vmem_limit_bytes=VMEM_LIMIT),
        name="na_attn",
    )(proj, proj, proj, proj_meta_pad, proj_meta_pad, bias)


def _ret_kernel(q_ref, k_ref, v_ref, g_ref, km_ref, vm_ref, lgf_ref, lgb_ref, gn_ref, o_ref, ob_ref, *, n_chunks):
    lgf = lgf_ref[0]
    lgb = lgb_ref[0]
    ii = lax.broadcasted_iota(jnp.int32, (CHUNK, CHUNK), 0).astype(F32)
    jj = lax.broadcasted_iota(jnp.int32, (CHUNK, CHUNK), 1).astype(F32)
    diff = ii - jj
    dmat = jnp.where(diff >= 0, jnp.exp(jnp.maximum(diff, 0.0) * lgf), jnp.exp(jnp.maximum(-diff, 0.0) * lgb))
    xi_f = jnp.exp((ii + 1.0) * lgf)
    zeta_f = jnp.exp((CHUNK - 1.0 - ii) * lgf)
    cd_f = jnp.exp(CHUNK * lgf)
    xi_b = jnp.exp((CHUNK - ii) * lgb)
    zeta_b = jnp.exp(ii * lgb)
    cd_b = jnp.exp(CHUNK * lgb)
    nt = (((1,), (1,)), ((), ()))
    tn_dims = (((0,), (0,)), ((), ()))
    gn = gn_ref[...]

    def state_update(S, k, v, zeta, cd):
        kz = (k.astype(F32) * zeta).astype(BF16)
        return S * cd + lax.dot_general(kz, v, tn_dims, preferred_element_type=F32)

    def bwd(t, S):
        c = n_chunks - 1 - t
        sl = pl.ds(pl.multiple_of(c * CHUNK, CHUNK), CHUNK)
        q, k, v = q_ref[sl, :], k_ref[sl, :], v_ref[sl, :]
        ob_ref[sl, :] = xi_b * jnp.dot(q, S.astype(BF16), preferred_element_type=F32)
        return state_update(S, k, v, zeta_b, cd_b)

    lax.fori_loop(0, n_chunks, bwd, jnp.zeros((CHUNK, CHUNK), F32))

    S0 = state_update(jnp.zeros((CHUNK, CHUNK), F32), km_ref[...], vm_ref[...], zeta_f, cd_f)

    def fwd(c, S):
        sl = pl.ds(pl.multiple_of(c * CHUNK, CHUNK), CHUNK)
        q, k, v = q_ref[sl, :], k_ref[sl, :], v_ref[sl, :]
        s = lax.dot_general(q, k, nt, preferred_element_type=F32) * dmat
        o = (jnp.dot(s.astype(BF16), v, preferred_element_type=F32)
             + xi_f * jnp.dot(q, S.astype(BF16), preferred_element_type=F32)
             + ob_ref[sl, :])
        mu = jnp.mean(o, axis=-1, keepdims=True)
        oc = o - mu
        var = jnp.mean(oc * oc, axis=-1, keepdims=True)
        on = oc * lax.rsqrt(var + LN_EPS) * gn
        g = g_ref[sl, :].astype(F32)
        o_ref[sl, :] = (g / (1.0 + jnp.exp(-g)) * on).astype(BF16)
        return state_update(S, k, v, zeta_f, cd_f)

    lax.fori_loop(0, n_chunks, fwd, S0)


def _retention(proj, proj_meta_pad, lgf, lgb, gn_g, B, n):
    assert n % CHUNK == 0
    seq_spec = lambda c0: pl.BlockSpec((n, LANES), lambda b, h: (b, c0 // LANES + h))
    meta_spec = lambda c0: pl.BlockSpec((CHUNK, LANES), lambda b, h: (0, c0 // LANES + h))
    lg_spec = pl.BlockSpec((1, 1, LANES), lambda b, h: (h, 0, 0))
    return pl.pallas_call(
        functools.partial(_ret_kernel, n_chunks=n // CHUNK),
        grid=(B, RET_HEADS),
        in_specs=[
            seq_spec(COL_QR), seq_spec(COL_KR), seq_spec(COL_VR), seq_spec(COL_GR),
            meta_spec(COL_KR), meta_spec(COL_VR),
            lg_spec, lg_spec,
            pl.BlockSpec((1, LANES), lambda b, h: (0, h)),
        ],
        out_specs=pl.BlockSpec((n, LANES), lambda b, h: (b, h)),
        out_shape=jax.ShapeDtypeStruct((B * n, RET_WIDTH), BF16),
        scratch_shapes=[pltpu.VMEM((n, LANES), F32)],
        compiler_params=pltpu.CompilerParams(
            dimension_semantics=("arbitrary", "arbitrary"), vmem_limit_bytes=VMEM_LIMIT),
        name="retention",
    )(proj, proj, proj, proj, proj_meta_pad, proj_meta_pad, lgf, lgb, gn_g)


def _outproj_kernel(ona_ref, oret_ref, x_ref, w_ref, gin_ref, bin_ref, g1_ref, b1_ref, o_ref, acc_ref, *, rb):
    tm = x_ref.shape[0]
    acc_ref[...] = (jnp.dot(ona_ref[...], w_ref[:NA_WIDTH, :], preferred_element_type=F32)
                    + jnp.dot(oret_ref[...], w_ref[NA_WIDTH:, :], preferred_element_type=F32))

    def rows(i, c):
        sl = pl.ds(pl.multiple_of(i * rb, rb), rb)
        h = _layer_norm_rows(x_ref[sl, :], gin_ref[...], bin_ref[...])
        o_ref[sl, :] = _layer_norm_rows(ALPHA * h + acc_ref[sl, :], g1_ref[...], b1_ref[...])
        return c

    lax.fori_loop(0, tm // rb, rows, 0)


def _outproj(o_na, o_ret, x2d, w_bf, gin, bin_, g1, b1, tm=512):
    T = x2d.shape[0]
    assert T % tm == 0
    row_vec = pl.BlockSpec((1, D_MODEL), lambda i: (0, 0))
    return pl.pallas_call(
        functools.partial(_outproj_kernel, rb=32),
        grid=(T // tm,),
        in_specs=[
            pl.BlockSpec((tm, NA_WIDTH), lambda i: (i, 0)),
            pl.BlockSpec((tm, RET_WIDTH), lambda i: (i, 0)),
            pl.BlockSpec((tm, D_MODEL), lambda i: (i, 0)),
            pl.BlockSpec((D_MODEL, D_MODEL), lambda i: (0, 0)),
            row_vec, row_vec, row_vec, row_vec,
        ],
        out_specs=pl.BlockSpec((tm, D_MODEL), lambda i: (i, 0)),
        out_shape=jax.ShapeDtypeStruct((T, D_MODEL), F32),
        scratch_shapes=[pltpu.VMEM((tm, D_MODEL), F32)],
        compiler_params=pltpu.CompilerParams(dimension_semantics=("arbitrary",), vmem_limit_bytes=VMEM_LIMIT),
        name="outproj_ln1",
    )(o_na, o_ret, x2d, w_bf, gin, bin_, g1, b1)


def _ffn_kernel(h_ref, wg_ref, wu_ref, wd_ref, g2_ref, b2_ref, o_ref, hb_ref, acc_ref, *, rb):
    j = pl.program_id(1)
    tm = h_ref.shape[0]

    @pl.when(j == 0)
    def _():
        hb_ref[...] = h_ref[...].astype(BF16)
        acc_ref[...] = jnp.zeros_like(acc_ref)

    hb = hb_ref[...]
    g = jnp.dot(hb, wg_ref[...], preferred_element_type=F32)
    u = jnp.dot(hb, wu_ref[...], preferred_element_type=F32)
    a = (g / (1.0 + jnp.exp(-g)) * u).astype(BF16)
    acc_ref[...] += jnp.dot(a, wd_ref[...], preferred_element_type=F32)

    @pl.when(j == pl.num_programs(1) - 1)
    def _():
        def rows(i, c):
            sl = pl.ds(pl.multiple_of(i * rb, rb), rb)
            o_ref[sl, :] = _layer_norm_rows(ALPHA * h_ref[sl, :] + acc_ref[sl, :], g2_ref[...], b2_ref[...])
            return c

        lax.fori_loop(0, tm // rb, rows, 0)


def _ffn(h1, wg_bf, wu_bf, wd_bf, g2, b2, tm=512, tf=512):
    T = h1.shape[0]
    assert T % tm == 0 and D_FF % tf == 0
    row_vec = pl.BlockSpec((1, D_MODEL), lambda i, j: (0, 0))
    return pl.pallas_call(
        functools.partial(_ffn_kernel, rb=32),
        grid=(T // tm, D_FF // tf),
        in_specs=[
            pl.BlockSpec((tm, D_MODEL), lambda i, j: (i, 0)),
            pl.BlockSpec((D_MODEL, tf), lambda i, j: (0, j)),
            pl.BlockSpec((D_MODEL, tf), lambda i, j: (0, j)),
            pl.BlockSpec((tf, D_MODEL), lambda i, j: (j, 0)),
            row_vec, row_vec,
        ],
        out_specs=pl.BlockSpec((tm, D_MODEL), lambda i, j: (i, 0)),
        out_shape=jax.ShapeDtypeStruct((T, D_MODEL), F32),
        scratch_shapes=[pltpu.VMEM((tm, D_MODEL), BF16), pltpu.VMEM((tm, D_MODEL), F32)],
        compiler_params=pltpu.CompilerParams(
            dimension_semantics=("arbitrary", "arbitrary"), vmem_limit_bytes=VMEM_LIMIT),
        name="ffn_ln2",
    )(h1, wg_bf, wu_bf, wd_bf, g2, b2)


def _rope_tables(pos):
    half = RET_HEAD_DIM // 2
    inv = ROPE_BASE ** (-jnp.arange(half, dtype=F32) / half)
    ang = pos[:, None] * inv[None, :]
    c, s = jnp.cos(ang), jnp.sin(ang)
    cq = jnp.concatenate([c, c], axis=-1)
    sq = jnp.concatenate([-s, s], axis=-1)
    ks = RET_HEAD_DIM ** -0.5
    return cq, sq, cq * ks, sq * ks


def _pick_tm(n, cap):
    tm = cap
    while n % tm:
        tm //= 2
    return tm


def kernel(x_prompt, x_sample, meta_tokens, ln_in_g, ln_in_b, w_in, na_rpb, ret_decay_f, ret_decay_b,
           ret_gn_g, w_out, ln1_g, ln1_b, w_ffn_gate, w_ffn_up, w_ffn_down, ln2_g, ln2_b):
    row = lambda v: v.reshape(1, -1).astype(F32)
    gin, bin_ = row(ln_in_g), row(ln_in_b)
    w_in_bf = w_in[0].astype(BF16)
    w_out_bf = w_out[0].astype(BF16)
    wg_bf, wu_bf, wd_bf = w_ffn_gate[0].astype(BF16), w_ffn_up[0].astype(BF16), w_ffn_down[0].astype(BF16)
    bias = _na_bias(na_rpb[0])
    lg = lambda d: jnp.broadcast_to(jax.nn.log_sigmoid(d[0].astype(F32))[:, None, None], (RET_HEADS, 1, LANES))
    lgf, lgb = lg(ret_decay_f), lg(ret_decay_b)
    gn_g = row(ret_gn_g[0])

    meta_tabs = _rope_tables(jnp.arange(N_META, dtype=F32))
    proj_meta = _inproj(meta_tokens.astype(F32), gin, bin_, w_in_bf, meta_tabs, N_META, N_META)
    proj_meta_pad = jnp.pad(proj_meta, ((CHUNK - N_META, 0), (0, 0)))

    def group(x):
        B, n, _ = x.shape
        x2d = x.reshape(B * n, D_MODEL)
        tabs = _rope_tables(jnp.arange(n, dtype=F32) + float(N_META))
        proj = _inproj(x2d, gin, bin_, w_in_bf, tabs, n, _pick_tm(n, 1024))
        o_na = _na(proj, _na_meta(proj_meta), bias, B, n)
        o_ret = _retention(proj, proj_meta_pad, lgf, lgb, gn_g, B, n)
        h1 = _outproj(o_na, o_ret, x2d, w_out_bf, gin, bin_, row(ln1_g[0]), row(ln1_b[0]), tm=_pick_tm(B * n, 512))
        y = _ffn(h1, wg_bf, wu_bf, wd_bf, row(ln2_g[0]), row(ln2_b[0]), tm=_pick_tm(B * n, 512))
        return y.reshape(B, n, D_MODEL)

    return (group(x_prompt), group(x_sample))


def _na_meta(proj_meta):
    return jnp.pad(proj_meta, ((0, LANES - N_META), (0, 0)))
```

```python
import functools

import jax
import jax.numpy as jnp
import numpy as np
from jax import lax
from jax.experimental import pallas as pl
from jax.experimental.pallas import tpu as pltpu

D_MODEL = 2048
N_META = 16
GRID_W = 64
NA_HEADS = 16
NA_HEAD_DIM = 64
NA_WIDTH = NA_HEADS * NA_HEAD_DIM
NA_WIN_H = 8
NA_WIN_W = 16
NA_BAND = NA_WIN_H * GRID_W
RET_HEADS = 8
RET_HEAD_DIM = 128
RET_WIDTH = RET_HEADS * RET_HEAD_DIM
CHUNK = 128
ROPE_BASE = 10000.0
IN_WIDTH = 3 * NA_WIDTH + 4 * RET_WIDTH
D_FF = 5632
LN_EPS = 1e-5
ALPHA = 2.0 ** 0.25

COL_QA, COL_KA, COL_VA = 0, NA_WIDTH, 2 * NA_WIDTH
COL_QR = 3 * NA_WIDTH
COL_KR = COL_QR + RET_WIDTH
COL_VR = COL_KR + RET_WIDTH
COL_GR = COL_VR + RET_WIDTH

LANES = 128
NEG = -1e30
VMEM_LIMIT = 56 * 1024 * 1024

F32 = jnp.float32
BF16 = jnp.bfloat16


def _layer_norm_rows(x, g, b):
    mu = jnp.mean(x, axis=-1, keepdims=True)
    xc = x - mu
    var = jnp.mean(xc * xc, axis=-1, keepdims=True)
    return xc * lax.rsqrt(var + LN_EPS) * g + b


def _inproj_kernel(x_ref, g_ref, b_ref, w_ref, cq_ref, sq_ref, ck_ref, sk_ref, o_ref, hb_ref, *, tn, rb):
    j = pl.program_id(1)
    tm = x_ref.shape[0]

    @pl.when(j == 0)
    def _():
        def ln_rows(i, c):
            sl = pl.ds(pl.multiple_of(i * rb, rb), rb)
            hb_ref[sl, :] = _layer_norm_rows(x_ref[sl, :], g_ref[...], b_ref[...]).astype(BF16)
            return c

        lax.fori_loop(0, tm // rb, ln_rows, 0)

    acc = jnp.dot(hb_ref[...], w_ref[...], preferred_element_type=F32)
    col0 = j * tn
    is_q = jnp.logical_and(col0 >= COL_QR, col0 < COL_KR)
    is_k = jnp.logical_and(col0 >= COL_KR, col0 < COL_VR)

    def rope(c_ref, s_ref):
        c = c_ref[...]
        s = s_ref[...]
        for t in range(tn // LANES):
            xs = acc[:, t * LANES:(t + 1) * LANES]
            o_ref[:, t * LANES:(t + 1) * LANES] = (xs * c + pltpu.roll(xs, LANES // 2, 1) * s).astype(BF16)

    @pl.when(is_q)
    def _():
        rope(cq_ref, sq_ref)

    @pl.when(is_k)
    def _():
        rope(ck_ref, sk_ref)

    @pl.when(jnp.logical_not(jnp.logical_or(is_q, is_k)))
    def _():
        o_ref[...] = acc.astype(BF16)


def _inproj(x2d, ln_g, ln_b, w_bf, tabs, n_seq, tm, tn=512):
    T = x2d.shape[0]
    assert T % tm == 0 and n_seq % tm == 0 and IN_WIDTH % tn == 0 and RET_WIDTH % tn == 0
    blocks_per_seq = n_seq // tm
    tab_spec = pl.BlockSpec((tm, LANES), lambda i, j: (i % blocks_per_seq, 0))
    rb = min(tm, 128)
    return pl.pallas_call(
        functools.partial(_inproj_kernel, tn=tn, rb=rb),
        grid=(T // tm, IN_WIDTH // tn),
        in_specs=[
            pl.BlockSpec((tm, D_MODEL), lambda i, j: (i, 0)),
            pl.BlockSpec((1, D_MODEL), lambda i, j: (0, 0)),
            pl.BlockSpec((1, D_MODEL), lambda i, j: (0, 0)),
            pl.BlockSpec((D_MODEL, tn), lambda i, j: (0, j)),
            tab_spec, tab_spec, tab_spec, tab_spec,
        ],
        out_specs=pl.BlockSpec((tm, tn), lambda i, j: (i, j)),
        out_shape=jax.ShapeDtypeStruct((T, IN_WIDTH), BF16),
        scratch_shapes=[pltpu.VMEM((tm, D_MODEL), BF16)],
        compiler_params=pltpu.CompilerParams(
            dimension_semantics=("arbitrary", "arbitrary"), vmem_limit_bytes=VMEM_LIMIT),
        name="inproj",
    )(x2d, ln_g, ln_b, w_bf, *tabs)


def _na_kernel(q_ref, k_ref, v_ref, km_ref, vm_ref, bias_ref, o_ref, *, rows, unroll):
    lane = lax.broadcasted_iota(jnp.int32, (1, LANES), 1)
    scale = NA_HEAD_DIM ** -0.5
    qmask = [jnp.where((lane // NA_HEAD_DIM) == hh, scale, 0.0).astype(BF16) for hh in range(2)]
    meta_mask = jnp.where(lane < N_META, 0.0, NEG).astype(F32)
    km = km_ref[...]
    vm = vm_ref[...]
    nt = (((1,), (1,)), ((), ()))

    def scores(r):
        rs = jnp.clip(r - NA_WIN_H // 2, 0, rows - NA_WIN_H)
        off = r - rs
        qsl = pl.ds(pl.multiple_of(r * GRID_W, GRID_W), GRID_W)
        bsl = pl.ds(pl.multiple_of(rs * GRID_W, GRID_W), NA_BAND)
        q = q_ref[qsl, :]
        q2 = jnp.concatenate([q * qmask[0], q * qmask[1]], axis=0)
        s = lax.dot_general(q2, k_ref[bsl, :], nt, preferred_element_type=F32) + bias_ref[0, pl.ds(off, 1)][0]
        sm = lax.dot_general(q2, km, nt, preferred_element_type=F32) + meta_mask
        return jnp.concatenate([s, sm], axis=1), qsl, bsl

    def finish(s_all, qsl, bsl):
        m = jnp.max(s_all, axis=-1, keepdims=True)
        p = jnp.exp(s_all - m)
        den = jnp.sum(p, axis=-1, keepdims=True)
        pb = p.astype(BF16)
        o2 = (jnp.dot(pb[:, :NA_BAND], v_ref[bsl, :], preferred_element_type=F32)
              + jnp.dot(pb[:, NA_BAND:], vm, preferred_element_type=F32)) / den
        o_ref[qsl, :] = jnp.where(lane < NA_HEAD_DIM, o2[:GRID_W], o2[GRID_W:]).astype(BF16)

    def group(g, c):
        staged = [scores(g * unroll + u) for u in range(unroll)]
        for st in staged:
            finish(*st)
        return c

    lax.fori_loop(0, rows // unroll, group, 0)


def _na_bias(rpb):
    cols = np.arange(GRID_W)
    cs = np.clip(cols - NA_WIN_W // 2, 0, GRID_W - NA_WIN_W)
    kc = np.arange(GRID_W)
    valid = (kc[None, :] >= cs[:, None]) & (kc[None, :] < cs[:, None] + NA_WIN_W)
    dc = np.clip(kc[None, :] - cols[:, None] + (NA_WIN_W - 1), 0, 2 * NA_WIN_W - 2)
    off = np.arange(NA_WIN_H)
    ki = np.arange(NA_WIN_H)
    dr = ki[None, :] - off[:, None] + (NA_WIN_H - 1)
    b = rpb[:, dr][:, :, :, dc]
    b = jnp.where(valid[None, None, None], b.astype(F32), NEG)
    b = jnp.transpose(b, (0, 1, 3, 2, 4))
    b = b.reshape(NA_HEADS // 2, 2, NA_WIN_H, GRID_W, NA_BAND)
    return jnp.transpose(b, (0, 2, 1, 3, 4)).reshape(NA_HEADS // 2, NA_WIN_H, 2 * GRID_W, NA_BAND)


def _na(proj, proj_meta_pad, bias, B, n):
    rows = n // GRID_W
    assert rows >= NA_WIN_H
    pairs = NA_WIDTH // LANES
    seq_spec = lambda c0: pl.BlockSpec((n, LANES), lambda hp, b: (b, c0 // LANES + hp))
    meta_spec = lambda c0: pl.BlockSpec((LANES, LANES), lambda hp, b: (0, c0 // LANES + hp))
    return pl.pallas_call(
        functools.partial(_na_kernel, rows=rows, unroll=8),
        grid=(pairs, B),
        in_specs=[
            seq_spec(COL_QA), seq_spec(COL_KA), seq_spec(COL_VA),
            meta_spec(COL_KA), meta_spec(COL_VA),
            pl.BlockSpec((1, NA_WIN_H, 2 * GRID_W, NA_BAND), lambda hp, b: (hp, 0, 0, 0)),
        ],
        out_specs=pl.BlockSpec((n, LANES), lambda hp, b: (b, hp)),
        out_shape=jax.ShapeDtypeStruct((B * n, NA_WIDTH), BF16),
        compiler_params=pltpu.CompilerParams(
            dimension_semantics=("arbitrary", "arbitrary"), vmem_limit_bytes=VMEM_LIMIT),
        name="na_attn",
    )(proj, proj, proj, proj_meta_pad, proj_meta_pad, bias)


def _ret_kernel(q_ref, k_ref, v_ref, g_ref, km_ref, vm_ref, lgf_ref, lgb_ref, gn_ref, o_ref,
                ds_ref, sf_ref, sb_ref, *, n_chunks, group):
    lgf = lgf_ref[0]
    lgb = lgb_ref[0]
    ii = lax.broadcasted_iota(jnp.int32, (CHUNK, CHUNK), 0).astype(F32)
    jj = lax.broadcasted_iota(jnp.int32, (CHUNK, CHUNK), 1).astype(F32)
    diff = ii - jj
    dmat = jnp.where(diff >= 0, jnp.exp(jnp.maximum(diff, 0.0) * lgf), jnp.exp(jnp.maximum(-diff, 0.0) * lgb))
    xi_f = jnp.exp((ii + 1.0) * lgf)
    zeta_f = jnp.exp((CHUNK - 1.0 - ii) * lgf)
    cd_f = jnp.exp(CHUNK * lgf)
    xi_b = jnp.exp((CHUNK - ii) * lgb)
    zeta_b = jnp.exp(ii * lgb)
    cd_b = jnp.exp(CHUNK * lgb)
    nt = (((1,), (1,)), ((), ()))
    tn_dims = (((0,), (0,)), ((), ()))
    gn = gn_ref[...]

    def chunk_slice(ch):
        return pl.ds(pl.multiple_of(ch * CHUNK, CHUNK), CHUNK)

    def increments(k, v):
        kf = k.astype(F32)
        kz = jnp.concatenate([(kf * zeta_f).astype(BF16), (kf * zeta_b).astype(BF16)], axis=1)
        return lax.dot_general(kz, v, tn_dims, preferred_element_type=F32)

    def stage1(gi, c):
        for u in range(group):
            ch = gi * group + u
            sl = chunk_slice(ch)
            ds_ref[ch] = increments(k_ref[sl, :], v_ref[sl, :])
        return c

    lax.fori_loop(0, n_chunks // group, stage1, 0)

    def scan_f(ch, S):
        sf_ref[ch] = S.astype(BF16)
        return S * cd_f + ds_ref[ch, :CHUNK, :]

    def scan_b(t, S):
        ch = n_chunks - 1 - t
        sb_ref[ch] = S.astype(BF16)
        return S * cd_b + ds_ref[ch, CHUNK:, :]

    S0 = increments(km_ref[...], vm_ref[...])[:CHUNK]
    lax.fori_loop(0, n_chunks, scan_f, S0, unroll=4)
    lax.fori_loop(0, n_chunks, scan_b, jnp.zeros((CHUNK, CHUNK), F32), unroll=4)

    def scores(ch):
        sl = chunk_slice(ch)
        q = q_ref[sl, :]
        s = lax.dot_general(q, k_ref[sl, :], nt, preferred_element_type=F32) * dmat
        qf = q.astype(F32)
        return jnp.concatenate([s.astype(BF16), (qf * xi_f).astype(BF16), (qf * xi_b).astype(BF16)], axis=1)

    def mix(ch, lhs):
        rhs = jnp.concatenate([v_ref[chunk_slice(ch), :], sf_ref[ch], sb_ref[ch]], axis=0)
        return jnp.dot(lhs, rhs, preferred_element_type=F32)

    def finish(ch, o):
        sl = chunk_slice(ch)
        mu = jnp.mean(o, axis=-1, keepdims=True)
        oc = o - mu
        var = jnp.mean(oc * oc, axis=-1, keepdims=True)
        on = oc * lax.rsqrt(var + LN_EPS) * gn
        g = g_ref[sl, :].astype(F32)
        o_ref[sl, :] = (g / (1.0 + jnp.exp(-g)) * on).astype(BF16)

    def stage3(gi, c):
        chs = [gi * group + u for u in range(group)]
        lhs = [scores(ch) for ch in chs]
        outs = [mix(ch, l) for ch, l in zip(chs, lhs)]
        for ch, o in zip(chs, outs):
            finish(ch, o)
        return c

    lax.fori_loop(0, n_chunks // group, stage3, 0)


def _retention(proj, proj_meta_pad, lgf, lgb, gn_g, B, n):
    n_chunks, group = n // CHUNK, 4
    assert n % CHUNK == 0 and n_chunks % group == 0
    seq_spec = lambda c0: pl.BlockSpec((n, LANES), lambda b, h: (b, c0 // LANES + h))
    meta_spec = lambda c0: pl.BlockSpec((CHUNK, LANES), lambda b, h: (0, c0 // LANES + h))
    lg_spec = pl.BlockSpec((1, 1, LANES), lambda b, h: (h, 0, 0))
    return pl.pallas_call(
        functools.partial(_ret_kernel, n_chunks=n_chunks, group=group),
        grid=(B, RET_HEADS),
        in_specs=[
            seq_spec(COL_QR), seq_spec(COL_KR), seq_spec(COL_VR), seq_spec(COL_GR),
            meta_spec(COL_KR), meta_spec(COL_VR),
            lg_spec, lg_spec,
            pl.BlockSpec((1, LANES), lambda b, h: (0, h)),
        ],
        out_specs=pl.BlockSpec((n, LANES), lambda b, h: (b, h)),
        out_shape=jax.ShapeDtypeStruct((B * n, RET_WIDTH), BF16),
        scratch_shapes=[pltpu.VMEM((n_chunks, 2 * CHUNK, LANES), F32),
                        pltpu.VMEM((n_chunks, CHUNK, LANES), BF16),
                        pltpu.VMEM((n_chunks, CHUNK, LANES), BF16)],
        compiler_params=pltpu.CompilerParams(
            dimension_semantics=("arbitrary", "arbitrary"), vmem_limit_bytes=VMEM_LIMIT),
        name="retention",
    )(proj, proj, proj, proj, proj_meta_pad, proj_meta_pad, lgf, lgb, gn_g)


def _outproj_kernel(ona_ref, oret_ref, x_ref, w_ref, gin_ref, bin_ref, g1_ref, b1_ref, o_ref, acc_ref, *, rb):
    tm = x_ref.shape[0]
    acc_ref[...] = (jnp.dot(ona_ref[...], w_ref[:NA_WIDTH, :], preferred_element_type=F32)
                    + jnp.dot(oret_ref[...], w_ref[NA_WIDTH:, :], preferred_element_type=F32))

    def rows(i, c):
        sl = pl.ds(pl.multiple_of(i * rb, rb), rb)
        h = _layer_norm_rows(x_ref[sl, :], gin_ref[...], bin_ref[...])
        o_ref[sl, :] = _layer_norm_rows(ALPHA * h + acc_ref[sl, :], g1_ref[...], b1_ref[...])
        return c

    lax.fori_loop(0, tm // rb, rows, 0)


def _outproj(o_na, o_ret, x2d, w_bf, gin, bin_, g1, b1, tm=512):
    T = x2d.shape[0]
    assert T % tm == 0
    row_vec = pl.BlockSpec((1, D_MODEL), lambda i: (0, 0))
    return pl.pallas_call(
        functools.partial(_outproj_kernel, rb=128),
        grid=(T // tm,),
        in_specs=[
            pl.BlockSpec((tm, NA_WIDTH), lambda i: (i, 0)),
            pl.BlockSpec((tm, RET_WIDTH), lambda i: (i, 0)),
            pl.BlockSpec((tm, D_MODEL), lambda i: (i, 0)),
            pl.BlockSpec((D_MODEL, D_MODEL), lambda i: (0, 0)),
            row_vec, row_vec, row_vec, row_vec,
        ],
        out_specs=pl.BlockSpec((tm, D_MODEL), lambda i: (i, 0)),
        out_shape=jax.ShapeDtypeStruct((T, D_MODEL), F32),
        scratch_shapes=[pltpu.VMEM((tm, D_MODEL), F32)],
        compiler_params=pltpu.CompilerParams(dimension_semantics=("arbitrary",), vmem_limit_bytes=VMEM_LIMIT),
        name="outproj_ln1",
    )(o_na, o_ret, x2d, w_bf, gin, bin_, g1, b1)


def _ffn_kernel(h_ref, wg_ref, wu_ref, wd_ref, g2_ref, b2_ref, o_ref, hb_ref, acc_ref, *, rb):
    j = pl.program_id(1)
    tm = h_ref.shape[0]

    @pl.when(j == 0)
    def _():
        hb_ref[...] = h_ref[...].astype(BF16)
        acc_ref[...] = jnp.zeros_like(acc_ref)

    hb = hb_ref[...]
    g = jnp.dot(hb, wg_ref[...], preferred_element_type=F32)
    u = jnp.dot(hb, wu_ref[...], preferred_element_type=F32)
    a = (g / (1.0 + jnp.exp(-g)) * u).astype(BF16)
    acc_ref[...] += jnp.dot(a, wd_ref[...], preferred_element_type=F32)

    @pl.when(j == pl.num_programs(1) - 1)
    def _():
        def rows(i, c):
            sl = pl.ds(pl.multiple_of(i * rb, rb), rb)
            o_ref[sl, :] = _layer_norm_rows(ALPHA * h_ref[sl, :] + acc_ref[sl, :], g2_ref[...], b2_ref[...])
            return c

        lax.fori_loop(0, tm // rb, rows, 0)


def _ffn(h1, wg_bf, wu_bf, wd_bf, g2, b2, tm=512, tf=512):
    T = h1.shape[0]
    assert T % tm == 0 and D_FF % tf == 0
    row_vec = pl.BlockSpec((1, D_MODEL), lambda i, j: (0, 0))
    return pl.pallas_call(
        functools.partial(_ffn_kernel, rb=128),
        grid=(T // tm, D_FF // tf),
        in_specs=[
            pl.BlockSpec((tm, D_MODEL), lambda i, j: (i, 0)),
            pl.BlockSpec((D_MODEL, tf), lambda i, j: (0, j)),
            pl.BlockSpec((D_MODEL, tf), lambda i, j: (0, j)),
            pl.BlockSpec((tf, D_MODEL), lambda i, j: (j, 0)),
            row_vec, row_vec,
        ],
        out_specs=pl.BlockSpec((tm, D_MODEL), lambda i, j: (i, 0)),
        out_shape=jax.ShapeDtypeStruct((T, D_MODEL), F32),
        scratch_shapes=[pltpu.VMEM((tm, D_MODEL), BF16), pltpu.VMEM((tm, D_MODEL), F32)],
        compiler_params=pltpu.CompilerParams(
            dimension_semantics=("arbitrary", "arbitrary"), vmem_limit_bytes=VMEM_LIMIT),
        name="ffn_ln2",
    )(h1, wg_bf, wu_bf, wd_bf, g2, b2)


def _rope_tables(pos):
    half = RET_HEAD_DIM // 2
    inv = ROPE_BASE ** (-jnp.arange(half, dtype=F32) / half)
    ang = pos[:, None] * inv[None, :]
    c, s = jnp.cos(ang), jnp.sin(ang)
    cq = jnp.concatenate([c, c], axis=-1)
    sq = jnp.concatenate([-s, s], axis=-1)
    ks = RET_HEAD_DIM ** -0.5
    return cq, sq, cq * ks, sq * ks


def _pick_tm(n, cap):
    tm = cap
    while n % tm:
        tm //= 2
    return tm


def kernel(x_prompt, x_sample, meta_tokens, ln_in_g, ln_in_b, w_in, na_rpb, ret_decay_f, ret_decay_b,
           ret_gn_g, w_out, ln1_g, ln1_b, w_ffn_gate, w_ffn_up, w_ffn_down, ln2_g, ln2_b):
    row = lambda v: v.reshape(1, -1).astype(F32)
    gin, bin_ = row(ln_in_g), row(ln_in_b)
    w_in_bf = w_in[0].astype(BF16)
    w_out_bf = w_out[0].astype(BF16)
    wg_bf, wu_bf, wd_bf = w_ffn_gate[0].astype(BF16), w_ffn_up[0].astype(BF16), w_ffn_down[0].astype(BF16)
    bias = _na_bias(na_rpb[0])
    lg = lambda d: jnp.broadcast_to(jax.nn.log_sigmoid(d[0].astype(F32))[:, None, None], (RET_HEADS, 1, LANES))
    lgf, lgb = lg(ret_decay_f), lg(ret_decay_b)
    gn_g = row(ret_gn_g[0])

    meta_tabs = _rope_tables(jnp.arange(N_META, dtype=F32))
    proj_meta = _inproj(meta_tokens.astype(F32), gin, bin_, w_in_bf, meta_tabs, N_META, N_META)
    proj_meta_pad = jnp.pad(proj_meta, ((CHUNK - N_META, 0), (0, 0)))

    def group(x):
        B, n, _ = x.shape
        x2d = x.reshape(B * n, D_MODEL)
        tabs = _rope_tables(jnp.arange(n, dtype=F32) + float(N_META))
        proj = _inproj(x2d, gin, bin_, w_in_bf, tabs, n, _pick_tm(n, 1024))
        o_na = _na(proj, _na_meta(proj_meta), bias, B, n)
        o_ret = _retention(proj, proj_meta_pad, lgf, lgb, gn_g, B, n)
        h1 = _outproj(o_na, o_ret, x2d, w_out_bf, gin, bin_, row(ln1_g[0]), row(ln1_b[0]), tm=_pick_tm(B * n, 512))
        y = _ffn(h1, wg_bf, wu_bf, wd_bf, row(ln2_g[0]), row(ln2_b[0]), tm=_pick_tm(B * n, 512))
        return y.reshape(B, n, D_MODEL)

    return (group(x_prompt), group(x_sample))


def _na_meta(proj_meta):
    return jnp.pad(proj_meta, ((0, LANES - N_META), (0, 0)))
```

```python
import functools

import jax
import jax.numpy as jnp
import numpy as np
from jax import lax
from jax.experimental import pallas as pl
from jax.experimental.pallas import tpu as pltpu

D_MODEL = 2048
N_META = 16
GRID_W = 64
NA_HEADS = 16
NA_HEAD_DIM = 64
NA_WIDTH = NA_HEADS * NA_HEAD_DIM
NA_WIN_H = 8
NA_WIN_W = 16
NA_BAND = NA_WIN_H * GRID_W
RET_HEADS = 8
RET_HEAD_DIM = 128
RET_WIDTH = RET_HEADS * RET_HEAD_DIM
CHUNK = 128
ROPE_BASE = 10000.0
IN_WIDTH = 3 * NA_WIDTH + 4 * RET_WIDTH
D_FF = 5632
LN_EPS = 1e-5
ALPHA = 2.0 ** 0.25

COL_QA, COL_KA, COL_VA = 0, NA_WIDTH, 2 * NA_WIDTH
COL_QR = 3 * NA_WIDTH
COL_KR = COL_QR + RET_WIDTH
COL_VR = COL_KR + RET_WIDTH
COL_GR = COL_VR + RET_WIDTH

LANES = 128
NEG = -1e30
VMEM_LIMIT = 56 * 1024 * 1024

F32 = jnp.float32
BF16 = jnp.bfloat16


def _layer_norm_rows(x, g, b):
    mu = jnp.mean(x, axis=-1, keepdims=True)
    xc = x - mu
    var = jnp.mean(xc * xc, axis=-1, keepdims=True)
    return xc * lax.rsqrt(var + LN_EPS) * g + b


def _inproj_kernel(x_ref, g_ref, b_ref, w_ref, c_ref, s_ref, o_ref, hb_ref, *, tn, rb):
    j = pl.program_id(1)
    tm = x_ref.shape[0]

    col0 = j * tn
    rotary = jnp.logical_and(col0 >= COL_QR, col0 < COL_VR)

    @pl.when(j == 0)
    def _():
        for sb in range(tm // rb):
            rows = slice(sb * rb, (sb + 1) * rb)
            hb_ref[rows, :] = _layer_norm_rows(x_ref[rows, :], g_ref[...], b_ref[...]).astype(BF16)
            o_ref[rows, :] = jnp.dot(hb_ref[rows, :], w_ref[...], preferred_element_type=F32).astype(BF16)

    @pl.when(jnp.logical_and(j != 0, jnp.logical_not(rotary)))
    def _():
        o_ref[...] = jnp.dot(hb_ref[...], w_ref[...], preferred_element_type=F32).astype(BF16)

    @pl.when(rotary)
    def _():
        acc = jnp.dot(hb_ref[...], w_ref[...], preferred_element_type=F32)
        c = c_ref[...]
        s = s_ref[...]
        for t in range(tn // LANES):
            xs = acc[:, t * LANES:(t + 1) * LANES]
            o_ref[:, t * LANES:(t + 1) * LANES] = (xs * c + pltpu.roll(xs, LANES // 2, 1) * s).astype(BF16)


def _inproj(x2d, ln_g, ln_b, w_bf, tabs, n_seq, tm, tn=512):
    T = x2d.shape[0]
    assert T % tm == 0 and n_seq % tm == 0 and IN_WIDTH % tn == 0 and RET_WIDTH % tn == 0
    blocks_per_seq = n_seq // tm

    def tab_index(i, j):
        col0 = j * tn
        is_k = jnp.logical_and(col0 >= COL_KR, col0 < COL_VR)
        return (jnp.where(is_k, 1, 0), i % blocks_per_seq, 0)

    tab_spec = pl.BlockSpec((None, tm, LANES), tab_index)
    rb = min(tm, 128)
    return pl.pallas_call(
        functools.partial(_inproj_kernel, tn=tn, rb=rb),
        grid=(T // tm, IN_WIDTH // tn),
        in_specs=[
            pl.BlockSpec((tm, D_MODEL), lambda i, j: (i, 0)),
            pl.BlockSpec((1, D_MODEL), lambda i, j: (0, 0)),
            pl.BlockSpec((1, D_MODEL), lambda i, j: (0, 0)),
            pl.BlockSpec((D_MODEL, tn), lambda i, j: (0, j)),
            tab_spec, tab_spec,
        ],
        out_specs=pl.BlockSpec((tm, tn), lambda i, j: (i, j)),
        out_shape=jax.ShapeDtypeStruct((T, IN_WIDTH), BF16),
        scratch_shapes=[pltpu.VMEM((tm, D_MODEL), BF16)],
        compiler_params=pltpu.CompilerParams(
            dimension_semantics=("arbitrary", "arbitrary"), vmem_limit_bytes=VMEM_LIMIT),
        name="inproj",
    )(x2d, ln_g, ln_b, w_bf, *tabs)


def _na_kernel(q_ref, k_ref, v_ref, km_ref, vm_ref, bias_ref, o_ref, *, rows, unroll):
    lane = lax.broadcasted_iota(jnp.int32, (1, LANES), 1)
    scale = NA_HEAD_DIM ** -0.5
    qmask = [jnp.where((lane // NA_HEAD_DIM) == hh, scale, 0.0).astype(BF16) for hh in range(2)]
    meta_mask = jnp.where(lane < N_META, 0.0, NEG).astype(F32)
    km = km_ref[...]
    vm = vm_ref[...]
    nt = (((1,), (1,)), ((), ()))

    def scores(r):
        rs = jnp.clip(r - NA_WIN_H // 2, 0, rows - NA_WIN_H)
        off = r - rs
        qsl = pl.ds(pl.multiple_of(r * GRID_W, GRID_W), GRID_W)
        bsl = pl.ds(pl.multiple_of(rs * GRID_W, GRID_W), NA_BAND)
        q = q_ref[qsl, :]
        q2 = jnp.concatenate([q * qmask[0], q * qmask[1]], axis=0)
        s = lax.dot_general(q2, k_ref[bsl, :], nt, preferred_element_type=F32) + bias_ref[0, pl.ds(off, 1)][0]
        sm = lax.dot_general(q2, km, nt, preferred_element_type=F32) + meta_mask
        return jnp.concatenate([s, sm], axis=1), qsl, bsl

    def finish(s_all, qsl, bsl):
        m = jnp.max(s_all, axis=-1, keepdims=True)
        p = jnp.exp(s_all - m)
        den = jnp.sum(p, axis=-1, keepdims=True)
        pb = p.astype(BF16)
        o2 = (jnp.dot(pb[:, :NA_BAND], v_ref[bsl, :], preferred_element_type=F32)
              + jnp.dot(pb[:, NA_BAND:], vm, preferred_element_type=F32)) / den
        o_ref[qsl, :] = jnp.where(lane < NA_HEAD_DIM, o2[:GRID_W], o2[GRID_W:]).astype(BF16)

    def group(g, c):
        staged = [scores(g * unroll + u) for u in range(unroll)]
        for st in staged:
            finish(*st)
        return c

    lax.fori_loop(0, rows // unroll, group, 0)


def _na_bias(rpb):
    cols = np.arange(GRID_W)
    cs = np.clip(cols - NA_WIN_W // 2, 0, GRID_W - NA_WIN_W)
    kc = np.arange(GRID_W)
    valid = (kc[None, :] >= cs[:, None]) & (kc[None, :] < cs[:, None] + NA_WIN_W)
    dc = np.clip(kc[None, :] - cols[:, None] + (NA_WIN_W - 1), 0, 2 * NA_WIN_W - 2)
    off = np.arange(NA_WIN_H)
    ki = np.arange(NA_WIN_H)
    dr = ki[None, :] - off[:, None] + (NA_WIN_H - 1)
    b = rpb[:, dr][:, :, :, dc]
    b = jnp.where(valid[None, None, None], b.astype(F32), NEG)
    b = jnp.transpose(b, (0, 1, 3, 2, 4))
    b = b.reshape(NA_HEADS // 2, 2, NA_WIN_H, GRID_W, NA_BAND)
    return jnp.transpose(b, (0, 2, 1, 3, 4)).reshape(NA_HEADS // 2, NA_WIN_H, 2 * GRID_W, NA_BAND)


def _na(proj, proj_meta_pad, bias, B, n):
    rows = n // GRID_W
    assert rows >= NA_WIN_H
    pairs = NA_WIDTH // LANES
    seq_spec = lambda c0: pl.BlockSpec((n, LANES), lambda hp, b: (b, c0 // LANES + hp))
    meta_spec = lambda c0: pl.BlockSpec((LANES, LANES), lambda hp, b: (0, c0 // LANES + hp))
    return pl.pallas_call(
        functools.partial(_na_kernel, rows=rows, unroll=8),
        grid=(pairs, B),
        in_specs=[
            seq_spec(COL_QA), seq_spec(COL_KA), seq_spec(COL_VA),
            meta_spec(COL_KA), meta_spec(COL_VA),
            pl.BlockSpec((1, NA_WIN_H, 2 * GRID_W, NA_BAND), lambda hp, b: (hp, 0, 0, 0)),
        ],
        out_specs=pl.BlockSpec((n, LANES), lambda hp, b: (b, hp)),
        out_shape=jax.ShapeDtypeStruct((B * n, NA_WIDTH), BF16),
        compiler_params=pltpu.CompilerParams(
            dimension_semantics=("arbitrary", "arbitrary"), vmem_limit_bytes=VMEM_LIMIT),
        name="na_attn",
    )(proj, proj, proj, proj_meta_pad, proj_meta_pad, bias)


def _ret_kernel(q_ref, k_ref, v_ref, g_ref, km_ref, vm_ref, lgf_ref, lgb_ref, gn_ref, o_ref,
                ds_ref, sf_ref, sb_ref, *, n_chunks, group):
    lgf = lgf_ref[0]
    lgb = lgb_ref[0]
    ii = lax.broadcasted_iota(jnp.int32, (CHUNK, CHUNK), 0).astype(F32)
    jj = lax.broadcasted_iota(jnp.int32, (CHUNK, CHUNK), 1).astype(F32)
    diff = ii - jj
    dmat = jnp.where(diff >= 0, jnp.exp(jnp.maximum(diff, 0.0) * lgf), jnp.exp(jnp.maximum(-diff, 0.0) * lgb))
    xi_f = jnp.exp((ii + 1.0) * lgf)
    zeta_f = jnp.exp((CHUNK - 1.0 - ii) * lgf)
    cd_f = jnp.exp(CHUNK * lgf)
    xi_b = jnp.exp((CHUNK - ii) * lgb)
    zeta_b = jnp.exp(ii * lgb)
    cd_b = jnp.exp(CHUNK * lgb)
    nt = (((1,), (1,)), ((), ()))
    tn_dims = (((0,), (0,)), ((), ()))
    gn = gn_ref[...]

    def chunk_slice(ch):
        return pl.ds(pl.multiple_of(ch * CHUNK, CHUNK), CHUNK)

    def increments(k, v):
        kf = k.astype(F32)
        kz = jnp.concatenate([(kf * zeta_f).astype(BF16), (kf * zeta_b).astype(BF16)], axis=1)
        return lax.dot_general(kz, v, tn_dims, preferred_element_type=F32)

    def stage1(gi, c):
        for u in range(group):
            ch = gi * group + u
            sl = chunk_slice(ch)
            ds_ref[ch] = increments(k_ref[sl, :], v_ref[sl, :])
        return c

    lax.fori_loop(0, n_chunks // group, stage1, 0)

    def scan_f(ch, S):
        sf_ref[ch] = S.astype(BF16)
        return S * cd_f + ds_ref[ch, :CHUNK, :]

    def scan_b(t, S):
        ch = n_chunks - 1 - t
        sb_ref[ch] = S.astype(BF16)
        return S * cd_b + ds_ref[ch, CHUNK:, :]

    S0 = increments(km_ref[...], vm_ref[...])[:CHUNK]
    lax.fori_loop(0, n_chunks, scan_f, S0, unroll=4)
    lax.fori_loop(0, n_chunks, scan_b, jnp.zeros((CHUNK, CHUNK), F32), unroll=4)

    def scores(ch):
        sl = chunk_slice(ch)
        q = q_ref[sl, :]
        s = lax.dot_general(q, k_ref[sl, :], nt, preferred_element_type=F32) * dmat
        qf = q.astype(F32)
        return jnp.concatenate([s.astype(BF16), (qf * xi_f).astype(BF16), (qf * xi_b).astype(BF16)], axis=1)

    def mix(ch, lhs):
        rhs = jnp.concatenate([v_ref[chunk_slice(ch), :], sf_ref[ch], sb_ref[ch]], axis=0)
        return jnp.dot(lhs, rhs, preferred_element_type=F32)

    def finish(ch, o):
        sl = chunk_slice(ch)
        mu = jnp.mean(o, axis=-1, keepdims=True)
        oc = o - mu
        var = jnp.mean(oc * oc, axis=-1, keepdims=True)
        on = oc * lax.rsqrt(var + LN_EPS) * gn
        g = g_ref[sl, :].astype(F32)
        o_ref[sl, :] = (g / (1.0 + jnp.exp(-g)) * on).astype(BF16)

    def stage3(gi, c):
        chs = [gi * group + u for u in range(group)]
        lhs = [scores(ch) for ch in chs]
        outs = [mix(ch, l) for ch, l in zip(chs, lhs)]
        for ch, o in zip(chs, outs):
            finish(ch, o)
        return c

    lax.fori_loop(0, n_chunks // group, stage3, 0)


def _retention(proj, proj_meta_pad, lgf, lgb, gn_g, B, n):
    n_chunks, group = n // CHUNK, 4
    assert n % CHUNK == 0 and n_chunks % group == 0
    seq_spec = lambda c0: pl.BlockSpec((n, LANES), lambda b, h: (b, c0 // LANES + h))
    meta_spec = lambda c0: pl.BlockSpec((CHUNK, LANES), lambda b, h: (0, c0 // LANES + h))
    lg_spec = pl.BlockSpec((1, 1, LANES), lambda b, h: (h, 0, 0))
    return pl.pallas_call(
        functools.partial(_ret_kernel, n_chunks=n_chunks, group=group),
        grid=(B, RET_HEADS),
        in_specs=[
            seq_spec(COL_QR), seq_spec(COL_KR), seq_spec(COL_VR), seq_spec(COL_GR),
            meta_spec(COL_KR), meta_spec(COL_VR),
            lg_spec, lg_spec,
            pl.BlockSpec((1, LANES), lambda b, h: (0, h)),
        ],
        out_specs=pl.BlockSpec((n, LANES), lambda b, h: (b, h)),
        out_shape=jax.ShapeDtypeStruct((B * n, RET_WIDTH), BF16),
        scratch_shapes=[pltpu.VMEM((n_chunks, 2 * CHUNK, LANES), F32),
                        pltpu.VMEM((n_chunks, CHUNK, LANES), BF16),
                        pltpu.VMEM((n_chunks, CHUNK, LANES), BF16)],
        compiler_params=pltpu.CompilerParams(
            dimension_semantics=("arbitrary", "arbitrary"), vmem_limit_bytes=VMEM_LIMIT),
        name="retention",
    )(proj, proj, proj, proj, proj_meta_pad, proj_meta_pad, lgf, lgb, gn_g)


def _outproj_kernel(ona_ref, oret_ref, x_ref, w_ref, gin_ref, bin_ref, g1_ref, b1_ref, o_ref, *, rb):
    tm = x_ref.shape[0]

    def norm_store(rows, mix):
        h = _layer_norm_rows(x_ref[rows, :], gin_ref[...], bin_ref[...])
        o_ref[rows, :] = _layer_norm_rows(ALPHA * h + mix, g1_ref[...], b1_ref[...])

    pending = None
    for sb in range(tm // rb):
        rows = slice(sb * rb, (sb + 1) * rb)
        lhs = jnp.concatenate([ona_ref[rows, :], oret_ref[rows, :]], axis=1)
        mix = jnp.dot(lhs, w_ref[...], preferred_element_type=F32)
        if pending is not None:
            norm_store(*pending)
        pending = (rows, mix)
    norm_store(*pending)


def _outproj(o_na, o_ret, x2d, w_bf, gin, bin_, g1, b1, tm=512):
    T = x2d.shape[0]
    assert T % tm == 0
    row_vec = pl.BlockSpec((1, D_MODEL), lambda i: (0, 0))
    return pl.pallas_call(
        functools.partial(_outproj_kernel, rb=128),
        grid=(T // tm,),
        in_specs=[
            pl.BlockSpec((tm, NA_WIDTH), lambda i: (i, 0)),
            pl.BlockSpec((tm, RET_WIDTH), lambda i: (i, 0)),
            pl.BlockSpec((tm, D_MODEL), lambda i: (i, 0)),
            pl.BlockSpec((D_MODEL, D_MODEL), lambda i: (0, 0)),
            row_vec, row_vec, row_vec, row_vec,
        ],
        out_specs=pl.BlockSpec((tm, D_MODEL), lambda i: (i, 0)),
        out_shape=jax.ShapeDtypeStruct((T, D_MODEL), F32),
        compiler_params=pltpu.CompilerParams(dimension_semantics=("arbitrary",), vmem_limit_bytes=VMEM_LIMIT),
        name="outproj_ln1",
    )(o_na, o_ret, x2d, w_bf, gin, bin_, g1, b1)


def _ffn_kernel(h_ref, wg_ref, wu_ref, wd_ref, g2_ref, b2_ref, o_ref, hb_ref, acc_ref, *, rb):
    j = pl.program_id(1)
    tm = h_ref.shape[0]

    @pl.when(j == 0)
    def _():
        hb_ref[...] = h_ref[...].astype(BF16)
        acc_ref[...] = jnp.zeros_like(acc_ref)

    hb = hb_ref[...]
    g = jnp.dot(hb, wg_ref[...], preferred_element_type=F32)
    u = jnp.dot(hb, wu_ref[...], preferred_element_type=F32)
    a = (g / (1.0 + jnp.exp(-g)) * u).astype(BF16)
    acc_ref[...] += jnp.dot(a, wd_ref[...], preferred_element_type=F32)

    @pl.when(j == pl.num_programs(1) - 1)
    def _():
        def rows(i, c):
            sl = pl.ds(pl.multiple_of(i * rb, rb), rb)
            o_ref[sl, :] = _layer_norm_rows(ALPHA * h_ref[sl, :] + acc_ref[sl, :], g2_ref[...], b2_ref[...])
            return c

        lax.fori_loop(0, tm // rb, rows, 0)


def _ffn(h1, wg_bf, wu_bf, wd_bf, g2, b2, tm=512, tf=512):
    T = h1.shape[0]
    assert T % tm == 0 and D_FF % tf == 0
    row_vec = pl.BlockSpec((1, D_MODEL), lambda i, j: (0, 0))
    return pl.pallas_call(
        functools.partial(_ffn_kernel, rb=128),
        grid=(T // tm, D_FF // tf),
        in_specs=[
            pl.BlockSpec((tm, D_MODEL), lambda i, j: (i, 0)),
            pl.BlockSpec((D_MODEL, tf), lambda i, j: (0, j)),
            pl.BlockSpec((D_MODEL, tf), lambda i, j: (0, j)),
            pl.BlockSpec((tf, D_MODEL), lambda i, j: (j, 0)),
            row_vec, row_vec,
        ],
        out_specs=pl.BlockSpec((tm, D_MODEL), lambda i, j: (i, 0)),
        out_shape=jax.ShapeDtypeStruct((T, D_MODEL), F32),
        scratch_shapes=[pltpu.VMEM((tm, D_MODEL), BF16), pltpu.VMEM((tm, D_MODEL), F32)],
        compiler_params=pltpu.CompilerParams(
            dimension_semantics=("arbitrary", "arbitrary"), vmem_limit_bytes=VMEM_LIMIT),
        name="ffn_ln2",
    )(h1, wg_bf, wu_bf, wd_bf, g2, b2)


def _rope_tables(pos):
    half = RET_HEAD_DIM // 2
    inv = ROPE_BASE ** (-jnp.arange(half, dtype=F32) / half)
    ang = pos[:, None] * inv[None, :]
    c, s = jnp.cos(ang), jnp.sin(ang)
    cq = jnp.concatenate([c, c], axis=-1)
    sq = jnp.concatenate([-s, s], axis=-1)
    ks = RET_HEAD_DIM ** -0.5
    return (jnp.stack([cq, cq * ks]), jnp.stack([sq, sq * ks]))


def _pick_tm(n, cap):
    tm = cap
    while n % tm:
        tm //= 2
    return tm


def kernel(x_prompt, x_sample, meta_tokens, ln_in_g, ln_in_b, w_in, na_rpb, ret_decay_f, ret_decay_b,
           ret_gn_g, w_out, ln1_g, ln1_b, w_ffn_gate, w_ffn_up, w_ffn_down, ln2_g, ln2_b):
    row = lambda v: v.reshape(1, -1).astype(F32)
    gin, bin_ = row(ln_in_g), row(ln_in_b)
    w_in_bf = w_in[0].astype(BF16)
    w_out_bf = w_out[0].astype(BF16)
    wg_bf, wu_bf, wd_bf = w_ffn_gate[0].astype(BF16), w_ffn_up[0].astype(BF16), w_ffn_down[0].astype(BF16)
    bias = _na_bias(na_rpb[0])
    lg = lambda d: jnp.broadcast_to(jax.nn.log_sigmoid(d[0].astype(F32))[:, None, None], (RET_HEADS, 1, LANES))
    lgf, lgb = lg(ret_decay_f), lg(ret_decay_b)
    gn_g = row(ret_gn_g[0])

    meta_tabs = _rope_tables(jnp.arange(N_META, dtype=F32))
    proj_meta = _inproj(meta_tokens.astype(F32), gin, bin_, w_in_bf, meta_tabs, N_META, N_META)
    proj_meta_pad = jnp.pad(proj_meta, ((CHUNK - N_META, 0), (0, 0)))

    def group(x):
        B, n, _ = x.shape
        x2d = x.reshape(B * n, D_MODEL)
        tabs = _rope_tables(jnp.arange(n, dtype=F32) + float(N_META))
        proj = _inproj(x2d, gin, bin_, w_in_bf, tabs, n, _pick_tm(n, 1024))
        o_na = _na(proj, _na_meta(proj_meta), bias, B, n)
        o_ret = _retention(proj, proj_meta_pad, lgf, lgb, gn_g, B, n)
        h1 = _outproj(o_na, o_ret, x2d, w_out_bf, gin, bin_, row(ln1_g[0]), row(ln1_b[0]), tm=_pick_tm(B * n, 512))
        y = _ffn(h1, wg_bf, wu_bf, wd_bf, row(ln2_g[0]), row(ln2_b[0]), tm=_pick_tm(B * n, 512))
        return y.reshape(B, n, D_MODEL)

    return (group(x_prompt), group(x_sample))


def _na_meta(proj_meta):
    return jnp.pad(proj_meta, ((0, LANES - N_META), (0, 0)))
```

```python
import functools

import jax
import jax.numpy as jnp
import numpy as np
from jax import lax
from jax.experimental import pallas as pl
from jax.experimental.pallas import tpu as pltpu

D_MODEL = 2048
N_META = 16
GRID_W = 64
NA_HEADS = 16
NA_HEAD_DIM = 64
NA_WIDTH = NA_HEADS * NA_HEAD_DIM
NA_WIN_H = 8
NA_WIN_W = 16
NA_HALF = GRID_W // 2
NA_KEY_COLS = 48
NA_KEY_SHIFT = GRID_W - NA_KEY_COLS
NA_KEYS = 512
RET_HEADS = 8
RET_HEAD_DIM = 128
RET_WIDTH = RET_HEADS * RET_HEAD_DIM
CHUNK = 128
ROPE_BASE = 10000.0
IN_WIDTH = 3 * NA_WIDTH + 4 * RET_WIDTH
D_FF = 5632
LN_EPS = 1e-5
ALPHA = 2.0 ** 0.25

COL_QA, COL_KA, COL_VA = 0, NA_WIDTH, 2 * NA_WIDTH
COL_QR = 3 * NA_WIDTH
COL_KR = COL_QR + RET_WIDTH
COL_VR = COL_KR + RET_WIDTH
COL_GR = COL_VR + RET_WIDTH

LANES = 128
NEG = -1e30
VMEM_LIMIT = 56 * 1024 * 1024

F32 = jnp.float32
BF16 = jnp.bfloat16


def _layer_norm_rows(x, g, b):
    mu = jnp.mean(x, axis=-1, keepdims=True)
    xc = x - mu
    var = jnp.mean(xc * xc, axis=-1, keepdims=True)
    return xc * lax.rsqrt(var + LN_EPS) * g + b


def _inproj_kernel(x_ref, g_ref, b_ref, w_ref, c_ref, s_ref, o_ref, hb_ref, *, tn, rb):
    j = pl.program_id(1)
    tm = x_ref.shape[0]

    col0 = j * tn
    rotary = jnp.logical_and(col0 >= COL_QR, col0 < COL_VR)

    @pl.when(j == 0)
    def _():
        for sb in range(tm // rb):
            rows = slice(sb * rb, (sb + 1) * rb)
            hb_ref[rows, :] = _layer_norm_rows(x_ref[rows, :], g_ref[...], b_ref[...]).astype(BF16)
            o_ref[rows, :] = jnp.dot(hb_ref[rows, :], w_ref[...], preferred_element_type=F32).astype(BF16)

    @pl.when(jnp.logical_and(j != 0, jnp.logical_not(rotary)))
    def _():
        o_ref[...] = jnp.dot(hb_ref[...], w_ref[...], preferred_element_type=F32).astype(BF16)

    @pl.when(rotary)
    def _():
        acc = jnp.dot(hb_ref[...], w_ref[...], preferred_element_type=F32)
        c = c_ref[...]
        s = s_ref[...]
        for t in range(tn // LANES):
            xs = acc[:, t * LANES:(t + 1) * LANES]
            o_ref[:, t * LANES:(t + 1) * LANES] = (xs * c + pltpu.roll(xs, LANES // 2, 1) * s).astype(BF16)


def _inproj(x2d, ln_g, ln_b, w_bf, tabs, n_seq, tm, tn=512):
    T = x2d.shape[0]
    assert T % tm == 0 and n_seq % tm == 0 and IN_WIDTH % tn == 0 and RET_WIDTH % tn == 0
    blocks_per_seq = n_seq // tm

    def tab_index(i, j):
        col0 = j * tn
        is_k = jnp.logical_and(col0 >= COL_KR, col0 < COL_VR)
        return (jnp.where(is_k, 1, 0), i % blocks_per_seq, 0)

    tab_spec = pl.BlockSpec((None, tm, LANES), tab_index)
    rb = min(tm, 128)
    return pl.pallas_call(
        functools.partial(_inproj_kernel, tn=tn, rb=rb),
        grid=(T // tm, IN_WIDTH // tn),
        in_specs=[
            pl.BlockSpec((tm, D_MODEL), lambda i, j: (i, 0)),
            pl.BlockSpec((1, D_MODEL), lambda i, j: (0, 0)),
            pl.BlockSpec((1, D_MODEL), lambda i, j: (0, 0)),
            pl.BlockSpec((D_MODEL, tn), lambda i, j: (0, j)),
            tab_spec, tab_spec,
        ],
        out_specs=pl.BlockSpec((tm, tn), lambda i, j: (i, j)),
        out_shape=jax.ShapeDtypeStruct((T, IN_WIDTH), BF16),
        scratch_shapes=[pltpu.VMEM((tm, D_MODEL), BF16)],
        compiler_params=pltpu.CompilerParams(
            dimension_semantics=("arbitrary", "arbitrary"), vmem_limit_bytes=VMEM_LIMIT),
        name="inproj",
    )(x2d, ln_g, ln_b, w_bf, *tabs)


def _na_kernel(q_ref, k_ref, v_ref, km_ref, vm_ref, bias_ref, o_ref, *, rows, unroll, lead):
    lane = lax.broadcasted_iota(jnp.int32, (1, LANES), 1)
    scale = NA_HEAD_DIM ** -0.5
    qmask = [jnp.where((lane // NA_HEAD_DIM) == hh, scale, 0.0).astype(BF16) for hh in range(2)]
    km = km_ref[...]
    vm = vm_ref[...]
    nt = (((1,), (1,)), ((), ()))

    def window(ref, meta, rs, hf):
        parts = [ref[pl.ds(pl.multiple_of((rs + i) * GRID_W + NA_KEY_SHIFT * hf, NA_KEY_SHIFT), NA_KEY_COLS), :]
                 for i in range(NA_WIN_H)]
        return jnp.concatenate(parts + [meta], axis=0)

    units = [(u, hf) for u in range(unroll) for hf in range(2)]
    n_groups = rows // unroll

    def locate(g, idx):
        u, hf = units[idx]
        r = g * unroll + u
        rs = jnp.clip(r - NA_WIN_H // 2, 0, rows - NA_WIN_H)
        qsl = pl.ds(pl.multiple_of(r * GRID_W + NA_HALF * hf, NA_HALF), NA_HALF)
        return r - rs, qsl, rs, hf

    def scores(g, idx):
        off, qsl, rs, hf = locate(g, idx)
        q = q_ref[qsl, :]
        q2 = jnp.concatenate([q * qmask[0], q * qmask[1]], axis=0)
        s = lax.dot_general(q2, window(k_ref, km, rs, hf), nt, preferred_element_type=F32)
        return s + bias_ref[0, hf, pl.ds(off, 1)][0]

    def finish(s, g, idx):
        _, qsl, rs, hf = locate(g, idx)
        m = jnp.max(s, axis=-1, keepdims=True)
        p = jnp.exp(s - m)
        den = jnp.sum(p, axis=-1, keepdims=True)
        o2 = jnp.dot(p.astype(BF16), window(v_ref, vm, rs, hf), preferred_element_type=F32) / den
        o_ref[qsl, :] = jnp.where(lane < NA_HEAD_DIM, o2[:NA_HALF], o2[NA_HALF:]).astype(BF16)

    def body(g, c):
        staged = [scores(g, idx) for idx in range(lead)]
        for idx in range(len(units)):
            if idx + lead < len(units):
                staged.append(scores(g, idx + lead))
            finish(staged[idx], g, idx)
        return c

    lax.fori_loop(0, n_groups, body, 0)


def _na_bias(rpb):
    cols = np.arange(GRID_W)
    cs = np.clip(cols - NA_WIN_W // 2, 0, GRID_W - NA_WIN_W)
    kc = np.arange(GRID_W)
    valid = (kc[None, :] >= cs[:, None]) & (kc[None, :] < cs[:, None] + NA_WIN_W)
    dc = np.clip(kc[None, :] - cols[:, None] + (NA_WIN_W - 1), 0, 2 * NA_WIN_W - 2)
    t = jnp.where(valid[None, None], rpb.astype(F32)[:, :, dc], NEG)
    halves = []
    for hf in range(2):
        th = t[:, :, NA_HALF * hf:NA_HALF * (hf + 1), NA_KEY_SHIFT * hf:NA_KEY_SHIFT * hf + NA_KEY_COLS]
        per_off = [jnp.transpose(th[:, NA_WIN_H - 1 - off:2 * NA_WIN_H - 1 - off], (0, 2, 1, 3))
                   .reshape(NA_HEADS, NA_HALF, NA_WIN_H * NA_KEY_COLS) for off in range(NA_WIN_H)]
        halves.append(jnp.stack(per_off, axis=1))
    b = jnp.stack(halves, axis=1)
    meta = jnp.asarray(np.where(np.arange(LANES) < N_META, 0.0, NEG), F32)
    b = jnp.concatenate([b, jnp.broadcast_to(meta, b.shape[:-1] + (LANES,))], axis=-1)
    b = b.reshape(NA_HEADS // 2, 2, 2, NA_WIN_H, NA_HALF, NA_KEYS)
    return jnp.transpose(b, (0, 2, 3, 1, 4, 5)).reshape(NA_HEADS // 2, 2, NA_WIN_H, 2 * NA_HALF, NA_KEYS)


def _na(proj, proj_meta_pad, bias, B, n):
    rows, unroll = n // GRID_W, 8
    assert rows >= NA_WIN_H and rows % unroll == 0
    pairs = NA_WIDTH // LANES
    seq_spec = lambda c0: pl.BlockSpec((n, LANES), lambda hp, b: (b, c0 // LANES + hp))
    meta_spec = lambda c0: pl.BlockSpec((LANES, LANES), lambda hp, b: (0, c0 // LANES + hp))
    return pl.pallas_call(
        functools.partial(_na_kernel, rows=rows, unroll=unroll, lead=2 * unroll),
        grid=(pairs, B),
        in_specs=[
            seq_spec(COL_QA), seq_spec(COL_KA), seq_spec(COL_VA),
            meta_spec(COL_KA), meta_spec(COL_VA),
            pl.BlockSpec((1, 2, NA_WIN_H, 2 * NA_HALF, NA_KEYS), lambda hp, b: (hp, 0, 0, 0, 0)),
        ],
        out_specs=pl.BlockSpec((n, LANES), lambda hp, b: (b, hp)),
        out_shape=jax.ShapeDtypeStruct((B * n, NA_WIDTH), BF16),
        compiler_params=pltpu.CompilerParams(
            dimension_semantics=("arbitrary", "arbitrary"), vmem_limit_bytes=VMEM_LIMIT),
        name="na_attn",
    )(proj, proj, proj, proj_meta_pad, proj_meta_pad, bias)


def _ret_kernel(q_ref, k_ref, v_ref, g_ref, km_ref, vm_ref, lgf_ref, lgb_ref, gn_ref, o_ref,
                ds_ref, sf_ref, sb_ref, *, n_chunks, group):
    lgf = lgf_ref[0]
    lgb = lgb_ref[0]
    ii = lax.broadcasted_iota(jnp.int32, (CHUNK, CHUNK), 0).astype(F32)
    jj = lax.broadcasted_iota(jnp.int32, (CHUNK, CHUNK), 1).astype(F32)
    diff = ii - jj
    dmat = jnp.where(diff >= 0, jnp.exp(jnp.maximum(diff, 0.0) * lgf), jnp.exp(jnp.maximum(-diff, 0.0) * lgb))
    xi_f = jnp.exp((ii + 1.0) * lgf)
    zeta_f = jnp.exp((CHUNK - 1.0 - ii) * lgf)
    cd_f = jnp.exp(CHUNK * lgf)
    xi_b = jnp.exp((CHUNK - ii) * lgb)
    zeta_b = jnp.exp(ii * lgb)
    cd_b = jnp.exp(CHUNK * lgb)
    nt = (((1,), (1,)), ((), ()))
    tn_dims = (((0,), (0,)), ((), ()))
    gn = gn_ref[...]

    def chunk_slice(ch):
        return pl.ds(pl.multiple_of(ch * CHUNK, CHUNK), CHUNK)

    def increments(k, v):
        kf = k.astype(F32)
        kz = jnp.concatenate([(kf * zeta_f).astype(BF16), (kf * zeta_b).astype(BF16)], axis=1)
        return lax.dot_general(kz, v, tn_dims, preferred_element_type=F32)

    def stage1(gi, c):
        for u in range(group):
            ch = gi * group + u
            sl = chunk_slice(ch)
            ds_ref[ch] = increments(k_ref[sl, :], v_ref[sl, :])
        return c

    lax.fori_loop(0, n_chunks // group, stage1, 0)

    def scan_f(ch, S):
        sf_ref[ch] = S.astype(BF16)
        return S * cd_f + ds_ref[ch, :CHUNK, :]

    def scan_b(t, S):
        ch = n_chunks - 1 - t
        sb_ref[ch] = S.astype(BF16)
        return S * cd_b + ds_ref[ch, CHUNK:, :]

    S0 = increments(km_ref[...], vm_ref[...])[:CHUNK]
    lax.fori_loop(0, n_chunks, scan_f, S0, unroll=4)
    lax.fori_loop(0, n_chunks, scan_b, jnp.zeros((CHUNK, CHUNK), F32), unroll=4)

    def scores(ch):
        sl = chunk_slice(ch)
        q = q_ref[sl, :]
        s = lax.dot_general(q, k_ref[sl, :], nt, preferred_element_type=F32) * dmat
        qf = q.astype(F32)
        return jnp.concatenate([s.astype(BF16), (qf * xi_f).astype(BF16), (qf * xi_b).astype(BF16)], axis=1)

    def mix(ch, lhs):
        rhs = jnp.concatenate([v_ref[chunk_slice(ch), :], sf_ref[ch], sb_ref[ch]], axis=0)
        return jnp.dot(lhs, rhs, preferred_element_type=F32)

    def finish(ch, o):
        sl = chunk_slice(ch)
        mu = jnp.mean(o, axis=-1, keepdims=True)
        oc = o - mu
        var = jnp.mean(oc * oc, axis=-1, keepdims=True)
        on = oc * lax.rsqrt(var + LN_EPS) * gn
        g = g_ref[sl, :].astype(F32)
        o_ref[sl, :] = (g / (1.0 + jnp.exp(-g)) * on).astype(BF16)

    def stage3(gi, c):
        chs = [gi * group + u for u in range(group)]
        lhs = [scores(ch) for ch in chs]
        outs = [mix(ch, l) for ch, l in zip(chs, lhs)]
        for ch, o in zip(chs, outs):
            finish(ch, o)
        return c

    lax.fori_loop(0, n_chunks // group, stage3, 0)


def _retention(proj, proj_meta_pad, lgf, lgb, gn_g, B, n):
    n_chunks, group = n // CHUNK, 8
    assert n % CHUNK == 0 and n_chunks % group == 0
    seq_spec = lambda c0: pl.BlockSpec((n, LANES), lambda b, h: (b, c0 // LANES + h))
    meta_spec = lambda c0: pl.BlockSpec((CHUNK, LANES), lambda b, h: (0, c0 // LANES + h))
    lg_spec = pl.BlockSpec((1, 1, LANES), lambda b, h: (h, 0, 0))
    return pl.pallas_call(
        functools.partial(_ret_kernel, n_chunks=n_chunks, group=group),
        grid=(B, RET_HEADS),
        in_specs=[
            seq_spec(COL_QR), seq_spec(COL_KR), seq_spec(COL_VR), seq_spec(COL_GR),
            meta_spec(COL_KR), meta_spec(COL_VR),
            lg_spec, lg_spec,
            pl.BlockSpec((1, LANES), lambda b, h: (0, h)),
        ],
        out_specs=pl.BlockSpec((n, LANES), lambda b, h: (b, h)),
        out_shape=jax.ShapeDtypeStruct((B * n, RET_WIDTH), BF16),
        scratch_shapes=[pltpu.VMEM((n_chunks, 2 * CHUNK, LANES), F32),
                        pltpu.VMEM((n_chunks, CHUNK, LANES), BF16),
                        pltpu.VMEM((n_chunks, CHUNK, LANES), BF16)],
        compiler_params=pltpu.CompilerParams(
            dimension_semantics=("arbitrary", "arbitrary"), vmem_limit_bytes=VMEM_LIMIT),
        name="retention",
    )(proj, proj, proj, proj, proj_meta_pad, proj_meta_pad, lgf, lgb, gn_g)


def _outproj_kernel(ona_ref, oret_ref, x_ref, w_ref, gin_ref, bin_ref, g1_ref, b1_ref, o_ref, *, rb):
    tm = x_ref.shape[0]

    def norm_store(rows, mix):
        h = _layer_norm_rows(x_ref[rows, :], gin_ref[...], bin_ref[...])
        o_ref[rows, :] = _layer_norm_rows(ALPHA * h + mix, g1_ref[...], b1_ref[...])

    pending = None
    for sb in range(tm // rb):
        rows = slice(sb * rb, (sb + 1) * rb)
        lhs = jnp.concatenate([ona_ref[rows, :], oret_ref[rows, :]], axis=1)
        mix = jnp.dot(lhs, w_ref[...], preferred_element_type=F32)
        if pending is not None:
            norm_store(*pending)
        pending = (rows, mix)
    norm_store(*pending)


def _outproj(o_na, o_ret, x2d, w_bf, gin, bin_, g1, b1, tm=512):
    T = x2d.shape[0]
    assert T % tm == 0
    row_vec = pl.BlockSpec((1, D_MODEL), lambda i: (0, 0))
    return pl.pallas_call(
        functools.partial(_outproj_kernel, rb=128),
        grid=(T // tm,),
        in_specs=[
            pl.BlockSpec((tm, NA_WIDTH), lambda i: (i, 0)),
            pl.BlockSpec((tm, RET_WIDTH), lambda i: (i, 0)),
            pl.BlockSpec((tm, D_MODEL), lambda i: (i, 0)),
            pl.BlockSpec((D_MODEL, D_MODEL), lambda i: (0, 0)),
            row_vec, row_vec, row_vec, row_vec,
        ],
        out_specs=pl.BlockSpec((tm, D_MODEL), lambda i: (i, 0)),
        out_shape=jax.ShapeDtypeStruct((T, D_MODEL), F32),
        compiler_params=pltpu.CompilerParams(dimension_semantics=("arbitrary",), vmem_limit_bytes=VMEM_LIMIT),
        name="outproj_ln1",
    )(o_na, o_ret, x2d, w_bf, gin, bin_, g1, b1)


def _ffn_kernel(h_ref, wg_ref, wu_ref, wd_ref, g2_ref, b2_ref, o_ref, hb_ref, acc_ref, *, rb):
    j = pl.program_id(1)
    tm = h_ref.shape[0]

    @pl.when(j == 0)
    def _():
        hb_ref[...] = h_ref[...].astype(BF16)
        acc_ref[...] = jnp.zeros_like(acc_ref)

    hb = hb_ref[...]
    g = jnp.dot(hb, wg_ref[...], preferred_element_type=F32)
    u = jnp.dot(hb, wu_ref[...], preferred_element_type=F32)
    a = (g / (1.0 + jnp.exp(-g)) * u).astype(BF16)
    acc_ref[...] += jnp.dot(a, wd_ref[...], preferred_element_type=F32)

    @pl.when(j == pl.num_programs(1) - 1)
    def _():
        def rows(i, c):
            sl = pl.ds(pl.multiple_of(i * rb, rb), rb)
            o_ref[sl, :] = _layer_norm_rows(ALPHA * h_ref[sl, :] + acc_ref[sl, :], g2_ref[...], b2_ref[...])
            return c

        lax.fori_loop(0, tm // rb, rows, 0)


def _ffn(h1, wg_bf, wu_bf, wd_bf, g2, b2, tm=512, tf=512):
    T = h1.shape[0]
    assert T % tm == 0 and D_FF % tf == 0
    row_vec = pl.BlockSpec((1, D_MODEL), lambda i, j: (0, 0))
    return pl.pallas_call(
        functools.partial(_ffn_kernel, rb=128),
        grid=(T // tm, D_FF // tf),
        in_specs=[
            pl.BlockSpec((tm, D_MODEL), lambda i, j: (i, 0)),
            pl.BlockSpec((D_MODEL, tf), lambda i, j: (0, j)),
            pl.BlockSpec((D_MODEL, tf), lambda i, j: (0, j)),
            pl.BlockSpec((tf, D_MODEL), lambda i, j: (j, 0)),
            row_vec, row_vec,
        ],
        out_specs=pl.BlockSpec((tm, D_MODEL), lambda i, j: (i, 0)),
        out_shape=jax.ShapeDtypeStruct((T, D_MODEL), F32),
        scratch_shapes=[pltpu.VMEM((tm, D_MODEL), BF16), pltpu.VMEM((tm, D_MODEL), F32)],
        compiler_params=pltpu.CompilerParams(
            dimension_semantics=("arbitrary", "arbitrary"), vmem_limit_bytes=VMEM_LIMIT),
        name="ffn_ln2",
    )(h1, wg_bf, wu_bf, wd_bf, g2, b2)


def _rope_tables(pos):
    half = RET_HEAD_DIM // 2
    inv = ROPE_BASE ** (-jnp.arange(half, dtype=F32) / half)
    ang = pos[:, None] * inv[None, :]
    c, s = jnp.cos(ang), jnp.sin(ang)
    cq = jnp.concatenate([c, c], axis=-1)
    sq = jnp.concatenate([-s, s], axis=-1)
    ks = RET_HEAD_DIM ** -0.5
    return (jnp.stack([cq, cq * ks]), jnp.stack([sq, sq * ks]))


def _pick_tm(n, cap):
    tm = cap
    while n % tm:
        tm //= 2
    return tm


def kernel(x_prompt, x_sample, meta_tokens, ln_in_g, ln_in_b, w_in, na_rpb, ret_decay_f, ret_decay_b,
           ret_gn_g, w_out, ln1_g, ln1_b, w_ffn_gate, w_ffn_up, w_ffn_down, ln2_g, ln2_b):
    row = lambda v: v.reshape(1, -1).astype(F32)
    gin, bin_ = row(ln_in_g), row(ln_in_b)
    w_in_bf = w_in[0].astype(BF16)
    w_out_bf = w_out[0].astype(BF16)
    wg_bf, wu_bf, wd_bf = w_ffn_gate[0].astype(BF16), w_ffn_up[0].astype(BF16), w_ffn_down[0].astype(BF16)
    bias = _na_bias(na_rpb[0])
    lg = lambda d: jnp.broadcast_to(jax.nn.log_sigmoid(d[0].astype(F32))[:, None, None], (RET_HEADS, 1, LANES))
    lgf, lgb = lg(ret_decay_f), lg(ret_decay_b)
    gn_g = row(ret_gn_g[0])

    meta_tabs = _rope_tables(jnp.arange(N_META, dtype=F32))
    proj_meta = _inproj(meta_tokens.astype(F32), gin, bin_, w_in_bf, meta_tabs, N_META, N_META)
    proj_meta_pad = jnp.pad(proj_meta, ((CHUNK - N_META, 0), (0, 0)))

    def group(x):
        B, n, _ = x.shape
        x2d = x.reshape(B * n, D_MODEL)
        tabs = _rope_tables(jnp.arange(n, dtype=F32) + float(N_META))
        proj = _inproj(x2d, gin, bin_, w_in_bf, tabs, n, _pick_tm(n, 1024))
        o_na = _na(proj, _na_meta(proj_meta), bias, B, n)
        o_ret = _retention(proj, proj_meta_pad, lgf, lgb, gn_g, B, n)
        h1 = _outproj(o_na, o_ret, x2d, w_out_bf, gin, bin_, row(ln1_g[0]), row(ln1_b[0]), tm=_pick_tm(B * n, 512))
        y = _ffn(h1, wg_bf, wu_bf, wd_bf, row(ln2_g[0]), row(ln2_b[0]), tm=_pick_tm(B * n, 512))
        return y.reshape(B, n, D_MODEL)

    return (group(x_prompt), group(x_sample))


def _na_meta(proj_meta):
    return jnp.pad(proj_meta, ((0, LANES - N_META), (0, 0)))
```

```python
import functools

import jax
import jax.numpy as jnp
import numpy as np
from jax import lax
from jax.experimental import pallas as pl
from jax.experimental.pallas import tpu as pltpu

D_MODEL = 2048
N_META = 16
GRID_W = 64
NA_HEADS = 16
NA_HEAD_DIM = 64
NA_WIDTH = NA_HEADS * NA_HEAD_DIM
NA_WIN_H = 8
NA_WIN_W = 16
NA_HALF = GRID_W // 2
NA_KEY_COLS = 48
NA_KEY_SHIFT = GRID_W - NA_KEY_COLS
NA_KEYS = 512
RET_HEADS = 8
RET_HEAD_DIM = 128
RET_WIDTH = RET_HEADS * RET_HEAD_DIM
CHUNK = 128
ROPE_BASE = 10000.0
IN_WIDTH = 3 * NA_WIDTH + 4 * RET_WIDTH
D_FF = 5632
LN_EPS = 1e-5
ALPHA = 2.0 ** 0.25

COL_QA, COL_KA, COL_VA = 0, NA_WIDTH, 2 * NA_WIDTH
COL_QR = 3 * NA_WIDTH
COL_KR = COL_QR + RET_WIDTH
COL_VR = COL_KR + RET_WIDTH
COL_GR = COL_VR + RET_WIDTH

LANES = 128
NEG = -1e30
VMEM_LIMIT = 56 * 1024 * 1024

F32 = jnp.float32
BF16 = jnp.bfloat16


def _layer_norm_rows(x, g, b):
    mu = jnp.mean(x, axis=-1, keepdims=True)
    xc = x - mu
    var = jnp.mean(xc * xc, axis=-1, keepdims=True)
    return xc * lax.rsqrt(var + LN_EPS) * g + b


def _inproj_kernel(x_ref, g_ref, b_ref, w_ref, c_ref, s_ref, o_ref, h_ref, hb_ref, *, tn, rb):
    j = pl.program_id(1)
    tm = x_ref.shape[0]

    col0 = j * tn
    rotary = jnp.logical_and(col0 >= COL_QR, col0 < COL_VR)

    @pl.when(j == 0)
    def _():
        for sb in range(tm // rb):
            rows = slice(sb * rb, (sb + 1) * rb)
            h = _layer_norm_rows(x_ref[rows, :], g_ref[...], b_ref[...])
            h_ref[rows, :] = h
            hb_ref[rows, :] = h.astype(BF16)
            o_ref[rows, :] = jnp.dot(hb_ref[rows, :], w_ref[...], preferred_element_type=F32).astype(BF16)

    @pl.when(jnp.logical_and(j != 0, jnp.logical_not(rotary)))
    def _():
        o_ref[...] = jnp.dot(hb_ref[...], w_ref[...], preferred_element_type=F32).astype(BF16)

    @pl.when(rotary)
    def _():
        acc = jnp.dot(hb_ref[...], w_ref[...], preferred_element_type=F32)
        c = c_ref[...]
        s = s_ref[...]
        for t in range(tn // LANES):
            xs = acc[:, t * LANES:(t + 1) * LANES]
            o_ref[:, t * LANES:(t + 1) * LANES] = (xs * c + pltpu.roll(xs, LANES // 2, 1) * s).astype(BF16)


def _inproj(x2d, ln_g, ln_b, w_bf, tabs, n_seq, tm, tn=512):
    T = x2d.shape[0]
    assert T % tm == 0 and n_seq % tm == 0 and IN_WIDTH % tn == 0 and RET_WIDTH % tn == 0
    blocks_per_seq = n_seq // tm

    def tab_index(i, j):
        col0 = j * tn
        is_k = jnp.logical_and(col0 >= COL_KR, col0 < COL_VR)
        return (jnp.where(is_k, 1, 0), i % blocks_per_seq, 0)

    tab_spec = pl.BlockSpec((None, tm, LANES), tab_index)
    rb = min(tm, 128)
    return pl.pallas_call(
        functools.partial(_inproj_kernel, tn=tn, rb=rb),
        grid=(T // tm, IN_WIDTH // tn),
        in_specs=[
            pl.BlockSpec((tm, D_MODEL), lambda i, j: (i, 0)),
            pl.BlockSpec((1, D_MODEL), lambda i, j: (0, 0)),
            pl.BlockSpec((1, D_MODEL), lambda i, j: (0, 0)),
            pl.BlockSpec((D_MODEL, tn), lambda i, j: (0, j)),
            tab_spec, tab_spec,
        ],
        out_specs=[pl.BlockSpec((tm, tn), lambda i, j: (i, j)),
                   pl.BlockSpec((tm, D_MODEL), lambda i, j: (i, 0))],
        out_shape=[jax.ShapeDtypeStruct((T, IN_WIDTH), BF16), jax.ShapeDtypeStruct((T, D_MODEL), F32)],
        scratch_shapes=[pltpu.VMEM((tm, D_MODEL), BF16)],
        compiler_params=pltpu.CompilerParams(
            dimension_semantics=("arbitrary", "arbitrary"), vmem_limit_bytes=VMEM_LIMIT),
        name="inproj",
    )(x2d, ln_g, ln_b, w_bf, *tabs)


def _na_kernel(q_ref, k_ref, v_ref, km_ref, vm_ref, bias_ref, o_ref, *, rows, unroll, lead):
    lane = lax.broadcasted_iota(jnp.int32, (1, LANES), 1)
    scale = NA_HEAD_DIM ** -0.5
    qmask = [jnp.where((lane // NA_HEAD_DIM) == hh, scale, 0.0).astype(BF16) for hh in range(2)]
    km = km_ref[...]
    vm = vm_ref[...]
    nt = (((1,), (1,)), ((), ()))

    def window(ref, meta, rs, hf):
        parts = [ref[pl.ds(pl.multiple_of((rs + i) * GRID_W + NA_KEY_SHIFT * hf, NA_KEY_SHIFT), NA_KEY_COLS), :]
                 for i in range(NA_WIN_H)]
        return jnp.concatenate(parts + [meta], axis=0)

    units = [(u, hf) for u in range(unroll) for hf in range(2)]
    n_groups = rows // unroll

    def locate(g, idx):
        u, hf = units[idx]
        r = g * unroll + u
        rs = jnp.clip(r - NA_WIN_H // 2, 0, rows - NA_WIN_H)
        qsl = pl.ds(pl.multiple_of(r * GRID_W + NA_HALF * hf, NA_HALF), NA_HALF)
        return r - rs, qsl, rs, hf

    def scores(g, idx):
        off, qsl, rs, hf = locate(g, idx)
        q = q_ref[qsl, :]
        q2 = jnp.concatenate([q * qmask[0], q * qmask[1]], axis=0)
        s = lax.dot_general(q2, window(k_ref, km, rs, hf), nt, preferred_element_type=F32)
        return s + bias_ref[0, hf, pl.ds(off, 1)][0]

    def finish(s, g, idx):
        _, qsl, rs, hf = locate(g, idx)
        m = jnp.max(s, axis=-1, keepdims=True)
        p = jnp.exp(s - m)
        den = jnp.sum(p, axis=-1, keepdims=True)
        o2 = jnp.dot(p.astype(BF16), window(v_ref, vm, rs, hf), preferred_element_type=F32) / den
        o_ref[qsl, :] = jnp.where(lane < NA_HEAD_DIM, o2[:NA_HALF], o2[NA_HALF:]).astype(BF16)

    def body(g, c):
        staged = [scores(g, idx) for idx in range(lead)]
        for idx in range(len(units)):
            if idx + lead < len(units):
                staged.append(scores(g, idx + lead))
            finish(staged[idx], g, idx)
        return c

    lax.fori_loop(0, n_groups, body, 0)


def _na_bias(rpb):
    cols = np.arange(GRID_W)
    cs = np.clip(cols - NA_WIN_W // 2, 0, GRID_W - NA_WIN_W)
    kc = np.arange(GRID_W)
    valid = (kc[None, :] >= cs[:, None]) & (kc[None, :] < cs[:, None] + NA_WIN_W)
    dc = np.clip(kc[None, :] - cols[:, None] + (NA_WIN_W - 1), 0, 2 * NA_WIN_W - 2)
    t = jnp.where(valid[None, None], rpb.astype(F32)[:, :, dc], NEG)
    halves = []
    for hf in range(2):
        th = t[:, :, NA_HALF * hf:NA_HALF * (hf + 1), NA_KEY_SHIFT * hf:NA_KEY_SHIFT * hf + NA_KEY_COLS]
        per_off = [jnp.transpose(th[:, NA_WIN_H - 1 - off:2 * NA_WIN_H - 1 - off], (0, 2, 1, 3))
                   .reshape(NA_HEADS, NA_HALF, NA_WIN_H * NA_KEY_COLS) for off in range(NA_WIN_H)]
        halves.append(jnp.stack(per_off, axis=1))
    b = jnp.stack(halves, axis=1)
    meta = jnp.asarray(np.where(np.arange(LANES) < N_META, 0.0, NEG), F32)
    b = jnp.concatenate([b, jnp.broadcast_to(meta, b.shape[:-1] + (LANES,))], axis=-1)
    b = b.reshape(NA_HEADS // 2, 2, 2, NA_WIN_H, NA_HALF, NA_KEYS)
    return jnp.transpose(b, (0, 2, 3, 1, 4, 5)).reshape(NA_HEADS // 2, 2, NA_WIN_H, 2 * NA_HALF, NA_KEYS)


def _na(proj, proj_meta_pad, bias, B, n):
    rows, unroll = n // GRID_W, 8
    assert rows >= NA_WIN_H and rows % unroll == 0
    pairs = NA_WIDTH // LANES
    seq_spec = lambda c0: pl.BlockSpec((n, LANES), lambda hp, b: (b, c0 // LANES + hp))
    meta_spec = lambda c0: pl.BlockSpec((LANES, LANES), lambda hp, b: (0, c0 // LANES + hp))
    return pl.pallas_call(
        functools.partial(_na_kernel, rows=rows, unroll=unroll, lead=2 * unroll),
        grid=(pairs, B),
        in_specs=[
            seq_spec(COL_QA), seq_spec(COL_KA), seq_spec(COL_VA),
            meta_spec(COL_KA), meta_spec(COL_VA),
            pl.BlockSpec((1, 2, NA_WIN_H, 2 * NA_HALF, NA_KEYS), lambda hp, b: (hp, 0, 0, 0, 0)),
        ],
        out_specs=pl.BlockSpec((n, LANES), lambda hp, b: (b, hp)),
        out_shape=jax.ShapeDtypeStruct((B * n, NA_WIDTH), BF16),
        compiler_params=pltpu.CompilerParams(
            dimension_semantics=("arbitrary", "arbitrary"), vmem_limit_bytes=VMEM_LIMIT),
        name="na_attn",
    )(proj, proj, proj, proj_meta_pad, proj_meta_pad, bias)


def _ret_kernel(q_ref, k_ref, v_ref, g_ref, km_ref, vm_ref, lgf_ref, lgb_ref, gn_ref, o_ref,
                ds_ref, sf_ref, sb_ref, *, n_chunks, group):
    lgf = lgf_ref[0]
    lgb = lgb_ref[0]
    ii = lax.broadcasted_iota(jnp.int32, (CHUNK, CHUNK), 0).astype(F32)
    jj = lax.broadcasted_iota(jnp.int32, (CHUNK, CHUNK), 1).astype(F32)
    diff = ii - jj
    dmat = jnp.where(diff >= 0, jnp.exp(jnp.maximum(diff, 0.0) * lgf), jnp.exp(jnp.maximum(-diff, 0.0) * lgb))
    xi_f = jnp.exp((ii + 1.0) * lgf)
    zeta_f = jnp.exp((CHUNK - 1.0 - ii) * lgf)
    cd_f = jnp.exp(CHUNK * lgf)
    xi_b = jnp.exp((CHUNK - ii) * lgb)
    zeta_b = jnp.exp(ii * lgb)
    cd_b = jnp.exp(CHUNK * lgb)
    nt = (((1,), (1,)), ((), ()))
    tn_dims = (((0,), (0,)), ((), ()))
    gn = gn_ref[...]

    def chunk_slice(ch):
        return pl.ds(pl.multiple_of(ch * CHUNK, CHUNK), CHUNK)

    def increments(k, v):
        kf = k.astype(F32)
        kz = jnp.concatenate([(kf * zeta_f).astype(BF16), (kf * zeta_b).astype(BF16)], axis=1)
        return lax.dot_general(kz, v, tn_dims, preferred_element_type=F32)

    def stage1(gi, c):
        for u in range(group):
            ch = gi * group + u
            sl = chunk_slice(ch)
            ds_ref[ch] = increments(k_ref[sl, :], v_ref[sl, :])
        return c

    lax.fori_loop(0, n_chunks // group, stage1, 0)

    def scan_f(ch, S):
        sf_ref[ch] = S.astype(BF16)
        return S * cd_f + ds_ref[ch, :CHUNK, :]

    def scan_b(t, S):
        ch = n_chunks - 1 - t
        sb_ref[ch] = S.astype(BF16)
        return S * cd_b + ds_ref[ch, CHUNK:, :]

    S0 = increments(km_ref[...], vm_ref[...])[:CHUNK]
    lax.fori_loop(0, n_chunks, scan_f, S0, unroll=4)
    lax.fori_loop(0, n_chunks, scan_b, jnp.zeros((CHUNK, CHUNK), F32), unroll=4)

    def scores(ch):
        sl = chunk_slice(ch)
        q = q_ref[sl, :]
        s = lax.dot_general(q, k_ref[sl, :], nt, preferred_element_type=F32) * dmat
        qf = q.astype(F32)
        return jnp.concatenate([s.astype(BF16), (qf * xi_f).astype(BF16), (qf * xi_b).astype(BF16)], axis=1)

    def mix(ch, lhs):
        rhs = jnp.concatenate([v_ref[chunk_slice(ch), :], sf_ref[ch], sb_ref[ch]], axis=0)
        return jnp.dot(lhs, rhs, preferred_element_type=F32)

    def finish(ch, o):
        sl = chunk_slice(ch)
        mu = jnp.mean(o, axis=-1, keepdims=True)
        oc = o - mu
        var = jnp.mean(oc * oc, axis=-1, keepdims=True)
        on = oc * lax.rsqrt(var + LN_EPS) * gn
        g = g_ref[sl, :].astype(F32)
        o_ref[sl, :] = (g / (1.0 + jnp.exp(-g)) * on).astype(BF16)

    def stage3(gi, c):
        chs = [gi * group + u for u in range(group)]
        lhs = [scores(ch) for ch in chs]
        outs = [mix(ch, l) for ch, l in zip(chs, lhs)]
        for ch, o in zip(chs, outs):
            finish(ch, o)
        return c

    lax.fori_loop(0, n_chunks // group, stage3, 0)


def _retention(proj, proj_meta_pad, lgf, lgb, gn_g, B, n):
    n_chunks, group = n // CHUNK, 8
    assert n % CHUNK == 0 and n_chunks % group == 0
    seq_spec = lambda c0: pl.BlockSpec((n, LANES), lambda b, h: (b, c0 // LANES + h))
    meta_spec = lambda c0: pl.BlockSpec((CHUNK, LANES), lambda b, h: (0, c0 // LANES + h))
    lg_spec = pl.BlockSpec((1, 1, LANES), lambda b, h: (h, 0, 0))
    return pl.pallas_call(
        functools.partial(_ret_kernel, n_chunks=n_chunks, group=group),
        grid=(B, RET_HEADS),
        in_specs=[
            seq_spec(COL_QR), seq_spec(COL_KR), seq_spec(COL_VR), seq_spec(COL_GR),
            meta_spec(COL_KR), meta_spec(COL_VR),
            lg_spec, lg_spec,
            pl.BlockSpec((1, LANES), lambda b, h: (0, h)),
        ],
        out_specs=pl.BlockSpec((n, LANES), lambda b, h: (b, h)),
        out_shape=jax.ShapeDtypeStruct((B * n, RET_WIDTH), BF16),
        scratch_shapes=[pltpu.VMEM((n_chunks, 2 * CHUNK, LANES), F32),
                        pltpu.VMEM((n_chunks, CHUNK, LANES), BF16),
                        pltpu.VMEM((n_chunks, CHUNK, LANES), BF16)],
        compiler_params=pltpu.CompilerParams(
            dimension_semantics=("arbitrary", "arbitrary"), vmem_limit_bytes=VMEM_LIMIT),
        name="retention",
    )(proj, proj, proj, proj, proj_meta_pad, proj_meta_pad, lgf, lgb, gn_g)


def _outproj_kernel(ona_ref, oret_ref, h_ref, w_ref, g1_ref, b1_ref, o_ref, *, rb):
    tm = h_ref.shape[0]

    def norm_store(rows, mix):
        o_ref[rows, :] = _layer_norm_rows(ALPHA * h_ref[rows, :] + mix, g1_ref[...], b1_ref[...])

    pending = None
    for sb in range(tm // rb):
        rows = slice(sb * rb, (sb + 1) * rb)
        lhs = jnp.concatenate([ona_ref[rows, :], oret_ref[rows, :]], axis=1)
        mix = jnp.dot(lhs, w_ref[...], preferred_element_type=F32)
        if pending is not None:
            norm_store(*pending)
        pending = (rows, mix)
    norm_store(*pending)


def _outproj(o_na, o_ret, h, w_bf, g1, b1, tm=512):
    T = h.shape[0]
    assert T % tm == 0
    row_vec = pl.BlockSpec((1, D_MODEL), lambda i: (0, 0))
    return pl.pallas_call(
        functools.partial(_outproj_kernel, rb=256),
        grid=(T // tm,),
        in_specs=[
            pl.BlockSpec((tm, NA_WIDTH), lambda i: (i, 0)),
            pl.BlockSpec((tm, RET_WIDTH), lambda i: (i, 0)),
            pl.BlockSpec((tm, D_MODEL), lambda i: (i, 0)),
            pl.BlockSpec((D_MODEL, D_MODEL), lambda i: (0, 0)),
            row_vec, row_vec,
        ],
        out_specs=pl.BlockSpec((tm, D_MODEL), lambda i: (i, 0)),
        out_shape=jax.ShapeDtypeStruct((T, D_MODEL), F32),
        compiler_params=pltpu.CompilerParams(dimension_semantics=("arbitrary",), vmem_limit_bytes=VMEM_LIMIT),
        name="outproj_ln1",
    )(o_na, o_ret, h, w_bf, g1, b1)


def _ffn_kernel(h_ref, wg_ref, wu_ref, wd_ref, g2_ref, b2_ref, o_ref, hb_ref, acc_ref, *, rb):
    j = pl.program_id(1)
    tm = h_ref.shape[0]

    @pl.when(j == 0)
    def _():
        hb_ref[...] = h_ref[...].astype(BF16)
        acc_ref[...] = jnp.zeros_like(acc_ref)

    hb = hb_ref[...]
    g = jnp.dot(hb, wg_ref[...], preferred_element_type=F32)
    u = jnp.dot(hb, wu_ref[...], preferred_element_type=F32)
    a = (g / (1.0 + jnp.exp(-g)) * u).astype(BF16)
    acc_ref[...] += jnp.dot(a, wd_ref[...], preferred_element_type=F32)

    @pl.when(j == pl.num_programs(1) - 1)
    def _():
        def rows(i, c):
            sl = pl.ds(pl.multiple_of(i * rb, rb), rb)
            o_ref[sl, :] = _layer_norm_rows(ALPHA * h_ref[sl, :] + acc_ref[sl, :], g2_ref[...], b2_ref[...])
            return c

        lax.fori_loop(0, tm // rb, rows, 0)


def _ffn(h1, wg_bf, wu_bf, wd_bf, g2, b2, tm=512, tf=512):
    T = h1.shape[0]
    assert T % tm == 0 and D_FF % tf == 0
    row_vec = pl.BlockSpec((1, D_MODEL), lambda i, j: (0, 0))
    return pl.pallas_call(
        functools.partial(_ffn_kernel, rb=128),
        grid=(T // tm, D_FF // tf),
        in_specs=[
            pl.BlockSpec((tm, D_MODEL), lambda i, j: (i, 0)),
            pl.BlockSpec((D_MODEL, tf), lambda i, j: (0, j)),
            pl.BlockSpec((D_MODEL, tf), lambda i, j: (0, j)),
            pl.BlockSpec((tf, D_MODEL), lambda i, j: (j, 0)),
            row_vec, row_vec,
        ],
        out_specs=pl.BlockSpec((tm, D_MODEL), lambda i, j: (i, 0)),
        out_shape=jax.ShapeDtypeStruct((T, D_MODEL), F32),
        scratch_shapes=[pltpu.VMEM((tm, D_MODEL), BF16), pltpu.VMEM((tm, D_MODEL), F32)],
        compiler_params=pltpu.CompilerParams(
            dimension_semantics=("arbitrary", "arbitrary"), vmem_limit_bytes=VMEM_LIMIT),
        name="ffn_ln2",
    )(h1, wg_bf, wu_bf, wd_bf, g2, b2)


def _rope_tables(pos):
    half = RET_HEAD_DIM // 2
    inv = ROPE_BASE ** (-jnp.arange(half, dtype=F32) / half)
    ang = pos[:, None] * inv[None, :]
    c, s = jnp.cos(ang), jnp.sin(ang)
    cq = jnp.concatenate([c, c], axis=-1)
    sq = jnp.concatenate([-s, s], axis=-1)
    ks = RET_HEAD_DIM ** -0.5
    return (jnp.stack([cq, cq * ks]), jnp.stack([sq, sq * ks]))


def _pick_tm(n, cap):
    tm = cap
    while n % tm:
        tm //= 2
    return tm


def kernel(x_prompt, x_sample, meta_tokens, ln_in_g, ln_in_b, w_in, na_rpb, ret_decay_f, ret_decay_b,
           ret_gn_g, w_out, ln1_g, ln1_b, w_ffn_gate, w_ffn_up, w_ffn_down, ln2_g, ln2_b):
    row = lambda v: v.reshape(1, -1).astype(F32)
    gin, bin_ = row(ln_in_g), row(ln_in_b)
    w_in_bf = w_in[0].astype(BF16)
    w_out_bf = w_out[0].astype(BF16)
    wg_bf, wu_bf, wd_bf = w_ffn_gate[0].astype(BF16), w_ffn_up[0].astype(BF16), w_ffn_down[0].astype(BF16)
    bias = _na_bias(na_rpb[0])
    lg = lambda d: jnp.broadcast_to(jax.nn.log_sigmoid(d[0].astype(F32))[:, None, None], (RET_HEADS, 1, LANES))
    lgf, lgb = lg(ret_decay_f), lg(ret_decay_b)
    gn_g = row(ret_gn_g[0])

    meta_tabs = _rope_tables(jnp.arange(N_META, dtype=F32))
    proj_meta, _ = _inproj(meta_tokens.astype(F32), gin, bin_, w_in_bf, meta_tabs, N_META, N_META)
    proj_meta_pad = jnp.pad(proj_meta, ((CHUNK - N_META, 0), (0, 0)))

    def group(x):
        B, n, _ = x.shape
        x2d = x.reshape(B * n, D_MODEL)
        tabs = _rope_tables(jnp.arange(n, dtype=F32) + float(N_META))
        proj, h = _inproj(x2d, gin, bin_, w_in_bf, tabs, n, _pick_tm(n, 1024))
        o_na = _na(proj, _na_meta(proj_meta), bias, B, n)
        o_ret = _retention(proj, proj_meta_pad, lgf, lgb, gn_g, B, n)
        h1 = _outproj(o_na, o_ret, h, w_out_bf, row(ln1_g[0]), row(ln1_b[0]), tm=_pick_tm(B * n, 512))
        y = _ffn(h1, wg_bf, wu_bf, wd_bf, row(ln2_g[0]), row(ln2_b[0]), tm=_pick_tm(B * n, 512))
        return y.reshape(B, n, D_MODEL)

    return (group(x_prompt), group(x_sample))


def _na_meta(proj_meta):
    return jnp.pad(proj_meta, ((0, LANES - N_META), (0, 0)))
```

```python
import functools

import jax
import jax.numpy as jnp
import numpy as np
from jax import lax
from jax.experimental import pallas as pl
from jax.experimental.pallas import tpu as pltpu

D_MODEL = 2048
N_META = 16
GRID_W = 64
NA_HEADS = 16
NA_HEAD_DIM = 64
NA_WIDTH = NA_HEADS * NA_HEAD_DIM
NA_WIN_H = 8
NA_WIN_W = 16
NA_HALF = GRID_W // 2
NA_KEY_COLS = 48
NA_KEY_SHIFT = GRID_W - NA_KEY_COLS
NA_KEYS = 512
RET_HEADS = 8
RET_HEAD_DIM = 128
RET_WIDTH = RET_HEADS * RET_HEAD_DIM
CHUNK = 128
ROPE_BASE = 10000.0
IN_WIDTH = 3 * NA_WIDTH + 4 * RET_WIDTH
D_FF = 5632
LN_EPS = 1e-5
ALPHA = 2.0 ** 0.25

COL_QA, COL_KA, COL_VA = 0, NA_WIDTH, 2 * NA_WIDTH
COL_QR = 3 * NA_WIDTH
COL_KR = COL_QR + RET_WIDTH
COL_VR = COL_KR + RET_WIDTH
COL_GR = COL_VR + RET_WIDTH

LANES = 128
NEG = -1e30
VMEM_LIMIT = 56 * 1024 * 1024

F32 = jnp.float32
BF16 = jnp.bfloat16


def _layer_norm_rows(x, g, b):
    mu = jnp.mean(x, axis=-1, keepdims=True)
    xc = x - mu
    var = jnp.mean(xc * xc, axis=-1, keepdims=True)
    return xc * lax.rsqrt(var + LN_EPS) * g + b


def _inproj_kernel(x_ref, g_ref, b_ref, w_ref, c_ref, s_ref, o_ref, h_ref, hb_ref, *, tn, rb):
    j = pl.program_id(1)
    tm = x_ref.shape[0]

    col0 = j * tn
    rotary = jnp.logical_and(col0 >= COL_QR, col0 < COL_VR)

    @pl.when(j == 0)
    def _():
        for sb in range(tm // rb):
            rows = slice(sb * rb, (sb + 1) * rb)
            h = _layer_norm_rows(x_ref[rows, :], g_ref[...], b_ref[...])
            h_ref[rows, :] = h
            hb_ref[rows, :] = h.astype(BF16)
            o_ref[rows, :] = jnp.dot(hb_ref[rows, :], w_ref[...], preferred_element_type=F32).astype(BF16)

    @pl.when(jnp.logical_and(j != 0, jnp.logical_not(rotary)))
    def _():
        o_ref[...] = jnp.dot(hb_ref[...], w_ref[...], preferred_element_type=F32).astype(BF16)

    @pl.when(rotary)
    def _():
        acc = jnp.dot(hb_ref[...], w_ref[...], preferred_element_type=F32)
        c = c_ref[...]
        s = s_ref[...]
        for t in range(tn // LANES):
            xs = acc[:, t * LANES:(t + 1) * LANES]
            o_ref[:, t * LANES:(t + 1) * LANES] = (xs * c + pltpu.roll(xs, LANES // 2, 1) * s).astype(BF16)


def _inproj(x2d, ln_g, ln_b, w_bf, tabs, n_seq, tm, tn=512):
    T = x2d.shape[0]
    assert T % tm == 0 and n_seq % tm == 0 and IN_WIDTH % tn == 0 and RET_WIDTH % tn == 0
    blocks_per_seq = n_seq // tm

    def tab_index(i, j):
        col0 = j * tn
        is_k = jnp.logical_and(col0 >= COL_KR, col0 < COL_VR)
        return (jnp.where(is_k, 1, 0), i % blocks_per_seq, 0)

    tab_spec = pl.BlockSpec((None, tm, LANES), tab_index)
    rb = min(tm, 128)
    return pl.pallas_call(
        functools.partial(_inproj_kernel, tn=tn, rb=rb),
        grid=(T // tm, IN_WIDTH // tn),
        in_specs=[
            pl.BlockSpec((tm, D_MODEL), lambda i, j: (i, 0)),
            pl.BlockSpec((1, D_MODEL), lambda i, j: (0, 0)),
            pl.BlockSpec((1, D_MODEL), lambda i, j: (0, 0)),
            pl.BlockSpec((D_MODEL, tn), lambda i, j: (0, j)),
            tab_spec, tab_spec,
        ],
        out_specs=[pl.BlockSpec((tm, tn), lambda i, j: (i, j)),
                   pl.BlockSpec((tm, D_MODEL), lambda i, j: (i, 0))],
        out_shape=[jax.ShapeDtypeStruct((T, IN_WIDTH), BF16), jax.ShapeDtypeStruct((T, D_MODEL), F32)],
        scratch_shapes=[pltpu.VMEM((tm, D_MODEL), BF16)],
        compiler_params=pltpu.CompilerParams(
            dimension_semantics=("arbitrary", "arbitrary"), vmem_limit_bytes=VMEM_LIMIT),
        name="inproj",
    )(x2d, ln_g, ln_b, w_bf, *tabs)


def _na_kernel(q_ref, k_ref, v_ref, km_ref, vm_ref, bias_ref, o_ref, *, rows, unroll, lead):
    lane = lax.broadcasted_iota(jnp.int32, (1, LANES), 1)
    scale = NA_HEAD_DIM ** -0.5
    qmask = [jnp.where((lane // NA_HEAD_DIM) == hh, scale, 0.0).astype(BF16) for hh in range(2)]
    km = km_ref[...]
    vm = vm_ref[...]
    nt = (((1,), (1,)), ((), ()))

    def window(ref, meta, rs, hf):
        parts = [ref[pl.ds(pl.multiple_of((rs + i) * GRID_W + NA_KEY_SHIFT * hf, NA_KEY_SHIFT), NA_KEY_COLS), :]
                 for i in range(NA_WIN_H)]
        return jnp.concatenate(parts + [meta], axis=0)

    units = [(u, hf) for u in range(unroll) for hf in range(2)]
    n_groups = rows // unroll

    def locate(g, idx):
        u, hf = units[idx]
        r = g * unroll + u
        rs = jnp.clip(r - NA_WIN_H // 2, 0, rows - NA_WIN_H)
        qsl = pl.ds(pl.multiple_of(r * GRID_W + NA_HALF * hf, NA_HALF), NA_HALF)
        return r - rs, qsl, rs, hf

    def scores(g, idx):
        off, qsl, rs, hf = locate(g, idx)
        q = q_ref[qsl, :]
        q2 = jnp.concatenate([q * qmask[0], q * qmask[1]], axis=0)
        s = lax.dot_general(q2, window(k_ref, km, rs, hf), nt, preferred_element_type=F32)
        return s + bias_ref[0, hf, pl.ds(off, 1)][0]

    def finish(s, g, idx):
        _, qsl, rs, hf = locate(g, idx)
        m = jnp.max(s, axis=-1, keepdims=True)
        p = jnp.exp(s - m)
        den = jnp.sum(p, axis=-1, keepdims=True)
        o2 = jnp.dot(p.astype(BF16), window(v_ref, vm, rs, hf), preferred_element_type=F32) / den
        o_ref[qsl, :] = jnp.where(lane < NA_HEAD_DIM, o2[:NA_HALF], o2[NA_HALF:]).astype(BF16)

    def body(g, c):
        staged = [scores(g, idx) for idx in range(lead)]
        for idx in range(len(units)):
            if idx + lead < len(units):
                staged.append(scores(g, idx + lead))
            finish(staged[idx], g, idx)
        return c

    lax.fori_loop(0, n_groups, body, 0)


def _na_bias_kernel(rpb_ref, o_ref):
    c32 = lax.broadcasted_iota(jnp.int32, (NA_HALF, LANES), 0)
    lane = lax.broadcasted_iota(jnp.int32, (NA_HALF, LANES), 1)
    meta = jnp.where(lane < N_META, 0.0, NEG).astype(F32)
    tiles = {}
    for hf in range(2):
        c = c32 + NA_HALF * hf
        kc = lane + NA_KEY_SHIFT * hf
        cs = jnp.clip(c - NA_WIN_W // 2, 0, GRID_W - NA_WIN_W)
        valid = (kc >= cs) & (kc < cs + NA_WIN_W) & (lane < NA_KEY_COLS)
        shift = (LANES - (NA_WIN_W - 1) - NA_KEY_SHIFT * hf + NA_HALF * hf) % LANES
        for h in range(2):
            for dr in range(2 * NA_WIN_H - 1):
                w = jnp.broadcast_to(rpb_ref[h, dr:dr + 1, :], (NA_HALF, LANES))
                tiles[(hf, h, dr)] = jnp.where(valid, pltpu.roll(w, shift, 1, stride=1, stride_axis=0), NEG)
    for hf in range(2):
        for off in range(NA_WIN_H):
            heads = [jnp.concatenate([tiles[(hf, h, i - off + NA_WIN_H - 1)][:, :NA_KEY_COLS]
                                      for i in range(NA_WIN_H)] + [meta], axis=1) for h in range(2)]
            o_ref[0, hf, off] = jnp.concatenate(heads, axis=0)


def _na_bias(rpb):
    n_dr = 2 * NA_WIN_H - 1
    rpb_lanes = jnp.pad(rpb.astype(F32), ((0, 0), (0, 0), (0, LANES - rpb.shape[-1])))
    shape = (NA_HEADS // 2, 2, NA_WIN_H, 2 * NA_HALF, NA_KEYS)
    return pl.pallas_call(
        _na_bias_kernel,
        grid=(NA_HEADS // 2,),
        in_specs=[pl.BlockSpec((2, n_dr, LANES), lambda p: (p, 0, 0))],
        out_specs=pl.BlockSpec((1,) + shape[1:], lambda p: (p, 0, 0, 0, 0)),
        out_shape=jax.ShapeDtypeStruct(shape, F32),
        compiler_params=pltpu.CompilerParams(dimension_semantics=("arbitrary",), vmem_limit_bytes=VMEM_LIMIT),
        name="na_bias",
    )(rpb_lanes)


def _na(proj, proj_meta_pad, bias, B, n):
    rows, unroll = n // GRID_W, 8
    assert rows >= NA_WIN_H and rows % unroll == 0
    pairs = NA_WIDTH // LANES
    seq_spec = lambda c0: pl.BlockSpec((n, LANES), lambda hp, b: (b, c0 // LANES + hp))
    meta_spec = lambda c0: pl.BlockSpec((LANES, LANES), lambda hp, b: (0, c0 // LANES + hp))
    return pl.pallas_call(
        functools.partial(_na_kernel, rows=rows, unroll=unroll, lead=2 * unroll),
        grid=(pairs, B),
        in_specs=[
            seq_spec(COL_QA), seq_spec(COL_KA), seq_spec(COL_VA),
            meta_spec(COL_KA), meta_spec(COL_VA),
            pl.BlockSpec((1, 2, NA_WIN_H, 2 * NA_HALF, NA_KEYS), lambda hp, b: (hp, 0, 0, 0, 0)),
        ],
        out_specs=pl.BlockSpec((n, LANES), lambda hp, b: (b, hp)),
        out_shape=jax.ShapeDtypeStruct((B * n, NA_WIDTH), BF16),
        compiler_params=pltpu.CompilerParams(
            dimension_semantics=("arbitrary", "arbitrary"), vmem_limit_bytes=VMEM_LIMIT),
        name="na_attn",
    )(proj, proj, proj, proj_meta_pad, proj_meta_pad, bias)


def _ret_kernel(q_ref, k_ref, v_ref, g_ref, km_ref, vm_ref, lgf_ref, lgb_ref, gn_ref, o_ref,
                ds_ref, sf_ref, sb_ref, *, n_chunks, group):
    lgf = lgf_ref[0]
    lgb = lgb_ref[0]
    ii = lax.broadcasted_iota(jnp.int32, (CHUNK, CHUNK), 0).astype(F32)
    jj = lax.broadcasted_iota(jnp.int32, (CHUNK, CHUNK), 1).astype(F32)
    diff = ii - jj
    dmat = jnp.where(diff >= 0, jnp.exp(jnp.maximum(diff, 0.0) * lgf), jnp.exp(jnp.maximum(-diff, 0.0) * lgb))
    xi_f = jnp.exp((ii + 1.0) * lgf)
    zeta_f = jnp.exp((CHUNK - 1.0 - ii) * lgf)
    cd_f = jnp.exp(CHUNK * lgf)
    xi_b = jnp.exp((CHUNK - ii) * lgb)
    zeta_b = jnp.exp(ii * lgb)
    cd_b = jnp.exp(CHUNK * lgb)
    nt = (((1,), (1,)), ((), ()))
    tn_dims = (((0,), (0,)), ((), ()))
    gn = gn_ref[...]

    def chunk_slice(ch):
        return pl.ds(pl.multiple_of(ch * CHUNK, CHUNK), CHUNK)

    def increments(k, v):
        kf = k.astype(F32)
        kz = jnp.concatenate([(kf * zeta_f).astype(BF16), (kf * zeta_b).astype(BF16)], axis=1)
        return lax.dot_general(kz, v, tn_dims, preferred_element_type=F32)

    def stage1(gi, c):
        for u in range(group):
            ch = gi * group + u
            sl = chunk_slice(ch)
            ds_ref[ch] = increments(k_ref[sl, :], v_ref[sl, :])
        return c

    lax.fori_loop(0, n_chunks // group, stage1, 0)

    def scan_f(ch, S):
        sf_ref[ch] = S.astype(BF16)
        return S * cd_f + ds_ref[ch, :CHUNK, :]

    def scan_b(t, S):
        ch = n_chunks - 1 - t
        sb_ref[ch] = S.astype(BF16)
        return S * cd_b + ds_ref[ch, CHUNK:, :]

    S0 = increments(km_ref[...], vm_ref[...])[:CHUNK]
    lax.fori_loop(0, n_chunks, scan_f, S0, unroll=4)
    lax.fori_loop(0, n_chunks, scan_b, jnp.zeros((CHUNK, CHUNK), F32), unroll=4)

    def scores(ch):
        sl = chunk_slice(ch)
        q = q_ref[sl, :]
        s = lax.dot_general(q, k_ref[sl, :], nt, preferred_element_type=F32) * dmat
        qf = q.astype(F32)
        return jnp.concatenate([s.astype(BF16), (qf * xi_f).astype(BF16), (qf * xi_b).astype(BF16)], axis=1)

    def mix(ch, lhs):
        rhs = jnp.concatenate([v_ref[chunk_slice(ch), :], sf_ref[ch], sb_ref[ch]], axis=0)
        return jnp.dot(lhs, rhs, preferred_element_type=F32)

    def finish(ch, o):
        sl = chunk_slice(ch)
        mu = jnp.mean(o, axis=-1, keepdims=True)
        oc = o - mu
        var = jnp.mean(oc * oc, axis=-1, keepdims=True)
        on = oc * lax.rsqrt(var + LN_EPS) * gn
        g = g_ref[sl, :].astype(F32)
        o_ref[sl, :] = (g / (1.0 + jnp.exp(-g)) * on).astype(BF16)

    def stage3(gi, c):
        chs = [gi * group + u for u in range(group)]
        lhs = [scores(ch) for ch in chs]
        outs = [mix(ch, l) for ch, l in zip(chs, lhs)]
        for ch, o in zip(chs, outs):
            finish(ch, o)
        return c

    lax.fori_loop(0, n_chunks // group, stage3, 0)


def _retention(proj, proj_meta_pad, lgf, lgb, gn_g, B, n):
    n_chunks, group = n // CHUNK, 8
    assert n % CHUNK == 0 and n_chunks % group == 0
    seq_spec = lambda c0: pl.BlockSpec((n, LANES), lambda b, h: (b, c0 // LANES + h))
    meta_spec = lambda c0: pl.BlockSpec((CHUNK, LANES), lambda b, h: (0, c0 // LANES + h))
    lg_spec = pl.BlockSpec((1, 1, LANES), lambda b, h: (h, 0, 0))
    return pl.pallas_call(
        functools.partial(_ret_kernel, n_chunks=n_chunks, group=group),
        grid=(B, RET_HEADS),
        in_specs=[
            seq_spec(COL_QR), seq_spec(COL_KR), seq_spec(COL_VR), seq_spec(COL_GR),
            meta_spec(COL_KR), meta_spec(COL_VR),
            lg_spec, lg_spec,
            pl.BlockSpec((1, LANES), lambda b, h: (0, h)),
        ],
        out_specs=pl.BlockSpec((n, LANES), lambda b, h: (b, h)),
        out_shape=jax.ShapeDtypeStruct((B * n, RET_WIDTH), BF16),
        scratch_shapes=[pltpu.VMEM((n_chunks, 2 * CHUNK, LANES), F32),
                        pltpu.VMEM((n_chunks, CHUNK, LANES), BF16),
                        pltpu.VMEM((n_chunks, CHUNK, LANES), BF16)],
        compiler_params=pltpu.CompilerParams(
            dimension_semantics=("arbitrary", "arbitrary"), vmem_limit_bytes=VMEM_LIMIT),
        name="retention",
    )(proj, proj, proj, proj, proj_meta_pad, proj_meta_pad, lgf, lgb, gn_g)


def _outproj_kernel(ona_ref, oret_ref, h_ref, w_ref, g1_ref, b1_ref, o_ref, *, rb):
    tm = h_ref.shape[0]

    def norm_store(rows, mix):
        o_ref[rows, :] = _layer_norm_rows(ALPHA * h_ref[rows, :] + mix, g1_ref[...], b1_ref[...])

    pending = None
    for sb in range(tm // rb):
        rows = slice(sb * rb, (sb + 1) * rb)
        lhs = jnp.concatenate([ona_ref[rows, :], oret_ref[rows, :]], axis=1)
        mix = jnp.dot(lhs, w_ref[...], preferred_element_type=F32)
        if pending is not None:
            norm_store(*pending)
        pending = (rows, mix)
    norm_store(*pending)


def _outproj(o_na, o_ret, h, w_bf, g1, b1, tm=512):
    T = h.shape[0]
    assert T % tm == 0
    row_vec = pl.BlockSpec((1, D_MODEL), lambda i: (0, 0))
    return pl.pallas_call(
        functools.partial(_outproj_kernel, rb=256),
        grid=(T // tm,),
        in_specs=[
            pl.BlockSpec((tm, NA_WIDTH), lambda i: (i, 0)),
            pl.BlockSpec((tm, RET_WIDTH), lambda i: (i, 0)),
            pl.BlockSpec((tm, D_MODEL), lambda i: (i, 0)),
            pl.BlockSpec((D_MODEL, D_MODEL), lambda i: (0, 0)),
            row_vec, row_vec,
        ],
        out_specs=pl.BlockSpec((tm, D_MODEL), lambda i: (i, 0)),
        out_shape=jax.ShapeDtypeStruct((T, D_MODEL), F32),
        compiler_params=pltpu.CompilerParams(dimension_semantics=("arbitrary",), vmem_limit_bytes=VMEM_LIMIT),
        name="outproj_ln1",
    )(o_na, o_ret, h, w_bf, g1, b1)


def _ffn_kernel(h_ref, wg_ref, wu_ref, wd_ref, g2_ref, b2_ref, o_ref, hb_ref, acc_ref, *, rb):
    j = pl.program_id(1)
    tm = h_ref.shape[0]

    @pl.when(j == 0)
    def _():
        hb_ref[...] = h_ref[...].astype(BF16)
        acc_ref[...] = jnp.zeros_like(acc_ref)

    hb = hb_ref[...]
    g = jnp.dot(hb, wg_ref[...], preferred_element_type=F32)
    u = jnp.dot(hb, wu_ref[...], preferred_element_type=F32)
    a = (g / (1.0 + jnp.exp(-g)) * u).astype(BF16)
    acc_ref[...] += jnp.dot(a, wd_ref[...], preferred_element_type=F32)

    @pl.when(j == pl.num_programs(1) - 1)
    def _():
        def rows(i, c):
            sl = pl.ds(pl.multiple_of(i * rb, rb), rb)
            o_ref[sl, :] = _layer_norm_rows(ALPHA * h_ref[sl, :] + acc_ref[sl, :], g2_ref[...], b2_ref[...])
            return c

        lax.fori_loop(0, tm // rb, rows, 0)


def _ffn(h1, wg_bf, wu_bf, wd_bf, g2, b2, tm=512, tf=512):
    T = h1.shape[0]
    assert T % tm == 0 and D_FF % tf == 0
    row_vec = pl.BlockSpec((1, D_MODEL), lambda i, j: (0, 0))
    return pl.pallas_call(
        functools.partial(_ffn_kernel, rb=128),
        grid=(T // tm, D_FF // tf),
        in_specs=[
            pl.BlockSpec((tm, D_MODEL), lambda i, j: (i, 0)),
            pl.BlockSpec((D_MODEL, tf), lambda i, j: (0, j)),
            pl.BlockSpec((D_MODEL, tf), lambda i, j: (0, j)),
            pl.BlockSpec((tf, D_MODEL), lambda i, j: (j, 0)),
            row_vec, row_vec,
        ],
        out_specs=pl.BlockSpec((tm, D_MODEL), lambda i, j: (i, 0)),
        out_shape=jax.ShapeDtypeStruct((T, D_MODEL), F32),
        scratch_shapes=[pltpu.VMEM((tm, D_MODEL), BF16), pltpu.VMEM((tm, D_MODEL), F32)],
        compiler_params=pltpu.CompilerParams(
            dimension_semantics=("arbitrary", "arbitrary"), vmem_limit_bytes=VMEM_LIMIT),
        name="ffn_ln2",
    )(h1, wg_bf, wu_bf, wd_bf, g2, b2)


def _rope_tables(pos):
    half = RET_HEAD_DIM // 2
    inv = ROPE_BASE ** (-jnp.arange(half, dtype=F32) / half)
    ang = pos[:, None] * inv[None, :]
    c, s = jnp.cos(ang), jnp.sin(ang)
    cq = jnp.concatenate([c, c], axis=-1)
    sq = jnp.concatenate([-s, s], axis=-1)
    ks = RET_HEAD_DIM ** -0.5
    return (jnp.stack([cq, cq * ks]), jnp.stack([sq, sq * ks]))


def _pick_tm(n, cap):
    tm = cap
    while n % tm:
        tm //= 2
    return tm


def kernel(x_prompt, x_sample, meta_tokens, ln_in_g, ln_in_b, w_in, na_rpb, ret_decay_f, ret_decay_b,
           ret_gn_g, w_out, ln1_g, ln1_b, w_ffn_gate, w_ffn_up, w_ffn_down, ln2_g, ln2_b):
    row = lambda v: v.reshape(1, -1).astype(F32)
    gin, bin_ = row(ln_in_g), row(ln_in_b)
    w_in_bf = w_in[0].astype(BF16)
    w_out_bf = w_out[0].astype(BF16)
    wg_bf, wu_bf, wd_bf = w_ffn_gate[0].astype(BF16), w_ffn_up[0].astype(BF16), w_ffn_down[0].astype(BF16)
    bias = _na_bias(na_rpb[0])
    lg = lambda d: jnp.broadcast_to(jax.nn.log_sigmoid(d[0].astype(F32))[:, None, None], (RET_HEADS, 1, LANES))
    lgf, lgb = lg(ret_decay_f), lg(ret_decay_b)
    gn_g = row(ret_gn_g[0])

    meta_tabs = _rope_tables(jnp.arange(N_META, dtype=F32))
    proj_meta, _ = _inproj(meta_tokens.astype(F32), gin, bin_, w_in_bf, meta_tabs, N_META, N_META)
    proj_meta_pad = jnp.pad(proj_meta, ((CHUNK - N_META, 0), (0, 0)))

    def group(x):
        B, n, _ = x.shape
        x2d = x.reshape(B * n, D_MODEL)
        tabs = _rope_tables(jnp.arange(n, dtype=F32) + float(N_META))
        proj, h = _inproj(x2d, gin, bin_, w_in_bf, tabs, n, _pick_tm(n, 1024))
        o_na = _na(proj, _na_meta(proj_meta), bias, B, n)
        o_ret = _retention(proj, proj_meta_pad, lgf, lgb, gn_g, B, n)
        h1 = _outproj(o_na, o_ret, h, w_out_bf, row(ln1_g[0]), row(ln1_b[0]), tm=_pick_tm(B * n, 512))
        y = _ffn(h1, wg_bf, wu_bf, wd_bf, row(ln2_g[0]), row(ln2_b[0]), tm=_pick_tm(B * n, 512))
        return y.reshape(B, n, D_MODEL)

    return (group(x_prompt), group(x_sample))


def _na_meta(proj_meta):
    return jnp.pad(proj_meta, ((0, LANES - N_META), (0, 0)))
```

```python
import functools

import jax
import jax.numpy as jnp
import numpy as np
from jax import lax
from jax.experimental import pallas as pl
from jax.experimental.pallas import tpu as pltpu

D_MODEL = 2048
N_META = 16
GRID_W = 64
NA_HEADS = 16
NA_HEAD_DIM = 64
NA_WIDTH = NA_HEADS * NA_HEAD_DIM
NA_WIN_H = 8
NA_WIN_W = 16
NA_HALF = GRID_W // 2
NA_KEY_COLS = 48
NA_KEY_SHIFT = GRID_W - NA_KEY_COLS
NA_KEYS = 512
RET_HEADS = 8
RET_HEAD_DIM = 128
RET_WIDTH = RET_HEADS * RET_HEAD_DIM
CHUNK = 128
ROPE_BASE = 10000.0
IN_WIDTH = 3 * NA_WIDTH + 4 * RET_WIDTH
D_FF = 5632
LN_EPS = 1e-5
ALPHA = 2.0 ** 0.25

COL_QA, COL_KA, COL_VA = 0, NA_WIDTH, 2 * NA_WIDTH
COL_QR = 3 * NA_WIDTH
COL_KR = COL_QR + RET_WIDTH
COL_VR = COL_KR + RET_WIDTH
COL_GR = COL_VR + RET_WIDTH

LANES = 128
NEG = -1e30
VMEM_LIMIT = 56 * 1024 * 1024

F32 = jnp.float32
BF16 = jnp.bfloat16


def _layer_norm_rows(x, g, b):
    mu = jnp.mean(x, axis=-1, keepdims=True)
    xc = x - mu
    var = jnp.mean(xc * xc, axis=-1, keepdims=True)
    return xc * lax.rsqrt(var + LN_EPS) * g + b


def _inproj_kernel(x_ref, g_ref, b_ref, w_ref, c_ref, s_ref, o_ref, h_ref, hb_ref, *, tn, rb):
    j = pl.program_id(1)
    tm = x_ref.shape[0]

    col0 = j * tn
    rotary = jnp.logical_and(col0 >= COL_QR, col0 < COL_VR)
    wsl = (slice(None), pl.ds(pl.multiple_of(col0, tn), tn))

    @pl.when(j == 0)
    def _():
        for sb in range(tm // rb):
            rows = slice(sb * rb, (sb + 1) * rb)
            h = _layer_norm_rows(x_ref[rows, :], g_ref[...], b_ref[...])
            h_ref[rows, :] = h
            hb_ref[rows, :] = h.astype(BF16)
            o_ref[rows, :] = jnp.dot(hb_ref[rows, :], w_ref[wsl], preferred_element_type=F32).astype(BF16)

    @pl.when(jnp.logical_and(j != 0, jnp.logical_not(rotary)))
    def _():
        o_ref[...] = jnp.dot(hb_ref[...], w_ref[wsl], preferred_element_type=F32).astype(BF16)

    @pl.when(rotary)
    def _():
        acc = jnp.dot(hb_ref[...], w_ref[wsl], preferred_element_type=F32)
        c = c_ref[...]
        s = s_ref[...]
        for t in range(tn // LANES):
            xs = acc[:, t * LANES:(t + 1) * LANES]
            o_ref[:, t * LANES:(t + 1) * LANES] = (xs * c + pltpu.roll(xs, LANES // 2, 1) * s).astype(BF16)


def _inproj(x2d, ln_g, ln_b, w_bf, tabs, n_seq, tm, tn=512):
    T = x2d.shape[0]
    assert T % tm == 0 and n_seq % tm == 0 and IN_WIDTH % tn == 0 and RET_WIDTH % tn == 0
    blocks_per_seq = n_seq // tm

    def tab_index(i, j):
        col0 = j * tn
        is_k = jnp.logical_and(col0 >= COL_KR, col0 < COL_VR)
        return (jnp.where(is_k, 1, 0), i % blocks_per_seq, 0)

    tab_spec = pl.BlockSpec((None, tm, LANES), tab_index)
    rb = min(tm, 128)
    return pl.pallas_call(
        functools.partial(_inproj_kernel, tn=tn, rb=rb),
        grid=(T // tm, IN_WIDTH // tn),
        in_specs=[
            pl.BlockSpec((tm, D_MODEL), lambda i, j: (i, 0)),
            pl.BlockSpec((1, D_MODEL), lambda i, j: (0, 0)),
            pl.BlockSpec((1, D_MODEL), lambda i, j: (0, 0)),
            pl.BlockSpec((D_MODEL, IN_WIDTH), lambda i, j: (0, 0), pipeline_mode=pl.Buffered(1)),
            tab_spec, tab_spec,
        ],
        out_specs=[pl.BlockSpec((tm, tn), lambda i, j: (i, j)),
                   pl.BlockSpec((tm, D_MODEL), lambda i, j: (i, 0))],
        out_shape=[jax.ShapeDtypeStruct((T, IN_WIDTH), BF16), jax.ShapeDtypeStruct((T, D_MODEL), F32)],
        scratch_shapes=[pltpu.VMEM((tm, D_MODEL), BF16)],
        compiler_params=pltpu.CompilerParams(
            dimension_semantics=("arbitrary", "arbitrary"), vmem_limit_bytes=VMEM_LIMIT),
        name="inproj",
    )(x2d, ln_g, ln_b, w_bf, *tabs)


def _na_kernel(q_ref, k_ref, v_ref, km_ref, vm_ref, bias_ref, o_ref, *, rows, unroll, lead):
    lane = lax.broadcasted_iota(jnp.int32, (1, LANES), 1)
    scale = NA_HEAD_DIM ** -0.5
    qmask = [jnp.where((lane // NA_HEAD_DIM) == hh, scale, 0.0).astype(BF16) for hh in range(2)]
    km = km_ref[...]
    vm = vm_ref[...]
    nt = (((1,), (1,)), ((), ()))

    def window(ref, meta, rs, hf):
        parts = [ref[pl.ds(pl.multiple_of((rs + i) * GRID_W + NA_KEY_SHIFT * hf, NA_KEY_SHIFT), NA_KEY_COLS), :]
                 for i in range(NA_WIN_H)]
        return jnp.concatenate(parts + [meta], axis=0)

    units = [(u, hf) for u in range(unroll) for hf in range(2)]
    n_groups = rows // unroll

    def locate(g, idx):
        u, hf = units[idx]
        r = g * unroll + u
        rs = jnp.clip(r - NA_WIN_H // 2, 0, rows - NA_WIN_H)
        qsl = pl.ds(pl.multiple_of(r * GRID_W + NA_HALF * hf, NA_HALF), NA_HALF)
        return r - rs, qsl, rs, hf

    def scores(g, idx):
        off, qsl, rs, hf = locate(g, idx)
        q = q_ref[qsl, :]
        q2 = jnp.concatenate([q * qmask[0], q * qmask[1]], axis=0)
        s = lax.dot_general(q2, window(k_ref, km, rs, hf), nt, preferred_element_type=F32)
        return s + bias_ref[0, hf, pl.ds(off, 1)][0]

    def finish(s, g, idx):
        _, qsl, rs, hf = locate(g, idx)
        m = jnp.max(s, axis=-1, keepdims=True)
        p = jnp.exp(s - m)
        den = jnp.sum(p, axis=-1, keepdims=True)
        o2 = jnp.dot(p.astype(BF16), window(v_ref, vm, rs, hf), preferred_element_type=F32) / den
        o_ref[qsl, :] = jnp.where(lane < NA_HEAD_DIM, o2[:NA_HALF], o2[NA_HALF:]).astype(BF16)

    def body(g, c):
        staged = [scores(g, idx) for idx in range(lead)]
        for idx in range(len(units)):
            if idx + lead < len(units):
                staged.append(scores(g, idx + lead))
            finish(staged[idx], g, idx)
        return c

    lax.fori_loop(0, n_groups, body, 0)


def _na_bias_kernel(rpb_ref, o_ref):
    c32 = lax.broadcasted_iota(jnp.int32, (NA_HALF, LANES), 0)
    lane = lax.broadcasted_iota(jnp.int32, (NA_HALF, LANES), 1)
    meta = jnp.where(lane < N_META, 0.0, NEG).astype(F32)
    tiles = {}
    for hf in range(2):
        c = c32 + NA_HALF * hf
        kc = lane + NA_KEY_SHIFT * hf
        cs = jnp.clip(c - NA_WIN_W // 2, 0, GRID_W - NA_WIN_W)
        valid = (kc >= cs) & (kc < cs + NA_WIN_W) & (lane < NA_KEY_COLS)
        shift = (LANES - (NA_WIN_W - 1) - NA_KEY_SHIFT * hf + NA_HALF * hf) % LANES
        for h in range(2):
            for dr in range(2 * NA_WIN_H - 1):
                w = jnp.broadcast_to(rpb_ref[h, dr:dr + 1, :], (NA_HALF, LANES))
                tiles[(hf, h, dr)] = jnp.where(valid, pltpu.roll(w, shift, 1, stride=1, stride_axis=0), NEG)
    for hf in range(2):
        for off in range(NA_WIN_H):
            heads = [jnp.concatenate([tiles[(hf, h, i - off + NA_WIN_H - 1)][:, :NA_KEY_COLS]
                                      for i in range(NA_WIN_H)] + [meta], axis=1) for h in range(2)]
            o_ref[0, hf, off] = jnp.concatenate(heads, axis=0)


def _na_bias(rpb):
    n_dr = 2 * NA_WIN_H - 1
    rpb_lanes = jnp.pad(rpb.astype(F32), ((0, 0), (0, 0), (0, LANES - rpb.shape[-1])))
    shape = (NA_HEADS // 2, 2, NA_WIN_H, 2 * NA_HALF, NA_KEYS)
    return pl.pallas_call(
        _na_bias_kernel,
        grid=(NA_HEADS // 2,),
        in_specs=[pl.BlockSpec((2, n_dr, LANES), lambda p: (p, 0, 0))],
        out_specs=pl.BlockSpec((1,) + shape[1:], lambda p: (p, 0, 0, 0, 0)),
        out_shape=jax.ShapeDtypeStruct(shape, F32),
        compiler_params=pltpu.CompilerParams(dimension_semantics=("arbitrary",), vmem_limit_bytes=VMEM_LIMIT),
        name="na_bias",
    )(rpb_lanes)


def _na(proj, proj_meta_pad, bias, B, n):
    rows, unroll = n // GRID_W, 8
    assert rows >= NA_WIN_H and rows % unroll == 0
    pairs = NA_WIDTH // LANES
    seq_spec = lambda c0: pl.BlockSpec((n, LANES), lambda hp, b: (b, c0 // LANES + hp))
    meta_spec = lambda c0: pl.BlockSpec((LANES, LANES), lambda hp, b: (0, c0 // LANES + hp))
    return pl.pallas_call(
        functools.partial(_na_kernel, rows=rows, unroll=unroll, lead=2 * unroll),
        grid=(pairs, B),
        in_specs=[
            seq_spec(COL_QA), seq_spec(COL_KA), seq_spec(COL_VA),
            meta_spec(COL_KA), meta_spec(COL_VA),
            pl.BlockSpec((1, 2, NA_WIN_H, 2 * NA_HALF, NA_KEYS), lambda hp, b: (hp, 0, 0, 0, 0)),
        ],
        out_specs=pl.BlockSpec((n, LANES), lambda hp, b: (b, hp)),
        out_shape=jax.ShapeDtypeStruct((B * n, NA_WIDTH), BF16),
        compiler_params=pltpu.CompilerParams(
            dimension_semantics=("arbitrary", "arbitrary"), vmem_limit_bytes=VMEM_LIMIT),
        name="na_attn",
    )(proj, proj, proj, proj_meta_pad, proj_meta_pad, bias)


def _ret_kernel(q_ref, k_ref, v_ref, g_ref, km_ref, vm_ref, lgf_ref, lgb_ref, gn_ref, o_ref,
                ds_ref, sf_ref, sb_ref, *, n_chunks, group):
    lgf = lgf_ref[0]
    lgb = lgb_ref[0]
    ii = lax.broadcasted_iota(jnp.int32, (CHUNK, CHUNK), 0).astype(F32)
    jj = lax.broadcasted_iota(jnp.int32, (CHUNK, CHUNK), 1).astype(F32)
    diff = ii - jj
    dmat = jnp.where(diff >= 0, jnp.exp(jnp.maximum(diff, 0.0) * lgf), jnp.exp(jnp.maximum(-diff, 0.0) * lgb))
    xi_f = jnp.exp((ii + 1.0) * lgf)
    zeta_f = jnp.exp((CHUNK - 1.0 - ii) * lgf)
    cd_f = jnp.exp(CHUNK * lgf)
    xi_b = jnp.exp((CHUNK - ii) * lgb)
    zeta_b = jnp.exp(ii * lgb)
    cd_b = jnp.exp(CHUNK * lgb)
    nt = (((1,), (1,)), ((), ()))
    tn_dims = (((0,), (0,)), ((), ()))
    gn = gn_ref[...]

    def chunk_slice(ch):
        return pl.ds(pl.multiple_of(ch * CHUNK, CHUNK), CHUNK)

    def increments(k, v):
        kf = k.astype(F32)
        kz = jnp.concatenate([(kf * zeta_f).astype(BF16), (kf * zeta_b).astype(BF16)], axis=1)
        return lax.dot_general(kz, v, tn_dims, preferred_element_type=F32)

    def stage1(gi, c):
        for u in range(group):
            ch = gi * group + u
            sl = chunk_slice(ch)
            ds_ref[ch] = increments(k_ref[sl, :], v_ref[sl, :])
        return c

    lax.fori_loop(0, n_chunks // group, stage1, 0)

    def scan_f(ch, S):
        sf_ref[ch] = S.astype(BF16)
        return S * cd_f + ds_ref[ch, :CHUNK, :]

    def scan_b(t, S):
        ch = n_chunks - 1 - t
        sb_ref[ch] = S.astype(BF16)
        return S * cd_b + ds_ref[ch, CHUNK:, :]

    S0 = increments(km_ref[...], vm_ref[...])[:CHUNK]
    lax.fori_loop(0, n_chunks, scan_f, S0, unroll=4)
    lax.fori_loop(0, n_chunks, scan_b, jnp.zeros((CHUNK, CHUNK), F32), unroll=4)

    def scores(ch):
        sl = chunk_slice(ch)
        q = q_ref[sl, :]
        s = lax.dot_general(q, k_ref[sl, :], nt, preferred_element_type=F32) * dmat
        qf = q.astype(F32)
        return jnp.concatenate([s.astype(BF16), (qf * xi_f).astype(BF16), (qf * xi_b).astype(BF16)], axis=1)

    def mix(ch, lhs):
        rhs = jnp.concatenate([v_ref[chunk_slice(ch), :], sf_ref[ch], sb_ref[ch]], axis=0)
        return jnp.dot(lhs, rhs, preferred_element_type=F32)

    def finish(ch, o):
        sl = chunk_slice(ch)
        mu = jnp.mean(o, axis=-1, keepdims=True)
        oc = o - mu
        var = jnp.mean(oc * oc, axis=-1, keepdims=True)
        on = oc * lax.rsqrt(var + LN_EPS) * gn
        g = g_ref[sl, :].astype(F32)
        o_ref[sl, :] = (g / (1.0 + jnp.exp(-g)) * on).astype(BF16)

    def stage3(gi, c):
        chs = [gi * group + u for u in range(group)]
        lhs = [scores(ch) for ch in chs]
        outs = [mix(ch, l) for ch, l in zip(chs, lhs)]
        for ch, o in zip(chs, outs):
            finish(ch, o)
        return c

    lax.fori_loop(0, n_chunks // group, stage3, 0)


def _retention(proj, proj_meta_pad, lgf, lgb, gn_g, B, n):
    n_chunks, group = n // CHUNK, 8
    assert n % CHUNK == 0 and n_chunks % group == 0
    seq_spec = lambda c0: pl.BlockSpec((n, LANES), lambda b, h: (b, c0 // LANES + h))
    meta_spec = lambda c0: pl.BlockSpec((CHUNK, LANES), lambda b, h: (0, c0 // LANES + h))
    lg_spec = pl.BlockSpec((1, 1, LANES), lambda b, h: (h, 0, 0))
    return pl.pallas_call(
        functools.partial(_ret_kernel, n_chunks=n_chunks, group=group),
        grid=(B, RET_HEADS),
        in_specs=[
            seq_spec(COL_QR), seq_spec(COL_KR), seq_spec(COL_VR), seq_spec(COL_GR),
            meta_spec(COL_KR), meta_spec(COL_VR),
            lg_spec, lg_spec,
            pl.BlockSpec((1, LANES), lambda b, h: (0, h)),
        ],
        out_specs=pl.BlockSpec((n, LANES), lambda b, h: (b, h)),
        out_shape=jax.ShapeDtypeStruct((B * n, RET_WIDTH), BF16),
        scratch_shapes=[pltpu.VMEM((n_chunks, 2 * CHUNK, LANES), F32),
                        pltpu.VMEM((n_chunks, CHUNK, LANES), BF16),
                        pltpu.VMEM((n_chunks, CHUNK, LANES), BF16)],
        compiler_params=pltpu.CompilerParams(
            dimension_semantics=("arbitrary", "arbitrary"), vmem_limit_bytes=VMEM_LIMIT),
        name="retention",
    )(proj, proj, proj, proj, proj_meta_pad, proj_meta_pad, lgf, lgb, gn_g)


def _outproj_kernel(ona_ref, oret_ref, h_ref, w_ref, g1_ref, b1_ref, o_ref, *, rb):
    tm = h_ref.shape[0]

    def norm_store(rows, mix):
        o_ref[rows, :] = _layer_norm_rows(ALPHA * h_ref[rows, :] + mix, g1_ref[...], b1_ref[...])

    pending = None
    for sb in range(tm // rb):
        rows = slice(sb * rb, (sb + 1) * rb)
        lhs = jnp.concatenate([ona_ref[rows, :], oret_ref[rows, :]], axis=1)
        mix = jnp.dot(lhs, w_ref[...], preferred_element_type=F32)
        if pending is not None:
            norm_store(*pending)
        pending = (rows, mix)
    norm_store(*pending)


def _outproj(o_na, o_ret, h, w_bf, g1, b1, tm=512):
    T = h.shape[0]
    assert T % tm == 0
    row_vec = pl.BlockSpec((1, D_MODEL), lambda i: (0, 0))
    return pl.pallas_call(
        functools.partial(_outproj_kernel, rb=256),
        grid=(T // tm,),
        in_specs=[
            pl.BlockSpec((tm, NA_WIDTH), lambda i: (i, 0)),
            pl.BlockSpec((tm, RET_WIDTH), lambda i: (i, 0)),
            pl.BlockSpec((tm, D_MODEL), lambda i: (i, 0)),
            pl.BlockSpec((D_MODEL, D_MODEL), lambda i: (0, 0)),
            row_vec, row_vec,
        ],
        out_specs=pl.BlockSpec((tm, D_MODEL), lambda i: (i, 0)),
        out_shape=jax.ShapeDtypeStruct((T, D_MODEL), F32),
        compiler_params=pltpu.CompilerParams(dimension_semantics=("arbitrary",), vmem_limit_bytes=VMEM_LIMIT),
        name="outproj_ln1",
    )(o_na, o_ret, h, w_bf, g1, b1)


def _ffn_kernel(h_ref, wg_ref, wu_ref, wd_ref, g2_ref, b2_ref, o_ref, hb_ref, acc_ref, *, rb):
    j = pl.program_id(1)
    tm = h_ref.shape[0]

    @pl.when(j == 0)
    def _():
        hb_ref[...] = h_ref[...].astype(BF16)
        acc_ref[...] = jnp.zeros_like(acc_ref)

    hb = hb_ref[...]
    g = jnp.dot(hb, wg_ref[...], preferred_element_type=F32)
    u = jnp.dot(hb, wu_ref[...], preferred_element_type=F32)
    a = (g / (1.0 + jnp.exp(-g)) * u).astype(BF16)
    acc_ref[...] += jnp.dot(a, wd_ref[...], preferred_element_type=F32)

    @pl.when(j == pl.num_programs(1) - 1)
    def _():
        def rows(i, c):
            sl = pl.ds(pl.multiple_of(i * rb, rb), rb)
            o_ref[sl, :] = _layer_norm_rows(ALPHA * h_ref[sl, :] + acc_ref[sl, :], g2_ref[...], b2_ref[...])
            return c

        lax.fori_loop(0, tm // rb, rows, 0)


def _ffn(h1, wg_bf, wu_bf, wd_bf, g2, b2, tm=512, tf=512):
    T = h1.shape[0]
    assert T % tm == 0 and D_FF % tf == 0
    row_vec = pl.BlockSpec((1, D_MODEL), lambda i, j: (0, 0))
    return pl.pallas_call(
        functools.partial(_ffn_kernel, rb=128),
        grid=(T // tm, D_FF // tf),
        in_specs=[
            pl.BlockSpec((tm, D_MODEL), lambda i, j: (i, 0)),
            pl.BlockSpec((D_MODEL, tf), lambda i, j: (0, j)),
            pl.BlockSpec((D_MODEL, tf), lambda i, j: (0, j)),
            pl.BlockSpec((tf, D_MODEL), lambda i, j: (j, 0)),
            row_vec, row_vec,
        ],
        out_specs=pl.BlockSpec((tm, D_MODEL), lambda i, j: (i, 0)),
        out_shape=jax.ShapeDtypeStruct((T, D_MODEL), F32),
        scratch_shapes=[pltpu.VMEM((tm, D_MODEL), BF16), pltpu.VMEM((tm, D_MODEL), F32)],
        compiler_params=pltpu.CompilerParams(
            dimension_semantics=("arbitrary", "arbitrary"), vmem_limit_bytes=VMEM_LIMIT),
        name="ffn_ln2",
    )(h1, wg_bf, wu_bf, wd_bf, g2, b2)


def _rope_tables(pos):
    half = RET_HEAD_DIM // 2
    inv = ROPE_BASE ** (-jnp.arange(half, dtype=F32) / half)
    ang = pos[:, None] * inv[None, :]
    c, s = jnp.cos(ang), jnp.sin(ang)
    cq = jnp.concatenate([c, c], axis=-1)
    sq = jnp.concatenate([-s, s], axis=-1)
    ks = RET_HEAD_DIM ** -0.5
    return (jnp.stack([cq, cq * ks]), jnp.stack([sq, sq * ks]))


def _pick_tm(n, cap):
    tm = cap
    while n % tm:
        tm //= 2
    return tm


def kernel(x_prompt, x_sample, meta_tokens, ln_in_g, ln_in_b, w_in, na_rpb, ret_decay_f, ret_decay_b,
           ret_gn_g, w_out, ln1_g, ln1_b, w_ffn_gate, w_ffn_up, w_ffn_down, ln2_g, ln2_b):
    row = lambda v: v.reshape(1, -1).astype(F32)
    gin, bin_ = row(ln_in_g), row(ln_in_b)
    w_in_bf = w_in[0].astype(BF16)
    w_out_bf = w_out[0].astype(BF16)
    wg_bf, wu_bf, wd_bf = w_ffn_gate[0].astype(BF16), w_ffn_up[0].astype(BF16), w_ffn_down[0].astype(BF16)
    bias = _na_bias(na_rpb[0])
    lg = lambda d: jnp.broadcast_to(jax.nn.log_sigmoid(d[0].astype(F32))[:, None, None], (RET_HEADS, 1, LANES))
    lgf, lgb = lg(ret_decay_f), lg(ret_decay_b)
    gn_g = row(ret_gn_g[0])

    meta_tabs = _rope_tables(jnp.arange(N_META, dtype=F32))
    proj_meta, _ = _inproj(meta_tokens.astype(F32), gin, bin_, w_in_bf, meta_tabs, N_META, N_META)
    proj_meta_pad = jnp.pad(proj_meta, ((CHUNK - N_META, 0), (0, 0)))

    def group(x):
        B, n, _ = x.shape
        x2d = x.reshape(B * n, D_MODEL)
        tabs = _rope_tables(jnp.arange(n, dtype=F32) + float(N_META))
        proj, h = _inproj(x2d, gin, bin_, w_in_bf, tabs, n, _pick_tm(n, 512))
        o_na = _na(proj, _na_meta(proj_meta), bias, B, n)
        o_ret = _retention(proj, proj_meta_pad, lgf, lgb, gn_g, B, n)
        h1 = _outproj(o_na, o_ret, h, w_out_bf, row(ln1_g[0]), row(ln1_b[0]), tm=_pick_tm(B * n, 512))
        y = _ffn(h1, wg_bf, wu_bf, wd_bf, row(ln2_g[0]), row(ln2_b[0]), tm=_pick_tm(B * n, 512))
        return y.reshape(B, n, D_MODEL)

    return (group(x_prompt), group(x_sample))


def _na_meta(proj_meta):
    return jnp.pad(proj_meta, ((0, LANES - N_META), (0, 0)))
```

```python
import functools

import jax
import jax.numpy as jnp
import numpy as np
from jax import lax
from jax.experimental import pallas as pl
from jax.experimental.pallas import tpu as pltpu

D_MODEL = 2048
N_META = 16
GRID_W = 64
NA_HEADS = 16
NA_HEAD_DIM = 64
NA_WIDTH = NA_HEADS * NA_HEAD_DIM
NA_WIN_H = 8
NA_WIN_W = 16
NA_HALF = GRID_W // 2
NA_KEY_COLS = 48
NA_KEY_SHIFT = GRID_W - NA_KEY_COLS
NA_KEYS = 512
RET_HEADS = 8
RET_HEAD_DIM = 128
RET_WIDTH = RET_HEADS * RET_HEAD_DIM
CHUNK = 128
ROPE_BASE = 10000.0
IN_WIDTH = 3 * NA_WIDTH + 4 * RET_WIDTH
D_FF = 5632
LN_EPS = 1e-5
ALPHA = 2.0 ** 0.25

COL_QA, COL_KA, COL_VA = 0, NA_WIDTH, 2 * NA_WIDTH
COL_QR = 3 * NA_WIDTH
COL_KR = COL_QR + RET_WIDTH
COL_VR = COL_KR + RET_WIDTH
COL_GR = COL_VR + RET_WIDTH

LANES = 128
NEG = -1e30
VMEM_LIMIT = 56 * 1024 * 1024

F32 = jnp.float32
BF16 = jnp.bfloat16


def _layer_norm_rows(x, g, b):
    mu = jnp.mean(x, axis=-1, keepdims=True)
    xc = x - mu
    var = jnp.mean(xc * xc, axis=-1, keepdims=True)
    return xc * lax.rsqrt(var + LN_EPS) * g + b


def _inproj_kernel(x_ref, g_ref, b_ref, w_ref, c_ref, s_ref, o_ref, hb_ref, *, tn, rb):
    j = pl.program_id(1)
    tm = x_ref.shape[0]

    col0 = j * tn
    rotary = jnp.logical_and(col0 >= COL_QR, col0 < COL_VR)

    @pl.when(j == 0)
    def _():
        for sb in range(tm // rb):
            rows = slice(sb * rb, (sb + 1) * rb)
            hb_ref[rows, :] = _layer_norm_rows(x_ref[rows, :], g_ref[...], b_ref[...]).astype(BF16)
            o_ref[rows, :] = jnp.dot(hb_ref[rows, :], w_ref[...], preferred_element_type=F32).astype(BF16)

    @pl.when(jnp.logical_and(j != 0, jnp.logical_not(rotary)))
    def _():
        o_ref[...] = jnp.dot(hb_ref[...], w_ref[...], preferred_element_type=F32).astype(BF16)

    @pl.when(rotary)
    def _():
        acc = jnp.dot(hb_ref[...], w_ref[...], preferred_element_type=F32)
        c = c_ref[...]
        s = s_ref[...]
        for t in range(tn // LANES):
            xs = acc[:, t * LANES:(t + 1) * LANES]
            o_ref[:, t * LANES:(t + 1) * LANES] = (xs * c + pltpu.roll(xs, LANES // 2, 1) * s).astype(BF16)


def _inproj(x2d, ln_g, ln_b, w_bf, tabs, n_seq, tm, tn=1024):
    T = x2d.shape[0]
    assert T % tm == 0 and n_seq % tm == 0 and IN_WIDTH % tn == 0 and RET_WIDTH % tn == 0
    blocks_per_seq = n_seq // tm

    def tab_index(i, j):
        col0 = j * tn
        is_k = jnp.logical_and(col0 >= COL_KR, col0 < COL_VR)
        return (jnp.where(is_k, 1, 0), i % blocks_per_seq, 0)

    tab_spec = pl.BlockSpec((None, tm, LANES), tab_index)
    rb = min(tm, 128)
    return pl.pallas_call(
        functools.partial(_inproj_kernel, tn=tn, rb=rb),
        grid=(T // tm, IN_WIDTH // tn),
        in_specs=[
            pl.BlockSpec((tm, D_MODEL), lambda i, j: (i, 0)),
            pl.BlockSpec((1, D_MODEL), lambda i, j: (0, 0)),
            pl.BlockSpec((1, D_MODEL), lambda i, j: (0, 0)),
            pl.BlockSpec((D_MODEL, tn), lambda i, j: (0, j)),
            tab_spec, tab_spec,
        ],
        out_specs=pl.BlockSpec((tm, tn), lambda i, j: (i, j)),
        out_shape=jax.ShapeDtypeStruct((T, IN_WIDTH), BF16),
        scratch_shapes=[pltpu.VMEM((tm, D_MODEL), BF16)],
        compiler_params=pltpu.CompilerParams(
            dimension_semantics=("arbitrary", "arbitrary"), vmem_limit_bytes=VMEM_LIMIT),
        name="inproj",
    )(x2d, ln_g, ln_b, w_bf, *tabs)


def _na_kernel(q_ref, k_ref, v_ref, km_ref, vm_ref, bias_ref, o_ref, *, rows, unroll, lead):
    lane = lax.broadcasted_iota(jnp.int32, (1, LANES), 1)
    scale = NA_HEAD_DIM ** -0.5
    qmask = [jnp.where((lane // NA_HEAD_DIM) == hh, scale, 0.0).astype(BF16) for hh in range(2)]
    km = km_ref[...]
    vm = vm_ref[...]
    nt = (((1,), (1,)), ((), ()))

    def window(ref, meta, rs, hf):
        parts = [ref[pl.ds(pl.multiple_of((rs + i) * GRID_W + NA_KEY_SHIFT * hf, NA_KEY_SHIFT), NA_KEY_COLS), :]
                 for i in range(NA_WIN_H)]
        return jnp.concatenate(parts + [meta], axis=0)

    units = [(u, hf) for u in range(unroll) for hf in range(2)]
    n_groups = rows // unroll

    def locate(g, idx):
        u, hf = units[idx]
        r = g * unroll + u
        rs = jnp.clip(r - NA_WIN_H // 2, 0, rows - NA_WIN_H)
        qsl = pl.ds(pl.multiple_of(r * GRID_W + NA_HALF * hf, NA_HALF), NA_HALF)
        return r - rs, qsl, rs, hf

    def scores(g, idx):
        off, qsl, rs, hf = locate(g, idx)
        q = q_ref[qsl, :]
        q2 = jnp.concatenate([q * qmask[0], q * qmask[1]], axis=0)
        s = lax.dot_general(q2, window(k_ref, km, rs, hf), nt, preferred_element_type=F32)
        return s + bias_ref[0, hf, pl.ds(off, 1)][0]

    def finish(s, g, idx):
        _, qsl, rs, hf = locate(g, idx)
        m = jnp.max(s, axis=-1, keepdims=True)
        p = jnp.exp(s - m)
        den = jnp.sum(p, axis=-1, keepdims=True)
        o2 = jnp.dot(p.astype(BF16), window(v_ref, vm, rs, hf), preferred_element_type=F32) / den
        o_ref[qsl, :] = jnp.where(lane < NA_HEAD_DIM, o2[:NA_HALF], o2[NA_HALF:]).astype(BF16)

    def body(g, c):
        staged = [scores(g, idx) for idx in range(lead)]
        for idx in range(len(units)):
            if idx + lead < len(units):
                staged.append(scores(g, idx + lead))
            finish(staged[idx], g, idx)
        return c

    lax.fori_loop(0, n_groups, body, 0)


def _na_bias_kernel(rpb_ref, o_ref):
    c32 = lax.broadcasted_iota(jnp.int32, (NA_HALF, LANES), 0)
    lane = lax.broadcasted_iota(jnp.int32, (NA_HALF, LANES), 1)
    meta = jnp.where(lane < N_META, 0.0, NEG).astype(F32)
    tiles = {}
    for hf in range(2):
        c = c32 + NA_HALF * hf
        kc = lane + NA_KEY_SHIFT * hf
        cs = jnp.clip(c - NA_WIN_W // 2, 0, GRID_W - NA_WIN_W)
        valid = (kc >= cs) & (kc < cs + NA_WIN_W) & (lane < NA_KEY_COLS)
        shift = (LANES - (NA_WIN_W - 1) - NA_KEY_SHIFT * hf + NA_HALF * hf) % LANES
        for h in range(2):
            for dr in range(2 * NA_WIN_H - 1):
                w = jnp.broadcast_to(rpb_ref[h, dr:dr + 1, :], (NA_HALF, LANES))
                tiles[(hf, h, dr)] = jnp.where(valid, pltpu.roll(w, shift, 1, stride=1, stride_axis=0), NEG)
    for hf in range(2):
        for off in range(NA_WIN_H):
            heads = [jnp.concatenate([tiles[(hf, h, i - off + NA_WIN_H - 1)][:, :NA_KEY_COLS]
                                      for i in range(NA_WIN_H)] + [meta], axis=1) for h in range(2)]
            o_ref[0, hf, off] = jnp.concatenate(heads, axis=0)


def _na_bias(rpb):
    n_dr = 2 * NA_WIN_H - 1
    rpb_lanes = jnp.pad(rpb.astype(F32), ((0, 0), (0, 0), (0, LANES - rpb.shape[-1])))
    shape = (NA_HEADS // 2, 2, NA_WIN_H, 2 * NA_HALF, NA_KEYS)
    return pl.pallas_call(
        _na_bias_kernel,
        grid=(NA_HEADS // 2,),
        in_specs=[pl.BlockSpec((2, n_dr, LANES), lambda p: (p, 0, 0))],
        out_specs=pl.BlockSpec((1,) + shape[1:], lambda p: (p, 0, 0, 0, 0)),
        out_shape=jax.ShapeDtypeStruct(shape, F32),
        compiler_params=pltpu.CompilerParams(dimension_semantics=("arbitrary",), vmem_limit_bytes=VMEM_LIMIT),
        name="na_bias",
    )(rpb_lanes)


def _na(proj, proj_meta_pad, bias, B, n):
    rows, unroll = n // GRID_W, 8
    assert rows >= NA_WIN_H and rows % unroll == 0
    pairs = NA_WIDTH // LANES
    seq_spec = lambda c0: pl.BlockSpec((n, LANES), lambda hp, b: (b, c0 // LANES + hp))
    meta_spec = lambda c0: pl.BlockSpec((LANES, LANES), lambda hp, b: (0, c0 // LANES + hp))
    return pl.pallas_call(
        functools.partial(_na_kernel, rows=rows, unroll=unroll, lead=2 * unroll),
        grid=(pairs, B),
        in_specs=[
            seq_spec(COL_QA), seq_spec(COL_KA), seq_spec(COL_VA),
            meta_spec(COL_KA), meta_spec(COL_VA),
            pl.BlockSpec((1, 2, NA_WIN_H, 2 * NA_HALF, NA_KEYS), lambda hp, b: (hp, 0, 0, 0, 0)),
        ],
        out_specs=pl.BlockSpec((n, LANES), lambda hp, b: (b, hp)),
        out_shape=jax.ShapeDtypeStruct((B * n, NA_WIDTH), BF16),
        compiler_params=pltpu.CompilerParams(
            dimension_semantics=("arbitrary", "arbitrary"), vmem_limit_bytes=VMEM_LIMIT),
        name="na_attn",
    )(proj, proj, proj, proj_meta_pad, proj_meta_pad, bias)


def _ret_kernel(q_ref, k_ref, v_ref, g_ref, km_ref, vm_ref, lgf_ref, lgb_ref, gn_ref, o_ref,
                ds_ref, sf_ref, sb_ref, *, n_chunks, group):
    lgf = lgf_ref[0]
    lgb = lgb_ref[0]
    ii = lax.broadcasted_iota(jnp.int32, (CHUNK, CHUNK), 0).astype(F32)
    jj = lax.broadcasted_iota(jnp.int32, (CHUNK, CHUNK), 1).astype(F32)
    diff = ii - jj
    dmat = jnp.where(diff >= 0, jnp.exp(jnp.maximum(diff, 0.0) * lgf), jnp.exp(jnp.maximum(-diff, 0.0) * lgb))
    xi_f = jnp.exp((ii + 1.0) * lgf)
    zeta_f = jnp.exp((CHUNK - 1.0 - ii) * lgf)
    cd_f = jnp.exp(CHUNK * lgf)
    xi_b = jnp.exp((CHUNK - ii) * lgb)
    zeta_b = jnp.exp(ii * lgb)
    cd_b = jnp.exp(CHUNK * lgb)
    nt = (((1,), (1,)), ((), ()))
    tn_dims = (((0,), (0,)), ((), ()))
    gn = gn_ref[...]

    def chunk_slice(ch):
        return pl.ds(pl.multiple_of(ch * CHUNK, CHUNK), CHUNK)

    def increments(k, v):
        kf = k.astype(F32)
        kz = jnp.concatenate([(kf * zeta_f).astype(BF16), (kf * zeta_b).astype(BF16)], axis=1)
        return lax.dot_general(kz, v, tn_dims, preferred_element_type=F32)

    def stage1(gi, c):
        for u in range(group):
            ch = gi * group + u
            sl = chunk_slice(ch)
            ds_ref[ch] = increments(k_ref[sl, :], v_ref[sl, :])
        return c

    lax.fori_loop(0, n_chunks // group, stage1, 0)

    def scan_f(ch, S):
        sf_ref[ch] = S.astype(BF16)
        return S * cd_f + ds_ref[ch, :CHUNK, :]

    def scan_b(t, S):
        ch = n_chunks - 1 - t
        sb_ref[ch] = S.astype(BF16)
        return S * cd_b + ds_ref[ch, CHUNK:, :]

    S0 = increments(km_ref[...], vm_ref[...])[:CHUNK]
    lax.fori_loop(0, n_chunks, scan_f, S0, unroll=4)
    lax.fori_loop(0, n_chunks, scan_b, jnp.zeros((CHUNK, CHUNK), F32), unroll=4)

    def scores(ch):
        sl = chunk_slice(ch)
        q = q_ref[sl, :]
        s = lax.dot_general(q, k_ref[sl, :], nt, preferred_element_type=F32) * dmat
        qf = q.astype(F32)
        return jnp.concatenate([s.astype(BF16), (qf * xi_f).astype(BF16), (qf * xi_b).astype(BF16)], axis=1)

    def mix(ch, lhs):
        rhs = jnp.concatenate([v_ref[chunk_slice(ch), :], sf_ref[ch], sb_ref[ch]], axis=0)
        return jnp.dot(lhs, rhs, preferred_element_type=F32)

    def finish(ch, o):
        sl = chunk_slice(ch)
        mu = jnp.mean(o, axis=-1, keepdims=True)
        oc = o - mu
        var = jnp.mean(oc * oc, axis=-1, keepdims=True)
        on = oc * lax.rsqrt(var + LN_EPS) * gn
        g = g_ref[sl, :].astype(F32)
        o_ref[sl, :] = (g / (1.0 + jnp.exp(-g)) * on).astype(BF16)

    def stage3(gi, c):
        chs = [gi * group + u for u in range(group)]
        lhs = [scores(ch) for ch in chs]
        outs = [mix(ch, l) for ch, l in zip(chs, lhs)]
        for ch, o in zip(chs, outs):
            finish(ch, o)
        return c

    lax.fori_loop(0, n_chunks // group, stage3, 0)


def _retention(proj, proj_meta_pad, lgf, lgb, gn_g, B, n):
    n_chunks, group = n // CHUNK, 8
    assert n % CHUNK == 0 and n_chunks % group == 0
    seq_spec = lambda c0: pl.BlockSpec((n, LANES), lambda b, h: (b, c0 // LANES + h))
    meta_spec = lambda c0: pl.BlockSpec((CHUNK, LANES), lambda b, h: (0, c0 // LANES + h))
    lg_spec = pl.BlockSpec((1, 1, LANES), lambda b, h: (h, 0, 0))
    return pl.pallas_call(
        functools.partial(_ret_kernel, n_chunks=n_chunks, group=group),
        grid=(B, RET_HEADS),
        in_specs=[
            seq_spec(COL_QR), seq_spec(COL_KR), seq_spec(COL_VR), seq_spec(COL_GR),
            meta_spec(COL_KR), meta_spec(COL_VR),
            lg_spec, lg_spec,
            pl.BlockSpec((1, LANES), lambda b, h: (0, h)),
        ],
        out_specs=pl.BlockSpec((n, LANES), lambda b, h: (b, h)),
        out_shape=jax.ShapeDtypeStruct((B * n, RET_WIDTH), BF16),
        scratch_shapes=[pltpu.VMEM((n_chunks, 2 * CHUNK, LANES), F32),
                        pltpu.VMEM((n_chunks, CHUNK, LANES), BF16),
                        pltpu.VMEM((n_chunks, CHUNK, LANES), BF16)],
        compiler_params=pltpu.CompilerParams(
            dimension_semantics=("arbitrary", "arbitrary"), vmem_limit_bytes=VMEM_LIMIT),
        name="retention",
    )(proj, proj, proj, proj, proj_meta_pad, proj_meta_pad, lgf, lgb, gn_g)


def _outproj_kernel(ona_ref, oret_ref, x_ref, w_ref, gin_ref, bin_ref, g1_ref, b1_ref, o_ref, *, rb):
    tm = x_ref.shape[0]

    def norm_store(rows, mix):
        h = _layer_norm_rows(x_ref[rows, :], gin_ref[...], bin_ref[...])
        o_ref[rows, :] = _layer_norm_rows(ALPHA * h + mix, g1_ref[...], b1_ref[...])

    pending = None
    for sb in range(tm // rb):
        rows = slice(sb * rb, (sb + 1) * rb)
        lhs = jnp.concatenate([ona_ref[rows, :], oret_ref[rows, :]], axis=1)
        mix = jnp.dot(lhs, w_ref[...], preferred_element_type=F32)
        if pending is not None:
            norm_store(*pending)
        pending = (rows, mix)
    norm_store(*pending)


def _outproj(o_na, o_ret, x2d, w_bf, gin, bin_, g1, b1, tm=512):
    T = x2d.shape[0]
    assert T % tm == 0
    row_vec = pl.BlockSpec((1, D_MODEL), lambda i: (0, 0))
    return pl.pallas_call(
        functools.partial(_outproj_kernel, rb=256),
        grid=(T // tm,),
        in_specs=[
            pl.BlockSpec((tm, NA_WIDTH), lambda i: (i, 0)),
            pl.BlockSpec((tm, RET_WIDTH), lambda i: (i, 0)),
            pl.BlockSpec((tm, D_MODEL), lambda i: (i, 0)),
            pl.BlockSpec((D_MODEL, D_MODEL), lambda i: (0, 0)),
            row_vec, row_vec, row_vec, row_vec,
        ],
        out_specs=pl.BlockSpec((tm, D_MODEL), lambda i: (i, 0)),
        out_shape=jax.ShapeDtypeStruct((T, D_MODEL), F32),
        compiler_params=pltpu.CompilerParams(dimension_semantics=("arbitrary",), vmem_limit_bytes=VMEM_LIMIT),
        name="outproj_ln1",
    )(o_na, o_ret, x2d, w_bf, gin, bin_, g1, b1)


def _ffn_kernel(h_ref, wg_ref, wu_ref, wd_ref, g2_ref, b2_ref, o_ref, hb_ref, *, rb):
    j = pl.program_id(1)
    tm = h_ref.shape[0]

    @pl.when(j == 0)
    def _():
        hb_ref[...] = h_ref[...].astype(BF16)
        o_ref[...] = jnp.zeros_like(o_ref)

    hb = hb_ref[...]
    g = jnp.dot(hb, wg_ref[...], preferred_element_type=F32)
    u = jnp.dot(hb, wu_ref[...], preferred_element_type=F32)
    a = (g / (1.0 + jnp.exp(-g)) * u).astype(BF16)
    o_ref[...] += jnp.dot(a, wd_ref[...], preferred_element_type=F32)

    @pl.when(j == pl.num_programs(1) - 1)
    def _():
        def rows(i, c):
            sl = pl.ds(pl.multiple_of(i * rb, rb), rb)
            o_ref[sl, :] = _layer_norm_rows(ALPHA * h_ref[sl, :] + o_ref[sl, :], g2_ref[...], b2_ref[...])
            return c

        lax.fori_loop(0, tm // rb, rows, 0)


def _ffn(h1, wg_bf, wu_bf, wd_bf, g2, b2, tm=512, tf=512):
    T = h1.shape[0]
    assert T % tm == 0 and D_FF % tf == 0
    row_vec = pl.BlockSpec((1, D_MODEL), lambda i, j: (0, 0))
    return pl.pallas_call(
        functools.partial(_ffn_kernel, rb=128),
        grid=(T // tm, D_FF // tf),
        in_specs=[
            pl.BlockSpec((tm, D_MODEL), lambda i, j: (i, 0)),
            pl.BlockSpec((D_MODEL, tf), lambda i, j: (0, j)),
            pl.BlockSpec((D_MODEL, tf), lambda i, j: (0, j)),
            pl.BlockSpec((tf, D_MODEL), lambda i, j: (j, 0)),
            row_vec, row_vec,
        ],
        out_specs=pl.BlockSpec((tm, D_MODEL), lambda i, j: (i, 0)),
        out_shape=jax.ShapeDtypeStruct((T, D_MODEL), F32),
        scratch_shapes=[pltpu.VMEM((tm, D_MODEL), BF16)],
        compiler_params=pltpu.CompilerParams(
            dimension_semantics=("arbitrary", "arbitrary"), vmem_limit_bytes=VMEM_LIMIT),
        name="ffn_ln2",
    )(h1, wg_bf, wu_bf, wd_bf, g2, b2)


def _rope_tables(pos):
    half = RET_HEAD_DIM // 2
    inv = ROPE_BASE ** (-jnp.arange(half, dtype=F32) / half)
    ang = pos[:, None] * inv[None, :]
    c, s = jnp.cos(ang), jnp.sin(ang)
    cq = jnp.concatenate([c, c], axis=-1)
    sq = jnp.concatenate([-s, s], axis=-1)
    ks = RET_HEAD_DIM ** -0.5
    return (jnp.stack([cq, cq * ks]), jnp.stack([sq, sq * ks]))


def _pick_tm(n, cap):
    tm = cap
    while n % tm:
        tm //= 2
    return tm


def kernel(x_prompt, x_sample, meta_tokens, ln_in_g, ln_in_b, w_in, na_rpb, ret_decay_f, ret_decay_b,
           ret_gn_g, w_out, ln1_g, ln1_b, w_ffn_gate, w_ffn_up, w_ffn_down, ln2_g, ln2_b):
    row = lambda v: v.reshape(1, -1).astype(F32)
    gin, bin_ = row(ln_in_g), row(ln_in_b)
    w_in_bf = w_in[0].astype(BF16)
    w_out_bf = w_out[0].astype(BF16)
    wg_bf, wu_bf, wd_bf = w_ffn_gate[0].astype(BF16), w_ffn_up[0].astype(BF16), w_ffn_down[0].astype(BF16)
    bias = _na_bias(na_rpb[0])
    lg = lambda d: jnp.broadcast_to(jax.nn.log_sigmoid(d[0].astype(F32))[:, None, None], (RET_HEADS, 1, LANES))
    lgf, lgb = lg(ret_decay_f), lg(ret_decay_b)
    gn_g = row(ret_gn_g[0])

    meta_tabs = _rope_tables(jnp.arange(N_META, dtype=F32))
    proj_meta = _inproj(meta_tokens.astype(F32), gin, bin_, w_in_bf, meta_tabs, N_META, N_META)
    proj_meta_pad = jnp.pad(proj_meta, ((CHUNK - N_META, 0), (0, 0)))

    def group(x):
        B, n, _ = x.shape
        x2d = x.reshape(B * n, D_MODEL)
        tabs = _rope_tables(jnp.arange(n, dtype=F32) + float(N_META))
        proj = _inproj(x2d, gin, bin_, w_in_bf, tabs, n, _pick_tm(n, 1024))
        o_na = _na(proj, _na_meta(proj_meta), bias, B, n)
        o_ret = _retention(proj, proj_meta_pad, lgf, lgb, gn_g, B, n)
        h1 = _outproj(o_na, o_ret, x2d, w_out_bf, gin, bin_, row(ln1_g[0]), row(ln1_b[0]), tm=_pick_tm(B * n, 512))
        y = _ffn(h1, wg_bf, wu_bf, wd_bf, row(ln2_g[0]), row(ln2_b[0]), tm=_pick_tm(B * n, 1024))
        return y.reshape(B, n, D_MODEL)

    return (group(x_prompt), group(x_sample))


def _na_meta(proj_meta):
    return jnp.pad(proj_meta, ((0, LANES - N_META), (0, 0)))
```

```python
import functools

import jax
import jax.numpy as jnp
import numpy as np
from jax import lax
from jax.experimental import pallas as pl
from jax.experimental.pallas import tpu as pltpu

D_MODEL = 2048
N_META = 16
GRID_W = 64
NA_HEADS = 16
NA_HEAD_DIM = 64
NA_WIDTH = NA_HEADS * NA_HEAD_DIM
NA_WIN_H = 8
NA_WIN_W = 16
NA_HALF = GRID_W // 2
NA_KEY_COLS = 48
NA_KEY_SHIFT = GRID_W - NA_KEY_COLS
NA_KEYS = 512
RET_HEADS = 8
RET_HEAD_DIM = 128
RET_WIDTH = RET_HEADS * RET_HEAD_DIM
CHUNK = 128
ROPE_BASE = 10000.0
IN_WIDTH = 3 * NA_WIDTH + 4 * RET_WIDTH
D_FF = 5632
LN_EPS = 1e-5
ALPHA = 2.0 ** 0.25

COL_QA, COL_KA, COL_VA = 0, NA_WIDTH, 2 * NA_WIDTH
COL_QR = 3 * NA_WIDTH
COL_KR = COL_QR + RET_WIDTH
COL_VR = COL_KR + RET_WIDTH
COL_GR = COL_VR + RET_WIDTH

LANES = 128
NEG = -1e30
VMEM_LIMIT = 56 * 1024 * 1024

F32 = jnp.float32
BF16 = jnp.bfloat16


def _layer_norm_rows(x, g, b):
    mu = jnp.mean(x, axis=-1, keepdims=True)
    xc = x - mu
    var = jnp.mean(xc * xc, axis=-1, keepdims=True)
    return xc * lax.rsqrt(var + LN_EPS) * g + b


def _inproj_kernel(x_ref, g_ref, b_ref, w_ref, c_ref, s_ref, o_ref, hb_ref, *, tn, rb):
    j = pl.program_id(1)
    tm = x_ref.shape[0]

    col0 = j * tn
    rotary = jnp.logical_and(col0 >= COL_QR, col0 < COL_VR)

    @pl.when(j == 0)
    def _():
        for sb in range(tm // rb):
            rows = slice(sb * rb, (sb + 1) * rb)
            hb_ref[rows, :] = _layer_norm_rows(x_ref[rows, :], g_ref[...], b_ref[...]).astype(BF16)
            o_ref[rows, :] = jnp.dot(hb_ref[rows, :], w_ref[...], preferred_element_type=F32).astype(BF16)

    @pl.when(jnp.logical_and(j != 0, jnp.logical_not(rotary)))
    def _():
        o_ref[...] = jnp.dot(hb_ref[...], w_ref[...], preferred_element_type=F32).astype(BF16)

    @pl.when(rotary)
    def _():
        acc = jnp.dot(hb_ref[...], w_ref[...], preferred_element_type=F32)
        c = c_ref[...]
        s = s_ref[...]
        for t in range(tn // LANES):
            xs = acc[:, t * LANES:(t + 1) * LANES]
            o_ref[:, t * LANES:(t + 1) * LANES] = (xs * c + pltpu.roll(xs, LANES // 2, 1) * s).astype(BF16)


def _inproj(x2d, ln_g, ln_b, w_bf, tabs, n_seq, tm, tn=1024):
    T = x2d.shape[0]
    assert T % tm == 0 and n_seq % tm == 0 and IN_WIDTH % tn == 0 and RET_WIDTH % tn == 0
    blocks_per_seq = n_seq // tm

    def tab_index(i, j):
        col0 = j * tn
        is_k = jnp.logical_and(col0 >= COL_KR, col0 < COL_VR)
        return (jnp.where(is_k, 1, 0), i % blocks_per_seq, 0)

    tab_spec = pl.BlockSpec((None, tm, LANES), tab_index)
    rb = min(tm, 128)
    return pl.pallas_call(
        functools.partial(_inproj_kernel, tn=tn, rb=rb),
        grid=(T // tm, IN_WIDTH // tn),
        in_specs=[
            pl.BlockSpec((tm, D_MODEL), lambda i, j: (i, 0)),
            pl.BlockSpec((1, D_MODEL), lambda i, j: (0, 0)),
            pl.BlockSpec((1, D_MODEL), lambda i, j: (0, 0)),
            pl.BlockSpec((D_MODEL, tn), lambda i, j: (0, j)),
            tab_spec, tab_spec,
        ],
        out_specs=pl.BlockSpec((tm, tn), lambda i, j: (i, j)),
        out_shape=jax.ShapeDtypeStruct((T, IN_WIDTH), BF16),
        scratch_shapes=[pltpu.VMEM((tm, D_MODEL), BF16)],
        compiler_params=pltpu.CompilerParams(
            dimension_semantics=("arbitrary", "arbitrary"), vmem_limit_bytes=VMEM_LIMIT),
        name="inproj",
    )(x2d, ln_g, ln_b, w_bf, *tabs)


def _na_kernel(q_ref, k_ref, v_ref, km_ref, vm_ref, bias_ref, o_ref, *, rows, unroll, lead):
    lane = lax.broadcasted_iota(jnp.int32, (1, LANES), 1)
    scale = NA_HEAD_DIM ** -0.5
    qmask = [jnp.where((lane // NA_HEAD_DIM) == hh, scale, 0.0).astype(BF16) for hh in range(2)]
    km = km_ref[...]
    vm = vm_ref[...]
    nt = (((1,), (1,)), ((), ()))

    def window(ref, meta, rs, hf):
        parts = [ref[pl.ds(pl.multiple_of((rs + i) * GRID_W + NA_KEY_SHIFT * hf, NA_KEY_SHIFT), NA_KEY_COLS), :]
                 for i in range(NA_WIN_H)]
        return jnp.concatenate(parts + [meta], axis=0)

    units = [(u, hf) for u in range(unroll) for hf in range(2)]
    n_groups = rows // unroll

    def locate(g, idx):
        u, hf = units[idx]
        r = g * unroll + u
        rs = jnp.clip(r - NA_WIN_H // 2, 0, rows - NA_WIN_H)
        qsl = pl.ds(pl.multiple_of(r * GRID_W + NA_HALF * hf, NA_HALF), NA_HALF)
        return r - rs, qsl, rs, hf

    def scores(g, idx):
        off, qsl, rs, hf = locate(g, idx)
        q = q_ref[qsl, :]
        q2 = jnp.concatenate([q * qmask[0], q * qmask[1]], axis=0)
        s = lax.dot_general(q2, window(k_ref, km, rs, hf), nt, preferred_element_type=F32)
        return s + bias_ref[0, hf, pl.ds(off, 1)][0]

    def finish(s, g, idx):
        _, qsl, rs, hf = locate(g, idx)
        m = jnp.max(s, axis=-1, keepdims=True)
        p = jnp.exp(s - m)
        den = jnp.sum(p, axis=-1, keepdims=True)
        o2 = jnp.dot(p.astype(BF16), window(v_ref, vm, rs, hf), preferred_element_type=F32) / den
        o_ref[qsl, :] = jnp.where(lane < NA_HEAD_DIM, o2[:NA_HALF], o2[NA_HALF:]).astype(BF16)

    def body(g, c):
        staged = [scores(g, idx) for idx in range(lead)]
        for idx in range(len(units)):
            if idx + lead < len(units):
                staged.append(scores(g, idx + lead))
            finish(staged[idx], g, idx)
        return c

    lax.fori_loop(0, n_groups, body, 0)


def _na_bias_kernel(rpb_ref, o_ref):
    c32 = lax.broadcasted_iota(jnp.int32, (NA_HALF, LANES), 0)
    lane = lax.broadcasted_iota(jnp.int32, (NA_HALF, LANES), 1)
    meta = jnp.where(lane < N_META, 0.0, NEG).astype(F32)
    tiles = {}
    for hf in range(2):
        c = c32 + NA_HALF * hf
        kc = lane + NA_KEY_SHIFT * hf
        cs = jnp.clip(c - NA_WIN_W // 2, 0, GRID_W - NA_WIN_W)
        valid = (kc >= cs) & (kc < cs + NA_WIN_W) & (lane < NA_KEY_COLS)
        shift = (LANES - (NA_WIN_W - 1) - NA_KEY_SHIFT * hf + NA_HALF * hf) % LANES
        for h in range(2):
            for dr in range(2 * NA_WIN_H - 1):
                w = jnp.broadcast_to(rpb_ref[h, dr:dr + 1, :], (NA_HALF, LANES))
                tiles[(hf, h, dr)] = jnp.where(valid, pltpu.roll(w, shift, 1, stride=1, stride_axis=0), NEG)
    for hf in range(2):
        for off in range(NA_WIN_H):
            heads = [jnp.concatenate([tiles[(hf, h, i - off + NA_WIN_H - 1)][:, :NA_KEY_COLS]
                                      for i in range(NA_WIN_H)] + [meta], axis=1) for h in range(2)]
            o_ref[0, hf, off] = jnp.concatenate(heads, axis=0)


def _na_bias(rpb):
    n_dr = 2 * NA_WIN_H - 1
    rpb_lanes = jnp.pad(rpb.astype(F32), ((0, 0), (0, 0), (0, LANES - rpb.shape[-1])))
    shape = (NA_HEADS // 2, 2, NA_WIN_H, 2 * NA_HALF, NA_KEYS)
    return pl.pallas_call(
        _na_bias_kernel,
        grid=(NA_HEADS // 2,),
        in_specs=[pl.BlockSpec((2, n_dr, LANES), lambda p: (p, 0, 0))],
        out_specs=pl.BlockSpec((1,) + shape[1:], lambda p: (p, 0, 0, 0, 0)),
        out_shape=jax.ShapeDtypeStruct(shape, F32),
        compiler_params=pltpu.CompilerParams(dimension_semantics=("arbitrary",), vmem_limit_bytes=VMEM_LIMIT),
        name="na_bias",
    )(rpb_lanes)


def _na(proj, proj_meta_pad, bias, B, n):
    rows, unroll = n // GRID_W, 8
    assert rows >= NA_WIN_H and rows % unroll == 0
    pairs = NA_WIDTH // LANES
    seq_spec = lambda c0: pl.BlockSpec((n, LANES), lambda hp, b: (b, c0 // LANES + hp))
    meta_spec = lambda c0: pl.BlockSpec((LANES, LANES), lambda hp, b: (0, c0 // LANES + hp))
    return pl.pallas_call(
        functools.partial(_na_kernel, rows=rows, unroll=unroll, lead=4),
        grid=(pairs, B),
        in_specs=[
            seq_spec(COL_QA), seq_spec(COL_KA), seq_spec(COL_VA),
            meta_spec(COL_KA), meta_spec(COL_VA),
            pl.BlockSpec((1, 2, NA_WIN_H, 2 * NA_HALF, NA_KEYS), lambda hp, b: (hp, 0, 0, 0, 0)),
        ],
        out_specs=pl.BlockSpec((n, LANES), lambda hp, b: (b, hp)),
        out_shape=jax.ShapeDtypeStruct((B * n, NA_WIDTH), BF16),
        compiler_params=pltpu.CompilerParams(
            dimension_semantics=("arbitrary", "arbitrary"), vmem_limit_bytes=VMEM_LIMIT),
        name="na_attn",
    )(proj, proj, proj, proj_meta_pad, proj_meta_pad, bias)


def _ret_kernel(q_ref, k_ref, v_ref, g_ref, km_ref, vm_ref, lgf_ref, lgb_ref, gn_ref, o_ref,
                ds_ref, sf_ref, sb_ref, *, n_chunks, group):
    lgf = lgf_ref[0]
    lgb = lgb_ref[0]
    ii = lax.broadcasted_iota(jnp.int32, (CHUNK, CHUNK), 0).astype(F32)
    jj = lax.broadcasted_iota(jnp.int32, (CHUNK, CHUNK), 1).astype(F32)
    diff = ii - jj
    dmat = jnp.where(diff >= 0, jnp.exp(jnp.maximum(diff, 0.0) * lgf), jnp.exp(jnp.maximum(-diff, 0.0) * lgb))
    xi_f = jnp.exp((ii + 1.0) * lgf)
    zeta_f = jnp.exp((CHUNK - 1.0 - ii) * lgf)
    cd_f = jnp.exp(CHUNK * lgf)
    xi_b = jnp.exp((CHUNK - ii) * lgb)
    zeta_b = jnp.exp(ii * lgb)
    cd_b = jnp.exp(CHUNK * lgb)
    nt = (((1,), (1,)), ((), ()))
    tn_dims = (((0,), (0,)), ((), ()))
    gn = gn_ref[...]

    def chunk_slice(ch):
        return pl.ds(pl.multiple_of(ch * CHUNK, CHUNK), CHUNK)

    def increments(k, v):
        kf = k.astype(F32)
        kz = jnp.concatenate([(kf * zeta_f).astype(BF16), (kf * zeta_b).astype(BF16)], axis=1)
        return lax.dot_general(kz, v, tn_dims, preferred_element_type=F32)

    def stage1(gi, c):
        for u in range(group):
            ch = gi * group + u
            sl = chunk_slice(ch)
            ds_ref[ch] = increments(k_ref[sl, :], v_ref[sl, :])
        return c

    lax.fori_loop(0, n_chunks // group, stage1, 0)

    def scan_f(ch, S):
        sf_ref[ch] = S.astype(BF16)
        return S * cd_f + ds_ref[ch, :CHUNK, :]

    def scan_b(t, S):
        ch = n_chunks - 1 - t
        sb_ref[ch] = S.astype(BF16)
        return S * cd_b + ds_ref[ch, CHUNK:, :]

    S0 = increments(km_ref[...], vm_ref[...])[:CHUNK]
    lax.fori_loop(0, n_chunks, scan_f, S0, unroll=4)
    lax.fori_loop(0, n_chunks, scan_b, jnp.zeros((CHUNK, CHUNK), F32), unroll=4)

    def scores(ch):
        sl = chunk_slice(ch)
        q = q_ref[sl, :]
        s = lax.dot_general(q, k_ref[sl, :], nt, preferred_element_type=F32) * dmat
        qf = q.astype(F32)
        return jnp.concatenate([s.astype(BF16), (qf * xi_f).astype(BF16), (qf * xi_b).astype(BF16)], axis=1)

    def mix(ch, lhs):
        rhs = jnp.concatenate([v_ref[chunk_slice(ch), :], sf_ref[ch], sb_ref[ch]], axis=0)
        return jnp.dot(lhs, rhs, preferred_element_type=F32)

    def finish(ch, o):
        sl = chunk_slice(ch)
        mu = jnp.mean(o, axis=-1, keepdims=True)
        oc = o - mu
        var = jnp.mean(oc * oc, axis=-1, keepdims=True)
        on = oc * lax.rsqrt(var + LN_EPS) * gn
        g = g_ref[sl, :].astype(F32)
        o_ref[sl, :] = (g / (1.0 + jnp.exp(-g)) * on).astype(BF16)

    def stage3(gi, c):
        chs = [gi * group + u for u in range(group)]
        lhs = [scores(ch) for ch in chs]
        outs = [mix(ch, l) for ch, l in zip(chs, lhs)]
        for ch, o in zip(chs, outs):
            finish(ch, o)
        return c

    lax.fori_loop(0, n_chunks // group, stage3, 0)


def _retention(proj, proj_meta_pad, lgf, lgb, gn_g, B, n):
    n_chunks, group = n // CHUNK, 8
    assert n % CHUNK == 0 and n_chunks % group == 0
    seq_spec = lambda c0: pl.BlockSpec((n, LANES), lambda b, h: (b, c0 // LANES + h))
    meta_spec = lambda c0: pl.BlockSpec((CHUNK, LANES), lambda b, h: (0, c0 // LANES + h))
    lg_spec = pl.BlockSpec((1, 1, LANES), lambda b, h: (h, 0, 0))
    return pl.pallas_call(
        functools.partial(_ret_kernel, n_chunks=n_chunks, group=group),
        grid=(B, RET_HEADS),
        in_specs=[
            seq_spec(COL_QR), seq_spec(COL_KR), seq_spec(COL_VR), seq_spec(COL_GR),
            meta_spec(COL_KR), meta_spec(COL_VR),
            lg_spec, lg_spec,
            pl.BlockSpec((1, LANES), lambda b, h: (0, h)),
        ],
        out_specs=pl.BlockSpec((n, LANES), lambda b, h: (b, h)),
        out_shape=jax.ShapeDtypeStruct((B * n, RET_WIDTH), BF16),
        scratch_shapes=[pltpu.VMEM((n_chunks, 2 * CHUNK, LANES), F32),
                        pltpu.VMEM((n_chunks, CHUNK, LANES), BF16),
                        pltpu.VMEM((n_chunks, CHUNK, LANES), BF16)],
        compiler_params=pltpu.CompilerParams(
            dimension_semantics=("arbitrary", "arbitrary"), vmem_limit_bytes=VMEM_LIMIT),
        name="retention",
    )(proj, proj, proj, proj, proj_meta_pad, proj_meta_pad, lgf, lgb, gn_g)


def _outproj_kernel(ona_ref, oret_ref, x_ref, w_ref, gin_ref, bin_ref, g1_ref, b1_ref, o_ref, *, rb):
    tm = x_ref.shape[0]

    def norm_store(rows, mix):
        h = _layer_norm_rows(x_ref[rows, :], gin_ref[...], bin_ref[...])
        o_ref[rows, :] = _layer_norm_rows(ALPHA * h + mix, g1_ref[...], b1_ref[...])

    pending = None
    for sb in range(tm // rb):
        rows = slice(sb * rb, (sb + 1) * rb)
        lhs = jnp.concatenate([ona_ref[rows, :], oret_ref[rows, :]], axis=1)
        mix = jnp.dot(lhs, w_ref[...], preferred_element_type=F32)
        if pending is not None:
            norm_store(*pending)
        pending = (rows, mix)
    norm_store(*pending)


def _outproj(o_na, o_ret, x2d, w_bf, gin, bin_, g1, b1, tm=512):
    T = x2d.shape[0]
    assert T % tm == 0
    row_vec = pl.BlockSpec((1, D_MODEL), lambda i: (0, 0))
    return pl.pallas_call(
        functools.partial(_outproj_kernel, rb=256),
        grid=(T // tm,),
        in_specs=[
            pl.BlockSpec((tm, NA_WIDTH), lambda i: (i, 0)),
            pl.BlockSpec((tm, RET_WIDTH), lambda i: (i, 0)),
            pl.BlockSpec((tm, D_MODEL), lambda i: (i, 0)),
            pl.BlockSpec((D_MODEL, D_MODEL), lambda i: (0, 0), pipeline_mode=pl.Buffered(1)),
            row_vec, row_vec, row_vec, row_vec,
        ],
        out_specs=pl.BlockSpec((tm, D_MODEL), lambda i: (i, 0)),
        out_shape=jax.ShapeDtypeStruct((T, D_MODEL), F32),
        compiler_params=pltpu.CompilerParams(dimension_semantics=("arbitrary",), vmem_limit_bytes=VMEM_LIMIT),
        name="outproj_ln1",
    )(o_na, o_ret, x2d, w_bf, gin, bin_, g1, b1)


def _ffn_kernel(h_ref, wg_ref, wu_ref, wd_ref, g2_ref, b2_ref, o_ref, hb_ref, *, rb):
    j = pl.program_id(1)
    tm = h_ref.shape[0]

    @pl.when(j == 0)
    def _():
        hb_ref[...] = h_ref[...].astype(BF16)
        o_ref[...] = jnp.zeros_like(o_ref)

    hb = hb_ref[...]
    g = jnp.dot(hb, wg_ref[...], preferred_element_type=F32)
    u = jnp.dot(hb, wu_ref[...], preferred_element_type=F32)
    a = (g / (1.0 + jnp.exp(-g)) * u).astype(BF16)
    o_ref[...] += jnp.dot(a, wd_ref[...], preferred_element_type=F32)

    @pl.when(j == pl.num_programs(1) - 1)
    def _():
        def rows(i, c):
            sl = pl.ds(pl.multiple_of(i * rb, rb), rb)
            o_ref[sl, :] = _layer_norm_rows(ALPHA * h_ref[sl, :] + o_ref[sl, :], g2_ref[...], b2_ref[...])
            return c

        lax.fori_loop(0, tm // rb, rows, 0)


def _ffn(h1, wg_bf, wu_bf, wd_bf, g2, b2, tm=512, tf=512):
    T = h1.shape[0]
    assert T % tm == 0 and D_FF % tf == 0
    row_vec = pl.BlockSpec((1, D_MODEL), lambda i, j: (0, 0))
    return pl.pallas_call(
        functools.partial(_ffn_kernel, rb=128),
        grid=(T // tm, D_FF // tf),
        in_specs=[
            pl.BlockSpec((tm, D_MODEL), lambda i, j: (i, 0)),
            pl.BlockSpec((D_MODEL, tf), lambda i, j: (0, j)),
            pl.BlockSpec((D_MODEL, tf), lambda i, j: (0, j)),
            pl.BlockSpec((tf, D_MODEL), lambda i, j: (j, 0)),
            row_vec, row_vec,
        ],
        out_specs=pl.BlockSpec((tm, D_MODEL), lambda i, j: (i, 0)),
        out_shape=jax.ShapeDtypeStruct((T, D_MODEL), F32),
        scratch_shapes=[pltpu.VMEM((tm, D_MODEL), BF16)],
        compiler_params=pltpu.CompilerParams(
            dimension_semantics=("arbitrary", "arbitrary"), vmem_limit_bytes=VMEM_LIMIT),
        name="ffn_ln2",
    )(h1, wg_bf, wu_bf, wd_bf, g2, b2)


def _rope_tables(pos):
    half = RET_HEAD_DIM // 2
    inv = ROPE_BASE ** (-jnp.arange(half, dtype=F32) / half)
    ang = pos[:, None] * inv[None, :]
    c, s = jnp.cos(ang), jnp.sin(ang)
    cq = jnp.concatenate([c, c], axis=-1)
    sq = jnp.concatenate([-s, s], axis=-1)
    ks = RET_HEAD_DIM ** -0.5
    return (jnp.stack([cq, cq * ks]), jnp.stack([sq, sq * ks]))


def _pick_tm(n, cap):
    tm = cap
    while n % tm:
        tm //= 2
    return tm


def kernel(x_prompt, x_sample, meta_tokens, ln_in_g, ln_in_b, w_in, na_rpb, ret_decay_f, ret_decay_b,
           ret_gn_g, w_out, ln1_g, ln1_b, w_ffn_gate, w_ffn_up, w_ffn_down, ln2_g, ln2_b):
    row = lambda v: v.reshape(1, -1).astype(F32)
    gin, bin_ = row(ln_in_g), row(ln_in_b)
    w_in_bf = w_in[0].astype(BF16)
    w_out_bf = w_out[0].astype(BF16)
    wg_bf, wu_bf, wd_bf = w_ffn_gate[0].astype(BF16), w_ffn_up[0].astype(BF16), w_ffn_down[0].astype(BF16)
    bias = _na_bias(na_rpb[0])
    lg = lambda d: jnp.broadcast_to(jax.nn.log_sigmoid(d[0].astype(F32))[:, None, None], (RET_HEADS, 1, LANES))
    lgf, lgb = lg(ret_decay_f), lg(ret_decay_b)
    gn_g = row(ret_gn_g[0])

    meta_tabs = _rope_tables(jnp.arange(N_META, dtype=F32))
    proj_meta = _inproj(meta_tokens.astype(F32), gin, bin_, w_in_bf, meta_tabs, N_META, N_META)
    proj_meta_pad = jnp.pad(proj_meta, ((CHUNK - N_META, 0), (0, 0)))

    def group(x):
        B, n, _ = x.shape
        x2d = x.reshape(B * n, D_MODEL)
        tabs = _rope_tables(jnp.arange(n, dtype=F32) + float(N_META))
        proj = _inproj(x2d, gin, bin_, w_in_bf, tabs, n, _pick_tm(n, 1024))
        o_na = _na(proj, _na_meta(proj_meta), bias, B, n)
        o_ret = _retention(proj, proj_meta_pad, lgf, lgb, gn_g, B, n)
        h1 = _outproj(o_na, o_ret, x2d, w_out_bf, gin, bin_, row(ln1_g[0]), row(ln1_b[0]), tm=_pick_tm(B * n, 512))
        y = _ffn(h1, wg_bf, wu_bf, wd_bf, row(ln2_g[0]), row(ln2_b[0]), tm=_pick_tm(B * n, 1024))
        return y.reshape(B, n, D_MODEL)

    return (group(x_prompt), group(x_sample))


def _na_meta(proj_meta):
    return jnp.pad(proj_meta, ((0, LANES - N_META), (0, 0)))
```

```python
import functools

import jax
import jax.numpy as jnp
import numpy as np
from jax import lax
from jax.experimental import pallas as pl
from jax.experimental.pallas import tpu as pltpu

D_MODEL = 2048
N_META = 16
GRID_W = 64
NA_HEADS = 16
NA_HEAD_DIM = 64
NA_WIDTH = NA_HEADS * NA_HEAD_DIM
NA_WIN_H = 8
NA_WIN_W = 16
NA_HALF = GRID_W // 2
NA_KEY_COLS = 48
NA_KEY_SHIFT = GRID_W - NA_KEY_COLS
NA_UNION = NA_WIN_H + 1
NA_KEYS = 512
NA_META_SLOTS = NA_KEYS - NA_UNION * NA_KEY_COLS
RET_HEADS = 8
RET_HEAD_DIM = 128
RET_WIDTH = RET_HEADS * RET_HEAD_DIM
CHUNK = 128
ROPE_BASE = 10000.0
IN_WIDTH = 3 * NA_WIDTH + 4 * RET_WIDTH
D_FF = 5632
LN_EPS = 1e-5
ALPHA = 2.0 ** 0.25

COL_QA, COL_KA, COL_VA = 0, NA_WIDTH, 2 * NA_WIDTH
COL_QR = 3 * NA_WIDTH
COL_KR = COL_QR + RET_WIDTH
COL_VR = COL_KR + RET_WIDTH
COL_GR = COL_VR + RET_WIDTH

LANES = 128
NEG = -1e30
VMEM_LIMIT = 56 * 1024 * 1024

F32 = jnp.float32
BF16 = jnp.bfloat16


def _layer_norm_rows(x, g, b):
    mu = jnp.mean(x, axis=-1, keepdims=True)
    xc = x - mu
    var = jnp.mean(xc * xc, axis=-1, keepdims=True)
    return xc * lax.rsqrt(var + LN_EPS) * g + b


def _inproj_kernel(x_ref, g_ref, b_ref, w_ref, c_ref, s_ref, o_ref, hb_ref, *, tn, rb):
    j = pl.program_id(1)
    tm = x_ref.shape[0]

    col0 = j * tn
    rotary = jnp.logical_and(col0 >= COL_QR, col0 < COL_VR)

    @pl.when(j == 0)
    def _():
        for sb in range(tm // rb):
            rows = slice(sb * rb, (sb + 1) * rb)
            hb_ref[rows, :] = _layer_norm_rows(x_ref[rows, :], g_ref[...], b_ref[...]).astype(BF16)
            o_ref[rows, :] = jnp.dot(hb_ref[rows, :], w_ref[...], preferred_element_type=F32).astype(BF16)

    @pl.when(jnp.logical_and(j != 0, jnp.logical_not(rotary)))
    def _():
        o_ref[...] = jnp.dot(hb_ref[...], w_ref[...], preferred_element_type=F32).astype(BF16)

    @pl.when(rotary)
    def _():
        acc = jnp.dot(hb_ref[...], w_ref[...], preferred_element_type=F32)
        c = c_ref[...]
        s = s_ref[...]
        for t in range(tn // LANES):
            xs = acc[:, t * LANES:(t + 1) * LANES]
            o_ref[:, t * LANES:(t + 1) * LANES] = (xs * c + pltpu.roll(xs, LANES // 2, 1) * s).astype(BF16)


def _inproj(x2d, ln_g, ln_b, w_bf, tabs, n_seq, tm, tn=1024):
    T = x2d.shape[0]
    assert T % tm == 0 and n_seq % tm == 0 and IN_WIDTH % tn == 0 and RET_WIDTH % tn == 0
    blocks_per_seq = n_seq // tm

    def tab_index(i, j):
        col0 = j * tn
        is_k = jnp.logical_and(col0 >= COL_KR, col0 < COL_VR)
        return (jnp.where(is_k, 1, 0), i % blocks_per_seq, 0)

    tab_spec = pl.BlockSpec((None, tm, LANES), tab_index)
    rb = min(tm, 128)
    return pl.pallas_call(
        functools.partial(_inproj_kernel, tn=tn, rb=rb),
        grid=(T // tm, IN_WIDTH // tn),
        in_specs=[
            pl.BlockSpec((tm, D_MODEL), lambda i, j: (i, 0)),
            pl.BlockSpec((1, D_MODEL), lambda i, j: (0, 0)),
            pl.BlockSpec((1, D_MODEL), lambda i, j: (0, 0)),
            pl.BlockSpec((D_MODEL, tn), lambda i, j: (0, j)),
            tab_spec, tab_spec,
        ],
        out_specs=pl.BlockSpec((tm, tn), lambda i, j: (i, j)),
        out_shape=jax.ShapeDtypeStruct((T, IN_WIDTH), BF16),
        scratch_shapes=[pltpu.VMEM((tm, D_MODEL), BF16)],
        compiler_params=pltpu.CompilerParams(
            dimension_semantics=("arbitrary", "arbitrary"), vmem_limit_bytes=VMEM_LIMIT),
        name="inproj",
    )(x2d, ln_g, ln_b, w_bf, *tabs)


def _na_kernel(q_ref, k_ref, v_ref, km_ref, vm_ref, bias_ref, o_ref, *, rows, unroll):
    lane = lax.broadcasted_iota(jnp.int32, (1, LANES), 1)
    scale = NA_HEAD_DIM ** -0.5
    qmask = [jnp.where((lane // NA_HEAD_DIM) == hh, scale, 0.0).astype(BF16) for hh in range(2)]
    km = km_ref[...]
    vm = vm_ref[...]
    nt = (((1,), (1,)), ((), ()))

    def window(ref, meta, b0, hf):
        parts = [ref[pl.ds(pl.multiple_of((b0 + i) * GRID_W + NA_KEY_SHIFT * hf, NA_KEY_SHIFT), NA_KEY_COLS), :]
                 for i in range(NA_UNION)]
        return jnp.concatenate(parts + [meta], axis=0)

    units = [(u, hf) for u in range(unroll // 2) for hf in range(2)]

    def locate(g, idx):
        u, hf = units[idx]
        r0 = g * unroll + 2 * u
        b0 = jnp.clip(r0 - NA_WIN_H // 2, 0, rows - NA_UNION)
        qsl = [pl.ds(pl.multiple_of((r0 + t) * GRID_W + NA_HALF * hf, NA_HALF), NA_HALF) for t in range(2)]
        return r0 - b0, qsl, b0, hf

    def scores(g, idx):
        e0, qsl, b0, hf = locate(g, idx)
        q2 = jnp.concatenate([q_ref[sl, :] * qmask[hh] for sl in qsl for hh in range(2)], axis=0)
        s = lax.dot_general(q2, window(k_ref, km, b0, hf), nt, preferred_element_type=F32)
        bias = jnp.concatenate([bias_ref[0, hf, pl.ds(e0 + t, 1)][0] for t in range(2)], axis=0)
        return s + bias

    def finish(s, g, idx):
        _, qsl, b0, hf = locate(g, idx)
        m = jnp.max(s, axis=-1, keepdims=True)
        p = jnp.exp(s - m)
        den = jnp.sum(p, axis=-1, keepdims=True)
        o2 = jnp.dot(p.astype(BF16), window(v_ref, vm, b0, hf), preferred_element_type=F32) / den
        for t in range(2):
            blk = o2[2 * NA_HALF * t:2 * NA_HALF * (t + 1)]
            o_ref[qsl[t], :] = jnp.where(lane < NA_HEAD_DIM, blk[:NA_HALF], blk[NA_HALF:]).astype(BF16)

    def body(g, c):
        staged = [scores(g, idx) for idx in range(len(units))]
        for idx in range(len(units)):
            finish(staged[idx], g, idx)
        return c

    lax.fori_loop(0, rows // unroll, body, 0)


def _na_bias_kernel(rpb_ref, o_ref):
    c32 = lax.broadcasted_iota(jnp.int32, (NA_HALF, LANES), 0)
    lane = lax.broadcasted_iota(jnp.int32, (NA_HALF, LANES), 1)
    masked = jnp.where(lane + c32 < 0, 0.0, NEG).astype(F32)[:, :NA_KEY_COLS]
    meta = jnp.where(lane + LANES * c32 < N_META + LANES * c32, 0.0, NEG).astype(F32)[:, :NA_META_SLOTS]
    tiles = {}
    for hf in range(2):
        c = c32 + NA_HALF * hf
        kc = lane + NA_KEY_SHIFT * hf
        cs = jnp.clip(c - NA_WIN_W // 2, 0, GRID_W - NA_WIN_W)
        valid = (kc >= cs) & (kc < cs + NA_WIN_W) & (lane < NA_KEY_COLS)
        shift = (LANES - (NA_WIN_W - 1) - NA_KEY_SHIFT * hf + NA_HALF * hf) % LANES
        for h in range(2):
            for dr in range(2 * NA_WIN_H - 1):
                w = jnp.broadcast_to(rpb_ref[h, dr:dr + 1, :], (NA_HALF, LANES))
                tiles[(hf, h, dr)] = jnp.where(valid, pltpu.roll(w, shift, 1, stride=1, stride_axis=0), NEG)
    for hf in range(2):
        for e in range(NA_UNION):
            d = 1 if e > NA_WIN_H // 2 else 0
            heads = []
            for h in range(2):
                parts = [tiles[(hf, h, i - e + NA_WIN_H - 1)][:, :NA_KEY_COLS] if 0 <= i - d < NA_WIN_H else masked
                         for i in range(NA_UNION)]
                heads.append(jnp.concatenate(parts + [meta], axis=1))
            o_ref[0, hf, e] = jnp.concatenate(heads, axis=0)


def _na_bias(rpb):
    n_dr = 2 * NA_WIN_H - 1
    rpb_lanes = jnp.pad(rpb.astype(F32), ((0, 0), (0, 0), (0, LANES - rpb.shape[-1])))
    shape = (NA_HEADS // 2, 2, NA_UNION, 2 * NA_HALF, NA_KEYS)
    return pl.pallas_call(
        _na_bias_kernel,
        grid=(NA_HEADS // 2,),
        in_specs=[pl.BlockSpec((2, n_dr, LANES), lambda p: (p, 0, 0))],
        out_specs=pl.BlockSpec((1,) + shape[1:], lambda p: (p, 0, 0, 0, 0)),
        out_shape=jax.ShapeDtypeStruct(shape, F32),
        compiler_params=pltpu.CompilerParams(dimension_semantics=("arbitrary",), vmem_limit_bytes=VMEM_LIMIT),
        name="na_bias",
    )(rpb_lanes)


def _na(proj, proj_meta_keys, bias, B, n):
    rows, unroll = n // GRID_W, 8
    assert rows >= NA_UNION and rows % unroll == 0
    pairs = NA_WIDTH // LANES
    seq_spec = lambda c0: pl.BlockSpec((n, LANES), lambda hp, b: (b, c0 // LANES + hp))
    meta_spec = lambda c0: pl.BlockSpec((NA_META_SLOTS, LANES), lambda hp, b: (0, c0 // LANES + hp))
    return pl.pallas_call(
        functools.partial(_na_kernel, rows=rows, unroll=unroll),
        grid=(pairs, B),
        in_specs=[
            seq_spec(COL_QA), seq_spec(COL_KA), seq_spec(COL_VA),
            meta_spec(COL_KA), meta_spec(COL_VA),
            pl.BlockSpec((1, 2, NA_UNION, 2 * NA_HALF, NA_KEYS), lambda hp, b: (hp, 0, 0, 0, 0)),
        ],
        out_specs=pl.BlockSpec((n, LANES), lambda hp, b: (b, hp)),
        out_shape=jax.ShapeDtypeStruct((B * n, NA_WIDTH), BF16),
        compiler_params=pltpu.CompilerParams(
            dimension_semantics=("arbitrary", "arbitrary"), vmem_limit_bytes=VMEM_LIMIT),
        name="na_attn",
    )(proj, proj, proj, proj_meta_keys, proj_meta_keys, bias)


def _ret_kernel(q_ref, k_ref, v_ref, g_ref, km_ref, vm_ref, lgf_ref, lgb_ref, gn_ref, o_ref,
                ds_ref, sf_ref, sb_ref, *, n_chunks, group):
    lgf = lgf_ref[0]
    lgb = lgb_ref[0]
    ii = lax.broadcasted_iota(jnp.int32, (CHUNK, CHUNK), 0).astype(F32)
    jj = lax.broadcasted_iota(jnp.int32, (CHUNK, CHUNK), 1).astype(F32)
    diff = ii - jj
    dmat = jnp.where(diff >= 0, jnp.exp(jnp.maximum(diff, 0.0) * lgf), jnp.exp(jnp.maximum(-diff, 0.0) * lgb))
    xi_f = jnp.exp((ii + 1.0) * lgf)
    zeta_f = jnp.exp((CHUNK - 1.0 - ii) * lgf)
    cd_f = jnp.exp(CHUNK * lgf)
    xi_b = jnp.exp((CHUNK - ii) * lgb)
    zeta_b = jnp.exp(ii * lgb)
    cd_b = jnp.exp(CHUNK * lgb)
    nt = (((1,), (1,)), ((), ()))
    tn_dims = (((0,), (0,)), ((), ()))
    gn = gn_ref[...]

    def chunk_slice(ch):
        return pl.ds(pl.multiple_of(ch * CHUNK, CHUNK), CHUNK)

    def increments(k, v):
        kf = k.astype(F32)
        kz = jnp.concatenate([(kf * zeta_f).astype(BF16), (kf * zeta_b).astype(BF16)], axis=1)
        return lax.dot_general(kz, v, tn_dims, preferred_element_type=F32)

    def stage1(gi, c):
        for u in range(group):
            ch = gi * group + u
            sl = chunk_slice(ch)
            ds_ref[ch] = increments(k_ref[sl, :], v_ref[sl, :])
        return c

    lax.fori_loop(0, n_chunks // group, stage1, 0)

    def scan_f(ch, S):
        sf_ref[ch] = S.astype(BF16)
        return S * cd_f + ds_ref[ch, :CHUNK, :]

    def scan_b(t, S):
        ch = n_chunks - 1 - t
        sb_ref[ch] = S.astype(BF16)
        return S * cd_b + ds_ref[ch, CHUNK:, :]

    S0 = increments(km_ref[...], vm_ref[...])[:CHUNK]
    lax.fori_loop(0, n_chunks, scan_f, S0, unroll=4)
    lax.fori_loop(0, n_chunks, scan_b, jnp.zeros((CHUNK, CHUNK), F32), unroll=4)

    def scores(ch):
        sl = chunk_slice(ch)
        q = q_ref[sl, :]
        s = lax.dot_general(q, k_ref[sl, :], nt, preferred_element_type=F32) * dmat
        qf = q.astype(F32)
        return jnp.concatenate([s.astype(BF16), (qf * xi_f).astype(BF16), (qf * xi_b).astype(BF16)], axis=1)

    def mix(ch, lhs):
        rhs = jnp.concatenate([v_ref[chunk_slice(ch), :], sf_ref[ch], sb_ref[ch]], axis=0)
        return jnp.dot(lhs, rhs, preferred_element_type=F32)

    def finish(ch, o):
        sl = chunk_slice(ch)
        mu = jnp.mean(o, axis=-1, keepdims=True)
        oc = o - mu
        var = jnp.mean(oc * oc, axis=-1, keepdims=True)
        on = oc * lax.rsqrt(var + LN_EPS) * gn
        g = g_ref[sl, :].astype(F32)
        o_ref[sl, :] = (g / (1.0 + jnp.exp(-g)) * on).astype(BF16)

    def stage3(gi, c):
        chs = [gi * group + u for u in range(group)]
        lhs = [scores(ch) for ch in chs]
        outs = [mix(ch, l) for ch, l in zip(chs, lhs)]
        for ch, o in zip(chs, outs):
            finish(ch, o)
        return c

    lax.fori_loop(0, n_chunks // group, stage3, 0)


def _retention(proj, proj_meta_pad, lgf, lgb, gn_g, B, n):
    n_chunks, group = n // CHUNK, 8
    assert n % CHUNK == 0 and n_chunks % group == 0
    seq_spec = lambda c0: pl.BlockSpec((n, LANES), lambda b, h: (b, c0 // LANES + h))
    meta_spec = lambda c0: pl.BlockSpec((CHUNK, LANES), lambda b, h: (0, c0 // LANES + h))
    lg_spec = pl.BlockSpec((1, 1, LANES), lambda b, h: (h, 0, 0))
    return pl.pallas_call(
        functools.partial(_ret_kernel, n_chunks=n_chunks, group=group),
        grid=(B, RET_HEADS),
        in_specs=[
            seq_spec(COL_QR), seq_spec(COL_KR), seq_spec(COL_VR), seq_spec(COL_GR),
            meta_spec(COL_KR), meta_spec(COL_VR),
            lg_spec, lg_spec,
            pl.BlockSpec((1, LANES), lambda b, h: (0, h)),
        ],
        out_specs=pl.BlockSpec((n, LANES), lambda b, h: (b, h)),
        out_shape=jax.ShapeDtypeStruct((B * n, RET_WIDTH), BF16),
        scratch_shapes=[pltpu.VMEM((n_chunks, 2 * CHUNK, LANES), F32),
                        pltpu.VMEM((n_chunks, CHUNK, LANES), BF16),
                        pltpu.VMEM((n_chunks, CHUNK, LANES), BF16)],
        compiler_params=pltpu.CompilerParams(
            dimension_semantics=("arbitrary", "arbitrary"), vmem_limit_bytes=VMEM_LIMIT),
        name="retention",
    )(proj, proj, proj, proj, proj_meta_pad, proj_meta_pad, lgf, lgb, gn_g)


def _outproj_kernel(ona_ref, oret_ref, x_ref, w_ref, gin_ref, bin_ref, g1_ref, b1_ref, o_ref, *, rb):
    tm = x_ref.shape[0]

    def norm_store(rows, mix):
        h = _layer_norm_rows(x_ref[rows, :], gin_ref[...], bin_ref[...])
        o_ref[rows, :] = _layer_norm_rows(ALPHA * h + mix, g1_ref[...], b1_ref[...])

    pending = None
    for sb in range(tm // rb):
        rows = slice(sb * rb, (sb + 1) * rb)
        lhs = jnp.concatenate([ona_ref[rows, :], oret_ref[rows, :]], axis=1)
        mix = jnp.dot(lhs, w_ref[...], preferred_element_type=F32)
        if pending is not None:
            norm_store(*pending)
        pending = (rows, mix)
    norm_store(*pending)


def _outproj(o_na, o_ret, x2d, w_bf, gin, bin_, g1, b1, tm=512):
    T = x2d.shape[0]
    assert T % tm == 0
    row_vec = pl.BlockSpec((1, D_MODEL), lambda i: (0, 0))
    return pl.pallas_call(
        functools.partial(_outproj_kernel, rb=256),
        grid=(T // tm,),
        in_specs=[
            pl.BlockSpec((tm, NA_WIDTH), lambda i: (i, 0)),
            pl.BlockSpec((tm, RET_WIDTH), lambda i: (i, 0)),
            pl.BlockSpec((tm, D_MODEL), lambda i: (i, 0)),
            pl.BlockSpec((D_MODEL, D_MODEL), lambda i: (0, 0), pipeline_mode=pl.Buffered(1)),
            row_vec, row_vec, row_vec, row_vec,
        ],
        out_specs=pl.BlockSpec((tm, D_MODEL), lambda i: (i, 0)),
        out_shape=jax.ShapeDtypeStruct((T, D_MODEL), F32),
        compiler_params=pltpu.CompilerParams(dimension_semantics=("arbitrary",), vmem_limit_bytes=VMEM_LIMIT),
        name="outproj_ln1",
    )(o_na, o_ret, x2d, w_bf, gin, bin_, g1, b1)


def _ffn_kernel(h_ref, wg_ref, wu_ref, wd_ref, g2_ref, b2_ref, o_ref, hb_ref, *, rb):
    j = pl.program_id(1)
    tm = h_ref.shape[0]

    @pl.when(j == 0)
    def _():
        hb_ref[...] = h_ref[...].astype(BF16)
        o_ref[...] = jnp.zeros_like(o_ref)

    hb = hb_ref[...]
    g = jnp.dot(hb, wg_ref[...], preferred_element_type=F32)
    u = jnp.dot(hb, wu_ref[...], preferred_element_type=F32)
    a = (g / (1.0 + jnp.exp(-g)) * u).astype(BF16)
    o_ref[...] += jnp.dot(a, wd_ref[...], preferred_element_type=F32)

    @pl.when(j == pl.num_programs(1) - 1)
    def _():
        def rows(i, c):
            sl = pl.ds(pl.multiple_of(i * rb, rb), rb)
            o_ref[sl, :] = _layer_norm_rows(ALPHA * h_ref[sl, :] + o_ref[sl, :], g2_ref[...], b2_ref[...])
            return c

        lax.fori_loop(0, tm // rb, rows, 0)


def _ffn(h1, wg_bf, wu_bf, wd_bf, g2, b2, tm=512, tf=512):
    T = h1.shape[0]
    assert T % tm == 0 and D_FF % tf == 0
    row_vec = pl.BlockSpec((1, D_MODEL), lambda i, j: (0, 0))
    return pl.pallas_call(
        functools.partial(_ffn_kernel, rb=128),
        grid=(T // tm, D_FF // tf),
        in_specs=[
            pl.BlockSpec((tm, D_MODEL), lambda i, j: (i, 0)),
            pl.BlockSpec((D_MODEL, tf), lambda i, j: (0, j)),
            pl.BlockSpec((D_MODEL, tf), lambda i, j: (0, j)),
            pl.BlockSpec((tf, D_MODEL), lambda i, j: (j, 0)),
            row_vec, row_vec,
        ],
        out_specs=pl.BlockSpec((tm, D_MODEL), lambda i, j: (i, 0)),
        out_shape=jax.ShapeDtypeStruct((T, D_MODEL), F32),
        scratch_shapes=[pltpu.VMEM((tm, D_MODEL), BF16)],
        compiler_params=pltpu.CompilerParams(
            dimension_semantics=("arbitrary", "arbitrary"), vmem_limit_bytes=VMEM_LIMIT),
        name="ffn_ln2",
    )(h1, wg_bf, wu_bf, wd_bf, g2, b2)


def _rope_tables(pos):
    half = RET_HEAD_DIM // 2
    inv = ROPE_BASE ** (-jnp.arange(half, dtype=F32) / half)
    ang = pos[:, None] * inv[None, :]
    c, s = jnp.cos(ang), jnp.sin(ang)
    cq = jnp.concatenate([c, c], axis=-1)
    sq = jnp.concatenate([-s, s], axis=-1)
    ks = RET_HEAD_DIM ** -0.5
    return (jnp.stack([cq, cq * ks]), jnp.stack([sq, sq * ks]))


def _pick_tm(n, cap):
    tm = cap
    while n % tm:
        tm //= 2
    return tm


def kernel(x_prompt, x_sample, meta_tokens, ln_in_g, ln_in_b, w_in, na_rpb, ret_decay_f, ret_decay_b,
           ret_gn_g, w_out, ln1_g, ln1_b, w_ffn_gate, w_ffn_up, w_ffn_down, ln2_g, ln2_b):
    row = lambda v: v.reshape(1, -1).astype(F32)
    gin, bin_ = row(ln_in_g), row(ln_in_b)
    w_in_bf = w_in[0].astype(BF16)
    w_out_bf = w_out[0].astype(BF16)
    wg_bf, wu_bf, wd_bf = w_ffn_gate[0].astype(BF16), w_ffn_up[0].astype(BF16), w_ffn_down[0].astype(BF16)
    bias = _na_bias(na_rpb[0])
    lg = lambda d: jnp.broadcast_to(jax.nn.log_sigmoid(d[0].astype(F32))[:, None, None], (RET_HEADS, 1, LANES))
    lgf, lgb = lg(ret_decay_f), lg(ret_decay_b)
    gn_g = row(ret_gn_g[0])

    meta_tabs = _rope_tables(jnp.arange(N_META, dtype=F32))
    proj_meta = _inproj(meta_tokens.astype(F32), gin, bin_, w_in_bf, meta_tabs, N_META, N_META)
    proj_meta_pad = jnp.pad(proj_meta, ((CHUNK - N_META, 0), (0, 0)))

    def group(x):
        B, n, _ = x.shape
        x2d = x.reshape(B * n, D_MODEL)
        tabs = _rope_tables(jnp.arange(n, dtype=F32) + float(N_META))
        proj = _inproj(x2d, gin, bin_, w_in_bf, tabs, n, _pick_tm(n, 1024))
        o_na = _na(proj, _na_meta(proj_meta), bias, B, n)
        o_ret = _retention(proj, proj_meta_pad, lgf, lgb, gn_g, B, n)
        h1 = _outproj(o_na, o_ret, x2d, w_out_bf, gin, bin_, row(ln1_g[0]), row(ln1_b[0]), tm=_pick_tm(B * n, 512))
        y = _ffn(h1, wg_bf, wu_bf, wd_bf, row(ln2_g[0]), row(ln2_b[0]), tm=_pick_tm(B * n, 1024))
        return y.reshape(B, n, D_MODEL)

    return (group(x_prompt), group(x_sample))


def _na_meta(proj_meta):
    return jnp.pad(proj_meta, ((0, NA_META_SLOTS - N_META), (0, 0)))
```

```python
import functools

import jax
import jax.numpy as jnp
import numpy as np
from jax import lax
from jax.experimental import pallas as pl
from jax.experimental.pallas import tpu as pltpu

D_MODEL = 2048
N_META = 16
GRID_W = 64
NA_HEADS = 16
NA_HEAD_DIM = 64
NA_WIDTH = NA_HEADS * NA_HEAD_DIM
NA_WIN_H = 8
NA_WIN_W = 16
NA_HALF = GRID_W // 2
NA_KEY_COLS = 48
NA_KEY_SHIFT = GRID_W - NA_KEY_COLS
NA_UNION = NA_WIN_H + 1
NA_KEYS = 512
NA_META_SLOTS = NA_KEYS - NA_UNION * NA_KEY_COLS
RET_HEADS = 8
RET_HEAD_DIM = 128
RET_WIDTH = RET_HEADS * RET_HEAD_DIM
CHUNK = 128
ROPE_BASE = 10000.0
IN_WIDTH = 3 * NA_WIDTH + 4 * RET_WIDTH
D_FF = 5632
LN_EPS = 1e-5
ALPHA = 2.0 ** 0.25

COL_QA, COL_KA, COL_VA = 0, NA_WIDTH, 2 * NA_WIDTH
COL_QR = 3 * NA_WIDTH
COL_KR = COL_QR + RET_WIDTH
COL_VR = COL_KR + RET_WIDTH
COL_GR = COL_VR + RET_WIDTH

LANES = 128
NEG = -1e30
VMEM_LIMIT = 56 * 1024 * 1024

F32 = jnp.float32
BF16 = jnp.bfloat16


def _layer_norm_rows(x, g, b):
    mu = jnp.mean(x, axis=-1, keepdims=True)
    xc = x - mu
    var = jnp.mean(xc * xc, axis=-1, keepdims=True)
    return xc * lax.rsqrt(var + LN_EPS) * g + b


def _inproj_kernel(x_ref, g_ref, b_ref, w_ref, c_ref, s_ref, o_ref, hb_ref, *, tn, rb):
    j = pl.program_id(1)
    tm = x_ref.shape[0]

    col0 = j * tn
    rotary = jnp.logical_and(col0 >= COL_QR, col0 < COL_VR)

    @pl.when(j == 0)
    def _():
        for sb in range(tm // rb):
            rows = slice(sb * rb, (sb + 1) * rb)
            hb_ref[rows, :] = _layer_norm_rows(x_ref[rows, :], g_ref[...], b_ref[...]).astype(BF16)
            o_ref[rows, :] = jnp.dot(hb_ref[rows, :], w_ref[...], preferred_element_type=F32).astype(BF16)

    @pl.when(jnp.logical_and(j != 0, jnp.logical_not(rotary)))
    def _():
        o_ref[...] = jnp.dot(hb_ref[...], w_ref[...], preferred_element_type=F32).astype(BF16)

    @pl.when(rotary)
    def _():
        acc = jnp.dot(hb_ref[...], w_ref[...], preferred_element_type=F32)
        c = c_ref[...]
        s = s_ref[...]
        for t in range(tn // LANES):
            xs = acc[:, t * LANES:(t + 1) * LANES]
            o_ref[:, t * LANES:(t + 1) * LANES] = (xs * c + pltpu.roll(xs, LANES // 2, 1) * s).astype(BF16)


def _inproj(x2d, ln_g, ln_b, w_bf, tabs, n_seq, tm, tn=1024):
    T = x2d.shape[0]
    assert T % tm == 0 and n_seq % tm == 0 and IN_WIDTH % tn == 0 and RET_WIDTH % tn == 0
    blocks_per_seq = n_seq // tm

    def tab_index(i, j):
        col0 = j * tn
        is_k = jnp.logical_and(col0 >= COL_KR, col0 < COL_VR)
        return (jnp.where(is_k, 1, 0), i % blocks_per_seq, 0)

    tab_spec = pl.BlockSpec((None, tm, LANES), tab_index)
    rb = min(tm, 128)
    return pl.pallas_call(
        functools.partial(_inproj_kernel, tn=tn, rb=rb),
        grid=(T // tm, IN_WIDTH // tn),
        in_specs=[
            pl.BlockSpec((tm, D_MODEL), lambda i, j: (i, 0)),
            pl.BlockSpec((1, D_MODEL), lambda i, j: (0, 0)),
            pl.BlockSpec((1, D_MODEL), lambda i, j: (0, 0)),
            pl.BlockSpec((D_MODEL, tn), lambda i, j: (0, j)),
            tab_spec, tab_spec,
        ],
        out_specs=pl.BlockSpec((tm, tn), lambda i, j: (i, j)),
        out_shape=jax.ShapeDtypeStruct((T, IN_WIDTH), BF16),
        scratch_shapes=[pltpu.VMEM((tm, D_MODEL), BF16)],
        compiler_params=pltpu.CompilerParams(
            dimension_semantics=("arbitrary", "arbitrary"), vmem_limit_bytes=VMEM_LIMIT),
        name="inproj",
    )(x2d, ln_g, ln_b, w_bf, *tabs)


def _na_kernel(q_ref, k_ref, v_ref, km_ref, vm_ref, bias_ref, o_ref, *, rows, unroll):
    lane = lax.broadcasted_iota(jnp.int32, (1, LANES), 1)
    scale = NA_HEAD_DIM ** -0.5
    qmask = [jnp.where((lane // NA_HEAD_DIM) == hh, scale, 0.0).astype(BF16) for hh in range(2)]
    km = km_ref[...]
    vm = vm_ref[...]
    nt = (((1,), (1,)), ((), ()))

    def window(ref, meta, b0, hf):
        parts = [ref[pl.ds(pl.multiple_of((b0 + i) * GRID_W + NA_KEY_SHIFT * hf, NA_KEY_SHIFT), NA_KEY_COLS), :]
                 for i in range(NA_UNION)]
        return jnp.concatenate(parts + [meta], axis=0)

    units = [(u, hf) for u in range(unroll // 2) for hf in range(2)]

    def locate(g, idx):
        u, hf = units[idx]
        r0 = g * unroll + 2 * u
        b0 = jnp.clip(r0 - NA_WIN_H // 2, 0, rows - NA_UNION)
        qsl = [pl.ds(pl.multiple_of((r0 + t) * GRID_W + NA_HALF * hf, NA_HALF), NA_HALF) for t in range(2)]
        return r0 - b0, qsl, b0, hf

    def scores(g, idx):
        e0, qsl, b0, hf = locate(g, idx)
        q2 = jnp.concatenate([q_ref[sl, :] * qmask[hh] for sl in qsl for hh in range(2)], axis=0)
        s = lax.dot_general(q2, window(k_ref, km, b0, hf), nt, preferred_element_type=F32)
        bias = jnp.concatenate([bias_ref[0, hf, pl.ds(e0 + t, 1)][0] for t in range(2)], axis=0)
        return s + bias

    def finish(s, g, idx):
        _, qsl, b0, hf = locate(g, idx)
        m = jnp.max(s, axis=-1, keepdims=True)
        p = jnp.exp(s - m)
        den = jnp.sum(p, axis=-1, keepdims=True)
        o2 = jnp.dot(p.astype(BF16), window(v_ref, vm, b0, hf), preferred_element_type=F32) / den
        for t in range(2):
            blk = o2[2 * NA_HALF * t:2 * NA_HALF * (t + 1)]
            o_ref[qsl[t], :] = jnp.where(lane < NA_HEAD_DIM, blk[:NA_HALF], blk[NA_HALF:]).astype(BF16)

    def body(g, c):
        staged = [scores(g, idx) for idx in range(len(units))]
        for idx in range(len(units)):
            finish(staged[idx], g, idx)
        return c

    lax.fori_loop(0, rows // unroll, body, 0)


def _na_bias_kernel(rpb_ref, o_ref):
    c32 = lax.broadcasted_iota(jnp.int32, (NA_HALF, LANES), 0)
    lane = lax.broadcasted_iota(jnp.int32, (NA_HALF, LANES), 1)
    masked = jnp.where(lane + c32 < 0, 0.0, NEG).astype(F32)[:, :NA_KEY_COLS]
    meta = jnp.where(lane + LANES * c32 < N_META + LANES * c32, 0.0, NEG).astype(F32)[:, :NA_META_SLOTS]
    tiles = {}
    for hf in range(2):
        c = c32 + NA_HALF * hf
        kc = lane + NA_KEY_SHIFT * hf
        cs = jnp.clip(c - NA_WIN_W // 2, 0, GRID_W - NA_WIN_W)
        valid = (kc >= cs) & (kc < cs + NA_WIN_W) & (lane < NA_KEY_COLS)
        shift = (LANES - (NA_WIN_W - 1) - NA_KEY_SHIFT * hf + NA_HALF * hf) % LANES
        for h in range(2):
            for dr in range(2 * NA_WIN_H - 1):
                w = jnp.broadcast_to(rpb_ref[h, dr:dr + 1, :], (NA_HALF, LANES))
                tiles[(hf, h, dr)] = jnp.where(valid, pltpu.roll(w, shift, 1, stride=1, stride_axis=0), NEG)
    for hf in range(2):
        for e in range(NA_UNION):
            d = 1 if e > NA_WIN_H // 2 else 0
            heads = []
            for h in range(2):
                parts = [tiles[(hf, h, i - e + NA_WIN_H - 1)][:, :NA_KEY_COLS] if 0 <= i - d < NA_WIN_H else masked
                         for i in range(NA_UNION)]
                heads.append(jnp.concatenate(parts + [meta], axis=1))
            o_ref[0, hf, e] = jnp.concatenate(heads, axis=0)


def _na_bias(rpb):
    n_dr = 2 * NA_WIN_H - 1
    rpb_lanes = jnp.pad(rpb.astype(F32), ((0, 0), (0, 0), (0, LANES - rpb.shape[-1])))
    shape = (NA_HEADS // 2, 2, NA_UNION, 2 * NA_HALF, NA_KEYS)
    return pl.pallas_call(
        _na_bias_kernel,
        grid=(NA_HEADS // 2,),
        in_specs=[pl.BlockSpec((2, n_dr, LANES), lambda p: (p, 0, 0))],
        out_specs=pl.BlockSpec((1,) + shape[1:], lambda p: (p, 0, 0, 0, 0)),
        out_shape=jax.ShapeDtypeStruct(shape, F32),
        compiler_params=pltpu.CompilerParams(dimension_semantics=("arbitrary",), vmem_limit_bytes=VMEM_LIMIT),
        name="na_bias",
    )(rpb_lanes)


def _na(proj, proj_meta_keys, bias, B, n):
    rows, unroll = n // GRID_W, 16
    assert rows >= NA_UNION and rows % unroll == 0
    pairs = NA_WIDTH // LANES
    seq_spec = lambda c0: pl.BlockSpec((n, LANES), lambda hp, b: (b, c0 // LANES + hp))
    meta_spec = lambda c0: pl.BlockSpec((NA_META_SLOTS, LANES), lambda hp, b: (0, c0 // LANES + hp))
    return pl.pallas_call(
        functools.partial(_na_kernel, rows=rows, unroll=unroll),
        grid=(pairs, B),
        in_specs=[
            seq_spec(COL_QA), seq_spec(COL_KA), seq_spec(COL_VA),
            meta_spec(COL_KA), meta_spec(COL_VA),
            pl.BlockSpec((1, 2, NA_UNION, 2 * NA_HALF, NA_KEYS), lambda hp, b: (hp, 0, 0, 0, 0)),
        ],
        out_specs=pl.BlockSpec((n, LANES), lambda hp, b: (b, hp)),
        out_shape=jax.ShapeDtypeStruct((B * n, NA_WIDTH), BF16),
        compiler_params=pltpu.CompilerParams(
            dimension_semantics=("arbitrary", "arbitrary"), vmem_limit_bytes=VMEM_LIMIT),
        name="na_attn",
    )(proj, proj, proj, proj_meta_keys, proj_meta_keys, bias)


def _ret_kernel(q_ref, k_ref, v_ref, g_ref, km_ref, vm_ref, lgf_ref, lgb_ref, gn_ref, o_ref,
                ds_ref, sf_ref, sb_ref, *, n_chunks, group):
    lgf = lgf_ref[0]
    lgb = lgb_ref[0]
    ii = lax.broadcasted_iota(jnp.int32, (CHUNK, CHUNK), 0).astype(F32)
    jj = lax.broadcasted_iota(jnp.int32, (CHUNK, CHUNK), 1).astype(F32)
    diff = ii - jj
    dmat = jnp.where(diff >= 0, jnp.exp(jnp.maximum(diff, 0.0) * lgf), jnp.exp(jnp.maximum(-diff, 0.0) * lgb))
    xi_f = jnp.exp((ii + 1.0) * lgf)
    zeta_f = jnp.exp((CHUNK - 1.0 - ii) * lgf)
    cd_f = jnp.exp(CHUNK * lgf)
    xi_b = jnp.exp((CHUNK - ii) * lgb)
    zeta_b = jnp.exp(ii * lgb)
    cd_b = jnp.exp(CHUNK * lgb)
    nt = (((1,), (1,)), ((), ()))
    tn_dims = (((0,), (0,)), ((), ()))
    gn = gn_ref[...]

    def chunk_slice(ch):
        return pl.ds(pl.multiple_of(ch * CHUNK, CHUNK), CHUNK)

    def increments(k, v):
        kf = k.astype(F32)
        kz = jnp.concatenate([(kf * zeta_f).astype(BF16), (kf * zeta_b).astype(BF16)], axis=1)
        return lax.dot_general(kz, v, tn_dims, preferred_element_type=F32)

    def stage1(gi, c):
        for u in range(group):
            ch = gi * group + u
            sl = chunk_slice(ch)
            ds_ref[ch] = increments(k_ref[sl, :], v_ref[sl, :])
        return c

    lax.fori_loop(0, n_chunks // group, stage1, 0)

    def scan_f(ch, S):
        sf_ref[ch] = S.astype(BF16)
        return S * cd_f + ds_ref[ch, :CHUNK, :]

    def scan_b(t, S):
        ch = n_chunks - 1 - t
        sb_ref[ch] = S.astype(BF16)
        return S * cd_b + ds_ref[ch, CHUNK:, :]

    S0 = increments(km_ref[...], vm_ref[...])[:CHUNK]
    lax.fori_loop(0, n_chunks, scan_f, S0, unroll=4)
    lax.fori_loop(0, n_chunks, scan_b, jnp.zeros((CHUNK, CHUNK), F32), unroll=4)

    def scores(ch):
        sl = chunk_slice(ch)
        q = q_ref[sl, :]
        s = lax.dot_general(q, k_ref[sl, :], nt, preferred_element_type=F32) * dmat
        qf = q.astype(F32)
        return jnp.concatenate([s.astype(BF16), (qf * xi_f).astype(BF16), (qf * xi_b).astype(BF16)], axis=1)

    def mix(ch, lhs):
        rhs = jnp.concatenate([v_ref[chunk_slice(ch), :], sf_ref[ch], sb_ref[ch]], axis=0)
        return jnp.dot(lhs, rhs, preferred_element_type=F32)

    def finish(ch, o):
        sl = chunk_slice(ch)
        mu = jnp.mean(o, axis=-1, keepdims=True)
        oc = o - mu
        var = jnp.mean(oc * oc, axis=-1, keepdims=True)
        on = oc * lax.rsqrt(var + LN_EPS) * gn
        g = g_ref[sl, :].astype(F32)
        o_ref[sl, :] = (g / (1.0 + jnp.exp(-g)) * on).astype(BF16)

    def stage3(gi, c):
        chs = [gi * group + u for u in range(group)]
        lhs = [scores(ch) for ch in chs]
        outs = [mix(ch, l) for ch, l in zip(chs, lhs)]
        for ch, o in zip(chs, outs):
            finish(ch, o)
        return c

    lax.fori_loop(0, n_chunks // group, stage3, 0)


def _retention(proj, proj_meta_pad, lgf, lgb, gn_g, B, n):
    n_chunks, group = n // CHUNK, 16
    assert n % CHUNK == 0 and n_chunks % group == 0
    seq_spec = lambda c0: pl.BlockSpec((n, LANES), lambda b, h: (b, c0 // LANES + h))
    meta_spec = lambda c0: pl.BlockSpec((CHUNK, LANES), lambda b, h: (0, c0 // LANES + h))
    lg_spec = pl.BlockSpec((1, 1, LANES), lambda b, h: (h, 0, 0))
    return pl.pallas_call(
        functools.partial(_ret_kernel, n_chunks=n_chunks, group=group),
        grid=(B, RET_HEADS),
        in_specs=[
            seq_spec(COL_QR), seq_spec(COL_KR), seq_spec(COL_VR), seq_spec(COL_GR),
            meta_spec(COL_KR), meta_spec(COL_VR),
            lg_spec, lg_spec,
            pl.BlockSpec((1, LANES), lambda b, h: (0, h)),
        ],
        out_specs=pl.BlockSpec((n, LANES), lambda b, h: (b, h)),
        out_shape=jax.ShapeDtypeStruct((B * n, RET_WIDTH), BF16),
        scratch_shapes=[pltpu.VMEM((n_chunks, 2 * CHUNK, LANES), F32),
                        pltpu.VMEM((n_chunks, CHUNK, LANES), BF16),
                        pltpu.VMEM((n_chunks, CHUNK, LANES), BF16)],
        compiler_params=pltpu.CompilerParams(
            dimension_semantics=("arbitrary", "arbitrary"), vmem_limit_bytes=VMEM_LIMIT),
        name="retention",
    )(proj, proj, proj, proj, proj_meta_pad, proj_meta_pad, lgf, lgb, gn_g)


def _outproj_kernel(ona_ref, oret_ref, x_ref, w_ref, gin_ref, bin_ref, g1_ref, b1_ref, o_ref, *, rb):
    tm = x_ref.shape[0]

    def norm_store(rows, mix):
        h = _layer_norm_rows(x_ref[rows, :], gin_ref[...], bin_ref[...])
        o_ref[rows, :] = _layer_norm_rows(ALPHA * h + mix, g1_ref[...], b1_ref[...])

    pending = None
    for sb in range(tm // rb):
        rows = slice(sb * rb, (sb + 1) * rb)
        lhs = jnp.concatenate([ona_ref[rows, :], oret_ref[rows, :]], axis=1)
        mix = jnp.dot(lhs, w_ref[...], preferred_element_type=F32)
        if pending is not None:
            norm_store(*pending)
        pending = (rows, mix)
    norm_store(*pending)


def _outproj(o_na, o_ret, x2d, w_bf, gin, bin_, g1, b1, tm=512):
    T = x2d.shape[0]
    assert T % tm == 0
    row_vec = pl.BlockSpec((1, D_MODEL), lambda i: (0, 0))
    return pl.pallas_call(
        functools.partial(_outproj_kernel, rb=256),
        grid=(T // tm,),
        in_specs=[
            pl.BlockSpec((tm, NA_WIDTH), lambda i: (i, 0)),
            pl.BlockSpec((tm, RET_WIDTH), lambda i: (i, 0)),
            pl.BlockSpec((tm, D_MODEL), lambda i: (i, 0)),
            pl.BlockSpec((D_MODEL, D_MODEL), lambda i: (0, 0), pipeline_mode=pl.Buffered(1)),
            row_vec, row_vec, row_vec, row_vec,
        ],
        out_specs=pl.BlockSpec((tm, D_MODEL), lambda i: (i, 0)),
        out_shape=jax.ShapeDtypeStruct((T, D_MODEL), F32),
        compiler_params=pltpu.CompilerParams(dimension_semantics=("arbitrary",), vmem_limit_bytes=VMEM_LIMIT),
        name="outproj_ln1",
    )(o_na, o_ret, x2d, w_bf, gin, bin_, g1, b1)


def _ffn_kernel(h_ref, wg_ref, wu_ref, wd_ref, g2_ref, b2_ref, o_ref, hb_ref, *, rb):
    j = pl.program_id(1)
    tm = h_ref.shape[0]

    @pl.when(j == 0)
    def _():
        h = h_ref[...]
        hb_ref[...] = h.astype(BF16)
        o_ref[...] = ALPHA * h

    hb = hb_ref[...]
    g = jnp.dot(hb, wg_ref[...], preferred_element_type=F32)
    u = jnp.dot(hb, wu_ref[...], preferred_element_type=F32)
    a = (g / (1.0 + jnp.exp(-g)) * u).astype(BF16)
    o_ref[...] += jnp.dot(a, wd_ref[...], preferred_element_type=F32)

    @pl.when(j == pl.num_programs(1) - 1)
    def _():
        def rows(i, c):
            sl = pl.ds(pl.multiple_of(i * rb, rb), rb)
            o_ref[sl, :] = _layer_norm_rows(o_ref[sl, :], g2_ref[...], b2_ref[...])
            return c

        lax.fori_loop(0, tm // rb, rows, 0)


def _ffn(h1, wg_bf, wu_bf, wd_bf, g2, b2, tm=512, tf=512):
    T = h1.shape[0]
    assert T % tm == 0 and D_FF % tf == 0
    row_vec = pl.BlockSpec((1, D_MODEL), lambda i, j: (0, 0))
    return pl.pallas_call(
        functools.partial(_ffn_kernel, rb=128),
        grid=(T // tm, D_FF // tf),
        in_specs=[
            pl.BlockSpec((tm, D_MODEL), lambda i, j: (i, 0)),
            pl.BlockSpec((D_MODEL, tf), lambda i, j: (0, j)),
            pl.BlockSpec((D_MODEL, tf), lambda i, j: (0, j)),
            pl.BlockSpec((tf, D_MODEL), lambda i, j: (j, 0)),
            row_vec, row_vec,
        ],
        out_specs=pl.BlockSpec((tm, D_MODEL), lambda i, j: (i, 0)),
        out_shape=jax.ShapeDtypeStruct((T, D_MODEL), F32),
        scratch_shapes=[pltpu.VMEM((tm, D_MODEL), BF16)],
        compiler_params=pltpu.CompilerParams(
            dimension_semantics=("arbitrary", "arbitrary"), vmem_limit_bytes=VMEM_LIMIT),
        name="ffn_ln2",
    )(h1, wg_bf, wu_bf, wd_bf, g2, b2)


def _rope_tables(pos):
    half = RET_HEAD_DIM // 2
    inv = ROPE_BASE ** (-jnp.arange(half, dtype=F32) / half)
    ang = pos[:, None] * inv[None, :]
    c, s = jnp.cos(ang), jnp.sin(ang)
    cq = jnp.concatenate([c, c], axis=-1)
    sq = jnp.concatenate([-s, s], axis=-1)
    ks = RET_HEAD_DIM ** -0.5
    return (jnp.stack([cq, cq * ks]), jnp.stack([sq, sq * ks]))


def _pick_tm(n, cap):
    tm = cap
    while n % tm:
        tm //= 2
    return tm


def kernel(x_prompt, x_sample, meta_tokens, ln_in_g, ln_in_b, w_in, na_rpb, ret_decay_f, ret_decay_b,
           ret_gn_g, w_out, ln1_g, ln1_b, w_ffn_gate, w_ffn_up, w_ffn_down, ln2_g, ln2_b):
    row = lambda v: v.reshape(1, -1).astype(F32)
    gin, bin_ = row(ln_in_g), row(ln_in_b)
    w_in_bf = w_in[0].astype(BF16)
    w_out_bf = w_out[0].astype(BF16)
    wg_bf, wu_bf, wd_bf = w_ffn_gate[0].astype(BF16), w_ffn_up[0].astype(BF16), w_ffn_down[0].astype(BF16)
    bias = _na_bias(na_rpb[0])
    lg = lambda d: jnp.broadcast_to(jax.nn.log_sigmoid(d[0].astype(F32))[:, None, None], (RET_HEADS, 1, LANES))
    lgf, lgb = lg(ret_decay_f), lg(ret_decay_b)
    gn_g = row(ret_gn_g[0])

    meta_tabs = _rope_tables(jnp.arange(N_META, dtype=F32))
    proj_meta = _inproj(meta_tokens.astype(F32), gin, bin_, w_in_bf, meta_tabs, N_META, N_META)
    proj_meta_pad = jnp.pad(proj_meta, ((CHUNK - N_META, 0), (0, 0)))

    def group(x):
        B, n, _ = x.shape
        x2d = x.reshape(B * n, D_MODEL)
        tabs = _rope_tables(jnp.arange(n, dtype=F32) + float(N_META))
        proj = _inproj(x2d, gin, bin_, w_in_bf, tabs, n, _pick_tm(n, 1024))
        o_na = _na(proj, _na_meta(proj_meta), bias, B, n)
        o_ret = _retention(proj, proj_meta_pad, lgf, lgb, gn_g, B, n)
        h1 = _outproj(o_na, o_ret, x2d, w_out_bf, gin, bin_, row(ln1_g[0]), row(ln1_b[0]), tm=_pick_tm(B * n, 512))
        y = _ffn(h1, wg_bf, wu_bf, wd_bf, row(ln2_g[0]), row(ln2_b[0]), tm=_pick_tm(B * n, 1024))
        return y.reshape(B, n, D_MODEL)

    return (group(x_prompt), group(x_sample))


def _na_meta(proj_meta):
    return jnp.pad(proj_meta, ((0, NA_META_SLOTS - N_META), (0, 0)))
```

```python
import functools
import math

import jax
import jax.numpy as jnp
import numpy as np
from jax import lax
from jax.experimental import pallas as pl
from jax.experimental.pallas import tpu as pltpu

D_MODEL = 2048
N_META = 16
GRID_W = 64
NA_HEADS = 16
NA_HEAD_DIM = 64
NA_WIDTH = NA_HEADS * NA_HEAD_DIM
NA_WIN_H = 8
NA_WIN_W = 16
NA_HALF = GRID_W // 2
NA_KEY_COLS = 48
NA_KEY_SHIFT = GRID_W - NA_KEY_COLS
NA_UNION = NA_WIN_H + 1
NA_KEYS = 512
NA_META_SLOTS = NA_KEYS - NA_UNION * NA_KEY_COLS
RET_HEADS = 8
RET_HEAD_DIM = 128
RET_WIDTH = RET_HEADS * RET_HEAD_DIM
CHUNK = 128
ROPE_BASE = 10000.0
IN_WIDTH = 3 * NA_WIDTH + 4 * RET_WIDTH
D_FF = 5632
LN_EPS = 1e-5
ALPHA = 2.0 ** 0.25

COL_QA, COL_KA, COL_VA = 0, NA_WIDTH, 2 * NA_WIDTH
COL_QR = 3 * NA_WIDTH
COL_KR = COL_QR + RET_WIDTH
COL_VR = COL_KR + RET_WIDTH
COL_GR = COL_VR + RET_WIDTH

NA_GROUP_ROWS = 32
RET_GROUP_CHUNKS = 32

LANES = 128
NEG = -1e30
VMEM_LIMIT = 56 * 1024 * 1024

F32 = jnp.float32
BF16 = jnp.bfloat16


def _layer_norm_rows(x, g, b):
    mu = jnp.mean(x, axis=-1, keepdims=True)
    xc = x - mu
    var = jnp.mean(xc * xc, axis=-1, keepdims=True)
    return xc * lax.rsqrt(var + LN_EPS) * g + b


def _inproj_kernel(x_ref, g_ref, b_ref, w_ref, c_ref, s_ref, o_ref, hb_ref, *, tn, rb):
    j = pl.program_id(1)
    tm = x_ref.shape[0]

    col0 = j * tn
    rotary = jnp.logical_and(col0 >= COL_QR, col0 < COL_VR)

    @pl.when(j == 0)
    def _():
        for sb in range(tm // rb):
            rows = slice(sb * rb, (sb + 1) * rb)
            hb_ref[rows, :] = _layer_norm_rows(x_ref[rows, :], g_ref[...], b_ref[...]).astype(BF16)
            o_ref[rows, :] = jnp.dot(hb_ref[rows, :], w_ref[...], preferred_element_type=F32).astype(BF16)

    @pl.when(jnp.logical_and(j != 0, jnp.logical_not(rotary)))
    def _():
        o_ref[...] = jnp.dot(hb_ref[...], w_ref[...], preferred_element_type=F32).astype(BF16)

    @pl.when(rotary)
    def _():
        acc = jnp.dot(hb_ref[...], w_ref[...], preferred_element_type=F32)
        c = c_ref[...]
        s = s_ref[...]
        for t in range(tn // LANES):
            xs = acc[:, t * LANES:(t + 1) * LANES]
            o_ref[:, t * LANES:(t + 1) * LANES] = (xs * c + pltpu.roll(xs, LANES // 2, 1) * s).astype(BF16)


def _inproj(x2d, ln_g, ln_b, w_bf, tabs, n_seq, tm, tn=1024):
    T = x2d.shape[0]
    assert T % tm == 0 and n_seq % tm == 0 and IN_WIDTH % tn == 0 and RET_WIDTH % tn == 0
    blocks_per_seq = n_seq // tm

    def tab_index(i, j):
        col0 = j * tn
        is_k = jnp.logical_and(col0 >= COL_KR, col0 < COL_VR)
        return (jnp.where(is_k, 1, 0), i % blocks_per_seq, 0)

    tab_spec = pl.BlockSpec((None, tm, LANES), tab_index)
    rb = min(tm, 128)
    return pl.pallas_call(
        functools.partial(_inproj_kernel, tn=tn, rb=rb),
        grid=(T // tm, IN_WIDTH // tn),
        in_specs=[
            pl.BlockSpec((tm, D_MODEL), lambda i, j: (i, 0)),
            pl.BlockSpec((1, D_MODEL), lambda i, j: (0, 0)),
            pl.BlockSpec((1, D_MODEL), lambda i, j: (0, 0)),
            pl.BlockSpec((D_MODEL, tn), lambda i, j: (0, j)),
            tab_spec, tab_spec,
        ],
        out_specs=pl.BlockSpec((tm, tn), lambda i, j: (i, j)),
        out_shape=jax.ShapeDtypeStruct((T, IN_WIDTH), BF16),
        scratch_shapes=[pltpu.VMEM((tm, D_MODEL), BF16)],
        compiler_params=pltpu.CompilerParams(
            dimension_semantics=("arbitrary", "arbitrary"), vmem_limit_bytes=VMEM_LIMIT),
        name="inproj",
    )(x2d, ln_g, ln_b, w_bf, *tabs)


def _na_kernel(q_ref, k_ref, v_ref, km_ref, vm_ref, bias_ref, o_ref, *, rows, unroll):
    lane = lax.broadcasted_iota(jnp.int32, (1, LANES), 1)
    scale = NA_HEAD_DIM ** -0.5
    qmask = [jnp.where((lane // NA_HEAD_DIM) == hh, scale, 0.0).astype(BF16) for hh in range(2)]
    km = km_ref[...]
    vm = vm_ref[...]
    nt = (((1,), (1,)), ((), ()))

    def window(ref, meta, b0, hf):
        parts = [ref[pl.ds(pl.multiple_of((b0 + i) * GRID_W + NA_KEY_SHIFT * hf, NA_KEY_SHIFT), NA_KEY_COLS), :]
                 for i in range(NA_UNION)]
        return jnp.concatenate(parts + [meta], axis=0)

    units = [(u, hf) for u in range(unroll // 2) for hf in range(2)]

    def locate(g, idx):
        u, hf = units[idx]
        r0 = g * unroll + 2 * u
        b0 = jnp.clip(r0 - NA_WIN_H // 2, 0, rows - NA_UNION)
        qsl = [pl.ds(pl.multiple_of((r0 + t) * GRID_W + NA_HALF * hf, NA_HALF), NA_HALF) for t in range(2)]
        return r0 - b0, qsl, b0, hf

    def scores(g, idx):
        e0, qsl, b0, hf = locate(g, idx)
        q2 = jnp.concatenate([q_ref[sl, :] * qmask[hh] for sl in qsl for hh in range(2)], axis=0)
        s = lax.dot_general(q2, window(k_ref, km, b0, hf), nt, preferred_element_type=F32)
        bias = jnp.concatenate([bias_ref[0, hf, pl.ds(e0 + t, 1)][0] for t in range(2)], axis=0)
        return s + bias

    def finish(s, g, idx):
        _, qsl, b0, hf = locate(g, idx)
        m = jnp.max(s, axis=-1, keepdims=True)
        p = jnp.exp(s - m)
        den = jnp.sum(p, axis=-1, keepdims=True)
        o2 = jnp.dot(p.astype(BF16), window(v_ref, vm, b0, hf), preferred_element_type=F32) / den
        for t in range(2):
            blk = o2[2 * NA_HALF * t:2 * NA_HALF * (t + 1)]
            o_ref[qsl[t], :] = jnp.where(lane < NA_HEAD_DIM, blk[:NA_HALF], blk[NA_HALF:]).astype(BF16)

    def body(g, c):
        staged = [scores(g, idx) for idx in range(len(units))]
        for idx in range(len(units)):
            finish(staged[idx], g, idx)
        return c

    lax.fori_loop(0, rows // unroll, body, 0)


def _na_bias_kernel(rpb_ref, o_ref):
    c32 = lax.broadcasted_iota(jnp.int32, (NA_HALF, LANES), 0)
    lane = lax.broadcasted_iota(jnp.int32, (NA_HALF, LANES), 1)
    masked = jnp.where(lane + c32 < 0, 0.0, NEG).astype(F32)[:, :NA_KEY_COLS]
    meta = jnp.where(lane + LANES * c32 < N_META + LANES * c32, 0.0, NEG).astype(F32)[:, :NA_META_SLOTS]
    tiles = {}
    for hf in range(2):
        c = c32 + NA_HALF * hf
        kc = lane + NA_KEY_SHIFT * hf
        cs = jnp.clip(c - NA_WIN_W // 2, 0, GRID_W - NA_WIN_W)
        valid = (kc >= cs) & (kc < cs + NA_WIN_W) & (lane < NA_KEY_COLS)
        shift = (LANES - (NA_WIN_W - 1) - NA_KEY_SHIFT * hf + NA_HALF * hf) % LANES
        for h in range(2):
            for dr in range(2 * NA_WIN_H - 1):
                w = jnp.broadcast_to(rpb_ref[h, dr:dr + 1, :], (NA_HALF, LANES))
                tiles[(hf, h, dr)] = jnp.where(valid, pltpu.roll(w, shift, 1, stride=1, stride_axis=0), NEG)
    for hf in range(2):
        for e in range(NA_UNION):
            d = 1 if e > NA_WIN_H // 2 else 0
            heads = []
            for h in range(2):
                parts = [tiles[(hf, h, i - e + NA_WIN_H - 1)][:, :NA_KEY_COLS] if 0 <= i - d < NA_WIN_H else masked
                         for i in range(NA_UNION)]
                heads.append(jnp.concatenate(parts + [meta], axis=1))
            o_ref[0, hf, e] = jnp.concatenate(heads, axis=0)


def _na_bias(rpb):
    n_dr = 2 * NA_WIN_H - 1
    rpb_lanes = jnp.pad(rpb.astype(F32), ((0, 0), (0, 0), (0, LANES - rpb.shape[-1])))
    shape = (NA_HEADS // 2, 2, NA_UNION, 2 * NA_HALF, NA_KEYS)
    return pl.pallas_call(
        _na_bias_kernel,
        grid=(NA_HEADS // 2,),
        in_specs=[pl.BlockSpec((2, n_dr, LANES), lambda p: (p, 0, 0))],
        out_specs=pl.BlockSpec((1,) + shape[1:], lambda p: (p, 0, 0, 0, 0)),
        out_shape=jax.ShapeDtypeStruct(shape, F32),
        compiler_params=pltpu.CompilerParams(dimension_semantics=("arbitrary",), vmem_limit_bytes=VMEM_LIMIT),
        name="na_bias",
    )(rpb_lanes)


def _na(proj, proj_meta_keys, bias, B, n):
    rows = n // GRID_W
    unroll = math.gcd(rows, NA_GROUP_ROWS)
    assert rows >= NA_UNION and unroll % 2 == 0
    pairs = NA_WIDTH // LANES
    seq_spec = lambda c0: pl.BlockSpec((n, LANES), lambda hp, b: (b, c0 // LANES + hp))
    meta_spec = lambda c0: pl.BlockSpec((NA_META_SLOTS, LANES), lambda hp, b: (0, c0 // LANES + hp))
    return pl.pallas_call(
        functools.partial(_na_kernel, rows=rows, unroll=unroll),
        grid=(pairs, B),
        in_specs=[
            seq_spec(COL_QA), seq_spec(COL_KA), seq_spec(COL_VA),
            meta_spec(COL_KA), meta_spec(COL_VA),
            pl.BlockSpec((1, 2, NA_UNION, 2 * NA_HALF, NA_KEYS), lambda hp, b: (hp, 0, 0, 0, 0)),
        ],
        out_specs=pl.BlockSpec((n, LANES), lambda hp, b: (b, hp)),
        out_shape=jax.ShapeDtypeStruct((B * n, NA_WIDTH), BF16),
        compiler_params=pltpu.CompilerParams(
            dimension_semantics=("arbitrary", "arbitrary"), vmem_limit_bytes=VMEM_LIMIT),
        name="na_attn",
    )(proj, proj, proj, proj_meta_keys, proj_meta_keys, bias)


def _ret_kernel(q_ref, k_ref, v_ref, g_ref, km_ref, vm_ref, lgf_ref, lgb_ref, gn_ref, o_ref,
                ds_ref, sf_ref, sb_ref, *, n_chunks, group):
    lgf = lgf_ref[0]
    lgb = lgb_ref[0]
    ii = lax.broadcasted_iota(jnp.int32, (CHUNK, CHUNK), 0).astype(F32)
    jj = lax.broadcasted_iota(jnp.int32, (CHUNK, CHUNK), 1).astype(F32)
    diff = ii - jj
    dmat = jnp.where(diff >= 0, jnp.exp(jnp.maximum(diff, 0.0) * lgf), jnp.exp(jnp.maximum(-diff, 0.0) * lgb))
    xi_f = jnp.exp((ii + 1.0) * lgf)
    zeta_f = jnp.exp((CHUNK - 1.0 - ii) * lgf)
    cd_f = jnp.exp(CHUNK * lgf)
    xi_b = jnp.exp((CHUNK - ii) * lgb)
    zeta_b = jnp.exp(ii * lgb)
    cd_b = jnp.exp(CHUNK * lgb)
    nt = (((1,), (1,)), ((), ()))
    tn_dims = (((0,), (0,)), ((), ()))
    gn = gn_ref[...]

    def chunk_slice(ch):
        return pl.ds(pl.multiple_of(ch * CHUNK, CHUNK), CHUNK)

    def increments(k, v):
        kf = k.astype(F32)
        kz = jnp.concatenate([(kf * zeta_f).astype(BF16), (kf * zeta_b).astype(BF16)], axis=1)
        return lax.dot_general(kz, v, tn_dims, preferred_element_type=F32)

    def stage1(gi, c):
        for u in range(group):
            ch = gi * group + u
            sl = chunk_slice(ch)
            ds_ref[ch] = increments(k_ref[sl, :], v_ref[sl, :])
        return c

    lax.fori_loop(0, n_chunks // group, stage1, 0)

    def scan_f(ch, S):
        sf_ref[ch] = S.astype(BF16)
        return S * cd_f + ds_ref[ch, :CHUNK, :]

    def scan_b(t, S):
        ch = n_chunks - 1 - t
        sb_ref[ch] = S.astype(BF16)
        return S * cd_b + ds_ref[ch, CHUNK:, :]

    S0 = increments(km_ref[...], vm_ref[...])[:CHUNK]
    lax.fori_loop(0, n_chunks, scan_f, S0, unroll=4)
    lax.fori_loop(0, n_chunks, scan_b, jnp.zeros((CHUNK, CHUNK), F32), unroll=4)

    def scores(ch):
        sl = chunk_slice(ch)
        q = q_ref[sl, :]
        s = lax.dot_general(q, k_ref[sl, :], nt, preferred_element_type=F32) * dmat
        qf = q.astype(F32)
        return jnp.concatenate([s.astype(BF16), (qf * xi_f).astype(BF16), (qf * xi_b).astype(BF16)], axis=1)

    def mix(ch, lhs):
        rhs = jnp.concatenate([v_ref[chunk_slice(ch), :], sf_ref[ch], sb_ref[ch]], axis=0)
        return jnp.dot(lhs, rhs, preferred_element_type=F32)

    def finish(ch, o):
        sl = chunk_slice(ch)
        mu = jnp.mean(o, axis=-1, keepdims=True)
        oc = o - mu
        var = jnp.mean(oc * oc, axis=-1, keepdims=True)
        on = oc * lax.rsqrt(var + LN_EPS) * gn
        g = g_ref[sl, :].astype(F32)
        o_ref[sl, :] = (g / (1.0 + jnp.exp(-g)) * on).astype(BF16)

    def stage3(gi, c):
        chs = [gi * group + u for u in range(group)]
        lhs = [scores(ch) for ch in chs]
        outs = [mix(ch, l) for ch, l in zip(chs, lhs)]
        for ch, o in zip(chs, outs):
            finish(ch, o)
        return c

    lax.fori_loop(0, n_chunks // group, stage3, 0)


def _retention(proj, proj_meta_pad, lgf, lgb, gn_g, B, n):
    assert n % CHUNK == 0
    n_chunks = n // CHUNK
    group = math.gcd(n_chunks, RET_GROUP_CHUNKS)
    seq_spec = lambda c0: pl.BlockSpec((n, LANES), lambda b, h: (b, c0 // LANES + h))
    meta_spec = lambda c0: pl.BlockSpec((CHUNK, LANES), lambda b, h: (0, c0 // LANES + h))
    lg_spec = pl.BlockSpec((1, 1, LANES), lambda b, h: (h, 0, 0))
    return pl.pallas_call(
        functools.partial(_ret_kernel, n_chunks=n_chunks, group=group),
        grid=(B, RET_HEADS),
        in_specs=[
            seq_spec(COL_QR), seq_spec(COL_KR), seq_spec(COL_VR), seq_spec(COL_GR),
            meta_spec(COL_KR), meta_spec(COL_VR),
            lg_spec, lg_spec,
            pl.BlockSpec((1, LANES), lambda b, h: (0, h)),
        ],
        out_specs=pl.BlockSpec((n, LANES), lambda b, h: (b, h)),
        out_shape=jax.ShapeDtypeStruct((B * n, RET_WIDTH), BF16),
        scratch_shapes=[pltpu.VMEM((n_chunks, 2 * CHUNK, LANES), F32),
                        pltpu.VMEM((n_chunks, CHUNK, LANES), BF16),
                        pltpu.VMEM((n_chunks, CHUNK, LANES), BF16)],
        compiler_params=pltpu.CompilerParams(
            dimension_semantics=("arbitrary", "arbitrary"), vmem_limit_bytes=VMEM_LIMIT),
        name="retention",
    )(proj, proj, proj, proj, proj_meta_pad, proj_meta_pad, lgf, lgb, gn_g)


def _outproj_kernel(ona_ref, oret_ref, x_ref, w_ref, gin_ref, bin_ref, g1_ref, b1_ref, o_ref, *, rb):
    tm = x_ref.shape[0]

    def norm_store(rows, mix):
        h = _layer_norm_rows(x_ref[rows, :], gin_ref[...], bin_ref[...])
        o_ref[rows, :] = _layer_norm_rows(ALPHA * h + mix, g1_ref[...], b1_ref[...])

    pending = None
    for sb in range(tm // rb):
        rows = slice(sb * rb, (sb + 1) * rb)
        lhs = jnp.concatenate([ona_ref[rows, :], oret_ref[rows, :]], axis=1)
        mix = jnp.dot(lhs, w_ref[...], preferred_element_type=F32)
        if pending is not None:
            norm_store(*pending)
        pending = (rows, mix)
    norm_store(*pending)


def _outproj(o_na, o_ret, x2d, w_bf, gin, bin_, g1, b1, tm=512):
    T = x2d.shape[0]
    assert T % tm == 0
    row_vec = pl.BlockSpec((1, D_MODEL), lambda i: (0, 0))
    return pl.pallas_call(
        functools.partial(_outproj_kernel, rb=256),
        grid=(T // tm,),
        in_specs=[
            pl.BlockSpec((tm, NA_WIDTH), lambda i: (i, 0)),
            pl.BlockSpec((tm, RET_WIDTH), lambda i: (i, 0)),
            pl.BlockSpec((tm, D_MODEL), lambda i: (i, 0)),
            pl.BlockSpec((D_MODEL, D_MODEL), lambda i: (0, 0), pipeline_mode=pl.Buffered(1)),
            row_vec, row_vec, row_vec, row_vec,
        ],
        out_specs=pl.BlockSpec((tm, D_MODEL), lambda i: (i, 0)),
        out_shape=jax.ShapeDtypeStruct((T, D_MODEL), F32),
        compiler_params=pltpu.CompilerParams(dimension_semantics=("arbitrary",), vmem_limit_bytes=VMEM_LIMIT),
        name="outproj_ln1",
    )(o_na, o_ret, x2d, w_bf, gin, bin_, g1, b1)


def _ffn_kernel(h_ref, wg_ref, wu_ref, wd_ref, g2_ref, b2_ref, o_ref, hb_ref, *, rb):
    j = pl.program_id(1)
    tm = h_ref.shape[0]

    @pl.when(j == 0)
    def _():
        hb_ref[...] = h_ref[...].astype(BF16)
        o_ref[...] = jnp.zeros_like(o_ref)

    hb = hb_ref[...]
    g = jnp.dot(hb, wg_ref[...], preferred_element_type=F32)
    u = jnp.dot(hb, wu_ref[...], preferred_element_type=F32)
    a = (g / (1.0 + jnp.exp(-g)) * u).astype(BF16)
    o_ref[...] += jnp.dot(a, wd_ref[...], preferred_element_type=F32)

    @pl.when(j == pl.num_programs(1) - 1)
    def _():
        def rows(i, c):
            sl = pl.ds(pl.multiple_of(i * rb, rb), rb)
            o_ref[sl, :] = _layer_norm_rows(ALPHA * h_ref[sl, :] + o_ref[sl, :], g2_ref[...], b2_ref[...])
            return c

        lax.fori_loop(0, tm // rb, rows, 0)


def _ffn(h1, wg_bf, wu_bf, wd_bf, g2, b2, tm=512, tf=512):
    T = h1.shape[0]
    assert T % tm == 0 and D_FF % tf == 0
    row_vec = pl.BlockSpec((1, D_MODEL), lambda i, j: (0, 0))
    return pl.pallas_call(
        functools.partial(_ffn_kernel, rb=128),
        grid=(T // tm, D_FF // tf),
        in_specs=[
            pl.BlockSpec((tm, D_MODEL), lambda i, j: (i, 0)),
            pl.BlockSpec((D_MODEL, tf), lambda i, j: (0, j)),
            pl.BlockSpec((D_MODEL, tf), lambda i, j: (0, j)),
            pl.BlockSpec((tf, D_MODEL), lambda i, j: (j, 0)),
            row_vec, row_vec,
        ],
        out_specs=pl.BlockSpec((tm, D_MODEL), lambda i, j: (i, 0)),
        out_shape=jax.ShapeDtypeStruct((T, D_MODEL), F32),
        scratch_shapes=[pltpu.VMEM((tm, D_MODEL), BF16)],
        compiler_params=pltpu.CompilerParams(
            dimension_semantics=("arbitrary", "arbitrary"), vmem_limit_bytes=VMEM_LIMIT),
        name="ffn_ln2",
    )(h1, wg_bf, wu_bf, wd_bf, g2, b2)


def _rope_tables(pos):
    half = RET_HEAD_DIM // 2
    inv = ROPE_BASE ** (-jnp.arange(half, dtype=F32) / half)
    ang = pos[:, None] * inv[None, :]
    c, s = jnp.cos(ang), jnp.sin(ang)
    cq = jnp.concatenate([c, c], axis=-1)
    sq = jnp.concatenate([-s, s], axis=-1)
    ks = RET_HEAD_DIM ** -0.5
    return (jnp.stack([cq, cq * ks]), jnp.stack([sq, sq * ks]))


def _pick_tm(n, cap):
    tm = cap
    while n % tm:
        tm //= 2
    return tm


def kernel(x_prompt, x_sample, meta_tokens, ln_in_g, ln_in_b, w_in, na_rpb, ret_decay_f, ret_decay_b,
           ret_gn_g, w_out, ln1_g, ln1_b, w_ffn_gate, w_ffn_up, w_ffn_down, ln2_g, ln2_b):
    row = lambda v: v.reshape(1, -1).astype(F32)
    gin, bin_ = row(ln_in_g), row(ln_in_b)
    w_in_bf = w_in[0].astype(BF16)
    w_out_bf = w_out[0].astype(BF16)
    wg_bf, wu_bf, wd_bf = w_ffn_gate[0].astype(BF16), w_ffn_up[0].astype(BF16), w_ffn_down[0].astype(BF16)
    bias = _na_bias(na_rpb[0])
    lg = lambda d: jnp.broadcast_to(jax.nn.log_sigmoid(d[0].astype(F32))[:, None, None], (RET_HEADS, 1, LANES))
    lgf, lgb = lg(ret_decay_f), lg(ret_decay_b)
    gn_g = row(ret_gn_g[0])

    meta_tabs = _rope_tables(jnp.arange(N_META, dtype=F32))
    proj_meta = _inproj(meta_tokens.astype(F32), gin, bin_, w_in_bf, meta_tabs, N_META, N_META)
    proj_meta_pad = jnp.pad(proj_meta, ((CHUNK - N_META, 0), (0, 0)))

    def group(x):
        B, n, _ = x.shape
        x2d = x.reshape(B * n, D_MODEL)
        tabs = _rope_tables(jnp.arange(n, dtype=F32) + float(N_META))
        proj = _inproj(x2d, gin, bin_, w_in_bf, tabs, n, _pick_tm(n, 1024))
        o_na = _na(proj, _na_meta(proj_meta), bias, B, n)
        o_ret = _retention(proj, proj_meta_pad, lgf, lgb, gn_g, B, n)
        h1 = _outproj(o_na, o_ret, x2d, w_out_bf, gin, bin_, row(ln1_g[0]), row(ln1_b[0]), tm=_pick_tm(B * n, 512))
        y = _ffn(h1, wg_bf, wu_bf, wd_bf, row(ln2_g[0]), row(ln2_b[0]), tm=_pick_tm(B * n, 1024))
        return y.reshape(B, n, D_MODEL)

    return (group(x_prompt), group(x_sample))


def _na_meta(proj_meta):
    return jnp.pad(proj_meta, ((0, NA_META_SLOTS - N_META), (0, 0)))
```

```python
import functools
import math

import jax
import jax.numpy as jnp
from jax import lax
from jax.experimental import pallas as pl
from jax.experimental.pallas import tpu as pltpu

D_MODEL = 2048
N_META = 16
GRID_W = 64
NA_HEADS = 16
NA_HEAD_DIM = 64
NA_WIDTH = NA_HEADS * NA_HEAD_DIM
NA_WIN_H = 8
NA_WIN_W = 16
NA_HALF = GRID_W // 2
NA_KEY_COLS = 48
NA_KEY_SHIFT = GRID_W - NA_KEY_COLS
NA_UNION = NA_WIN_H + 1
NA_KEYS = 512
NA_META_SLOTS = NA_KEYS - NA_UNION * NA_KEY_COLS
RET_HEADS = 8
RET_HEAD_DIM = 128
RET_WIDTH = RET_HEADS * RET_HEAD_DIM
CHUNK = 128
ROPE_BASE = 10000.0
IN_WIDTH = 3 * NA_WIDTH + 4 * RET_WIDTH
D_FF = 5632
LN_EPS = 1e-5
ALPHA = 2.0 ** 0.25

COL_QA, COL_KA, COL_VA = 0, NA_WIDTH, 2 * NA_WIDTH
COL_QR = 3 * NA_WIDTH
COL_KR = COL_QR + RET_WIDTH
COL_VR = COL_KR + RET_WIDTH
COL_GR = COL_VR + RET_WIDTH

NA_GROUP_ROWS = 32
RET_GROUP_CHUNKS = 32

LANES = 128
NEG = -1e30
VMEM_LIMIT = 56 * 1024 * 1024

F32 = jnp.float32
BF16 = jnp.bfloat16


def _layer_norm_rows(x, g, b):
    mu = jnp.mean(x, axis=-1, keepdims=True)
    xc = x - mu
    var = jnp.mean(xc * xc, axis=-1, keepdims=True)
    return xc * lax.rsqrt(var + LN_EPS) * g + b


def _inproj_kernel(x_ref, g_ref, b_ref, w_ref, c_ref, s_ref, o_ref, hb_ref, *, tn, rb):
    j = pl.program_id(1)
    tm = x_ref.shape[0]

    col0 = j * tn
    rotary = jnp.logical_and(col0 >= COL_QR, col0 < COL_VR)

    @pl.when(j == 0)
    def _():
        for sb in range(tm // rb):
            rows = slice(sb * rb, (sb + 1) * rb)
            hb_ref[rows, :] = _layer_norm_rows(x_ref[rows, :], g_ref[...], b_ref[...]).astype(BF16)
            o_ref[rows, :] = jnp.dot(hb_ref[rows, :], w_ref[...], preferred_element_type=F32).astype(BF16)

    @pl.when(jnp.logical_and(j != 0, jnp.logical_not(rotary)))
    def _():
        o_ref[...] = jnp.dot(hb_ref[...], w_ref[...], preferred_element_type=F32).astype(BF16)

    @pl.when(rotary)
    def _():
        acc = jnp.dot(hb_ref[...], w_ref[...], preferred_element_type=F32)
        c = c_ref[...]
        s = s_ref[...]
        for t in range(tn // LANES):
            xs = acc[:, t * LANES:(t + 1) * LANES]
            o_ref[:, t * LANES:(t + 1) * LANES] = (xs * c + pltpu.roll(xs, LANES // 2, 1) * s).astype(BF16)


def _inproj(x2d, ln_g, ln_b, w_bf, tabs, n_seq, tm, tn=1024):
    T = x2d.shape[0]
    assert T % tm == 0 and n_seq % tm == 0 and IN_WIDTH % tn == 0 and RET_WIDTH % tn == 0
    blocks_per_seq = n_seq // tm

    def tab_index(i, j):
        col0 = j * tn
        is_k = jnp.logical_and(col0 >= COL_KR, col0 < COL_VR)
        return (jnp.where(is_k, 1, 0), i % blocks_per_seq, 0)

    tab_spec = pl.BlockSpec((None, tm, LANES), tab_index)
    rb = min(tm, 128)
    return pl.pallas_call(
        functools.partial(_inproj_kernel, tn=tn, rb=rb),
        grid=(T // tm, IN_WIDTH // tn),
        in_specs=[
            pl.BlockSpec((tm, D_MODEL), lambda i, j: (i, 0)),
            pl.BlockSpec((1, D_MODEL), lambda i, j: (0, 0)),
            pl.BlockSpec((1, D_MODEL), lambda i, j: (0, 0)),
            pl.BlockSpec((D_MODEL, tn), lambda i, j: (0, j)),
            tab_spec, tab_spec,
        ],
        out_specs=pl.BlockSpec((tm, tn), lambda i, j: (i, j)),
        out_shape=jax.ShapeDtypeStruct((T, IN_WIDTH), BF16),
        scratch_shapes=[pltpu.VMEM((tm, D_MODEL), BF16)],
        compiler_params=pltpu.CompilerParams(
            dimension_semantics=("arbitrary", "arbitrary"), vmem_limit_bytes=VMEM_LIMIT),
        name="inproj",
    )(x2d, ln_g, ln_b, w_bf, *tabs)


def _na_kernel(q_ref, k_ref, v_ref, km_ref, vm_ref, bias_ref, o_ref, *, rows, unroll):
    lane = lax.broadcasted_iota(jnp.int32, (1, LANES), 1)
    scale = NA_HEAD_DIM ** -0.5
    qmask = [jnp.where((lane // NA_HEAD_DIM) == hh, scale, 0.0).astype(BF16) for hh in range(2)]
    km = km_ref[...]
    vm = vm_ref[...]
    nt = (((1,), (1,)), ((), ()))

    def window(ref, meta, b0, hf):
        parts = [ref[pl.ds(pl.multiple_of((b0 + i) * GRID_W + NA_KEY_SHIFT * hf, NA_KEY_SHIFT), NA_KEY_COLS), :]
                 for i in range(NA_UNION)]
        return jnp.concatenate(parts + [meta], axis=0)

    units = [(u, hf) for u in range(unroll // 2) for hf in range(2)]

    def locate(g, idx):
        u, hf = units[idx]
        r0 = g * unroll + 2 * u
        b0 = jnp.clip(r0 - NA_WIN_H // 2, 0, rows - NA_UNION)
        qsl = [pl.ds(pl.multiple_of((r0 + t) * GRID_W + NA_HALF * hf, NA_HALF), NA_HALF) for t in range(2)]
        return r0 - b0, qsl, b0, hf

    def scores(g, idx):
        e0, qsl, b0, hf = locate(g, idx)
        q2 = jnp.concatenate([q_ref[sl, :] * qmask[hh] for sl in qsl for hh in range(2)], axis=0)
        s = lax.dot_general(q2, window(k_ref, km, b0, hf), nt, preferred_element_type=F32)
        bias = jnp.concatenate([bias_ref[0, hf, pl.ds(e0 + t, 1)][0] for t in range(2)], axis=0)
        return s + bias

    def finish(s, g, idx):
        _, qsl, b0, hf = locate(g, idx)
        m = jnp.max(s, axis=-1, keepdims=True)
        p = jnp.exp(s - m)
        den = jnp.sum(p, axis=-1, keepdims=True)
        o2 = jnp.dot(p.astype(BF16), window(v_ref, vm, b0, hf), preferred_element_type=F32) / den
        for t in range(2):
            blk = o2[2 * NA_HALF * t:2 * NA_HALF * (t + 1)]
            o_ref[qsl[t], :] = jnp.where(lane < NA_HEAD_DIM, blk[:NA_HALF], blk[NA_HALF:]).astype(BF16)

    def body(g, c):
        staged = [scores(g, idx) for idx in range(len(units))]
        for idx in range(len(units)):
            finish(staged[idx], g, idx)
        return c

    lax.fori_loop(0, rows // unroll, body, 0)


def _na_bias_kernel(rpb_ref, o_ref):
    c32 = lax.broadcasted_iota(jnp.int32, (NA_HALF, LANES), 0)
    lane = lax.broadcasted_iota(jnp.int32, (NA_HALF, LANES), 1)
    masked = jnp.where(lane + c32 < 0, 0.0, NEG).astype(F32)[:, :NA_KEY_COLS]
    meta = jnp.where(lane + LANES * c32 < N_META + LANES * c32, 0.0, NEG).astype(F32)[:, :NA_META_SLOTS]
    tiles = {}
    for hf in range(2):
        c = c32 + NA_HALF * hf
        kc = lane + NA_KEY_SHIFT * hf
        cs = jnp.clip(c - NA_WIN_W // 2, 0, GRID_W - NA_WIN_W)
        valid = (kc >= cs) & (kc < cs + NA_WIN_W) & (lane < NA_KEY_COLS)
        shift = (LANES - (NA_WIN_W - 1) - NA_KEY_SHIFT * hf + NA_HALF * hf) % LANES
        for h in range(2):
            for dr in range(2 * NA_WIN_H - 1):
                w = jnp.broadcast_to(rpb_ref[h, dr:dr + 1, :], (NA_HALF, LANES))
                tiles[(hf, h, dr)] = jnp.where(valid, pltpu.roll(w, shift, 1, stride=1, stride_axis=0), NEG)
    for hf in range(2):
        for e in range(NA_UNION):
            d = 1 if e > NA_WIN_H // 2 else 0
            heads = []
            for h in range(2):
                parts = [tiles[(hf, h, i - e + NA_WIN_H - 1)][:, :NA_KEY_COLS] if 0 <= i - d < NA_WIN_H else masked
                         for i in range(NA_UNION)]
                heads.append(jnp.concatenate(parts + [meta], axis=1))
            o_ref[0, hf, e] = jnp.concatenate(heads, axis=0)


def _na_bias(rpb):
    n_dr = 2 * NA_WIN_H - 1
    rpb_lanes = jnp.pad(rpb.astype(F32), ((0, 0), (0, 0), (0, LANES - rpb.shape[-1])))
    shape = (NA_HEADS // 2, 2, NA_UNION, 2 * NA_HALF, NA_KEYS)
    return pl.pallas_call(
        _na_bias_kernel,
        grid=(NA_HEADS // 2,),
        in_specs=[pl.BlockSpec((2, n_dr, LANES), lambda p: (p, 0, 0))],
        out_specs=pl.BlockSpec((1,) + shape[1:], lambda p: (p, 0, 0, 0, 0)),
        out_shape=jax.ShapeDtypeStruct(shape, F32),
        compiler_params=pltpu.CompilerParams(dimension_semantics=("arbitrary",), vmem_limit_bytes=VMEM_LIMIT),
        name="na_bias",
    )(rpb_lanes)


def _na(proj, proj_meta_keys, bias, B, n):
    rows = n // GRID_W
    unroll = math.gcd(rows, NA_GROUP_ROWS)
    assert rows >= NA_UNION and unroll % 2 == 0
    pairs = NA_WIDTH // LANES
    seq_spec = lambda c0: pl.BlockSpec((n, LANES), lambda hp, b: (b, c0 // LANES + hp))
    meta_spec = lambda c0: pl.BlockSpec((NA_META_SLOTS, LANES), lambda hp, b: (0, c0 // LANES + hp))
    return pl.pallas_call(
        functools.partial(_na_kernel, rows=rows, unroll=unroll),
        grid=(pairs, B),
        in_specs=[
            seq_spec(COL_QA), seq_spec(COL_KA), seq_spec(COL_VA),
            meta_spec(COL_KA), meta_spec(COL_VA),
            pl.BlockSpec((1, 2, NA_UNION, 2 * NA_HALF, NA_KEYS), lambda hp, b: (hp, 0, 0, 0, 0)),
        ],
        out_specs=pl.BlockSpec((n, LANES), lambda hp, b: (b, hp)),
        out_shape=jax.ShapeDtypeStruct((B * n, NA_WIDTH), BF16),
        compiler_params=pltpu.CompilerParams(
            dimension_semantics=("arbitrary", "arbitrary"), vmem_limit_bytes=VMEM_LIMIT),
        name="na_attn",
    )(proj, proj, proj, proj_meta_keys, proj_meta_keys, bias)


def _ret_kernel(q_ref, k_ref, v_ref, g_ref, km_ref, vm_ref, lgf_ref, lgb_ref, gn_ref, o_ref,
                ds_ref, sf_ref, sb_ref, *, n_chunks, group):
    lgf = lgf_ref[0]
    lgb = lgb_ref[0]
    ii = lax.broadcasted_iota(jnp.int32, (CHUNK, CHUNK), 0).astype(F32)
    jj = lax.broadcasted_iota(jnp.int32, (CHUNK, CHUNK), 1).astype(F32)
    diff = ii - jj
    dmat = jnp.where(diff >= 0, jnp.exp(jnp.maximum(diff, 0.0) * lgf), jnp.exp(jnp.maximum(-diff, 0.0) * lgb))
    xi_f = jnp.exp((ii + 1.0) * lgf)
    zeta_f = jnp.exp((CHUNK - 1.0 - ii) * lgf)
    cd_f = jnp.exp(CHUNK * lgf)
    xi_b = jnp.exp((CHUNK - ii) * lgb)
    zeta_b = jnp.exp(ii * lgb)
    cd_b = jnp.exp(CHUNK * lgb)
    nt = (((1,), (1,)), ((), ()))
    tn_dims = (((0,), (0,)), ((), ()))
    gn = gn_ref[...]

    def chunk_slice(ch):
        return pl.ds(pl.multiple_of(ch * CHUNK, CHUNK), CHUNK)

    def increments(k, v):
        kf = k.astype(F32)
        kz = jnp.concatenate([(kf * zeta_f).astype(BF16), (kf * zeta_b).astype(BF16)], axis=1)
        return lax.dot_general(kz, v, tn_dims, preferred_element_type=F32)

    def stage1(gi, c):
        for u in range(group):
            ch = gi * group + u
            sl = chunk_slice(ch)
            ds_ref[ch] = increments(k_ref[sl, :], v_ref[sl, :])
        return c

    lax.fori_loop(0, n_chunks // group, stage1, 0)

    def scan_f(ch, S):
        sf_ref[ch] = S.astype(BF16)
        return S * cd_f + ds_ref[ch, :CHUNK, :]

    def scan_b(t, S):
        ch = n_chunks - 1 - t
        sb_ref[ch] = S.astype(BF16)
        return S * cd_b + ds_ref[ch, CHUNK:, :]

    S0 = increments(km_ref[...], vm_ref[...])[:CHUNK]
    lax.fori_loop(0, n_chunks, scan_f, S0, unroll=4)
    lax.fori_loop(0, n_chunks, scan_b, jnp.zeros((CHUNK, CHUNK), F32), unroll=4)

    def scores(ch):
        sl = chunk_slice(ch)
        q = q_ref[sl, :]
        s = lax.dot_general(q, k_ref[sl, :], nt, preferred_element_type=F32) * dmat
        qf = q.astype(F32)
        return jnp.concatenate([s.astype(BF16), (qf * xi_f).astype(BF16), (qf * xi_b).astype(BF16)], axis=1)

    def mix(ch, lhs):
        rhs = jnp.concatenate([v_ref[chunk_slice(ch), :], sf_ref[ch], sb_ref[ch]], axis=0)
        return jnp.dot(lhs, rhs, preferred_element_type=F32)

    def finish(ch, o):
        sl = chunk_slice(ch)
        mu = jnp.mean(o, axis=-1, keepdims=True)
        oc = o - mu
        var = jnp.mean(oc * oc, axis=-1, keepdims=True)
        on = oc * lax.rsqrt(var + LN_EPS) * gn
        g = g_ref[sl, :].astype(F32)
        o_ref[sl, :] = (g / (1.0 + jnp.exp(-g)) * on).astype(BF16)

    def stage3(gi, c):
        chs = [gi * group + u for u in range(group)]
        lhs = [scores(ch) for ch in chs]
        outs = [mix(ch, l) for ch, l in zip(chs, lhs)]
        for ch, o in zip(chs, outs):
            finish(ch, o)
        return c

    lax.fori_loop(0, n_chunks // group, stage3, 0)


def _retention(proj, proj_meta_pad, lgf, lgb, gn_g, B, n):
    assert n % CHUNK == 0
    n_chunks = n // CHUNK
    group = math.gcd(n_chunks, RET_GROUP_CHUNKS)
    seq_spec = lambda c0: pl.BlockSpec((n, LANES), lambda b, h: (b, c0 // LANES + h))
    meta_spec = lambda c0: pl.BlockSpec((CHUNK, LANES), lambda b, h: (0, c0 // LANES + h))
    lg_spec = pl.BlockSpec((1, 1, LANES), lambda b, h: (h, 0, 0))
    return pl.pallas_call(
        functools.partial(_ret_kernel, n_chunks=n_chunks, group=group),
        grid=(B, RET_HEADS),
        in_specs=[
            seq_spec(COL_QR), seq_spec(COL_KR), seq_spec(COL_VR), seq_spec(COL_GR),
            meta_spec(COL_KR), meta_spec(COL_VR),
            lg_spec, lg_spec,
            pl.BlockSpec((1, LANES), lambda b, h: (0, h)),
        ],
        out_specs=pl.BlockSpec((n, LANES), lambda b, h: (b, h)),
        out_shape=jax.ShapeDtypeStruct((B * n, RET_WIDTH), BF16),
        scratch_shapes=[pltpu.VMEM((n_chunks, 2 * CHUNK, LANES), F32),
                        pltpu.VMEM((n_chunks, CHUNK, LANES), BF16),
                        pltpu.VMEM((n_chunks, CHUNK, LANES), BF16)],
        compiler_params=pltpu.CompilerParams(
            dimension_semantics=("arbitrary", "arbitrary"), vmem_limit_bytes=VMEM_LIMIT),
        name="retention",
    )(proj, proj, proj, proj, proj_meta_pad, proj_meta_pad, lgf, lgb, gn_g)


def _outproj_kernel(ona_ref, oret_ref, x_ref, w_ref, gin_ref, bin_ref, g1_ref, b1_ref, o_ref, *, row_blocks):
    def norm_store(rows, mix):
        h = _layer_norm_rows(x_ref[rows, :], gin_ref[...], bin_ref[...])
        o_ref[rows, :] = _layer_norm_rows(ALPHA * h + mix, g1_ref[...], b1_ref[...])

    pending = None
    start = 0
    for size in row_blocks:
        rows = slice(start, start + size)
        start += size
        lhs = jnp.concatenate([ona_ref[rows, :], oret_ref[rows, :]], axis=1)
        mix = jnp.dot(lhs, w_ref[...], preferred_element_type=F32)
        if pending is not None:
            norm_store(*pending)
        pending = (rows, mix)
    norm_store(*pending)


def _outproj(o_na, o_ret, x2d, w_bf, gin, bin_, g1, b1, tm=512):
    T = x2d.shape[0]
    assert T % tm == 0 and tm % 4 == 0
    row_blocks = (tm // 2, tm // 4, tm // 4)
    row_vec = pl.BlockSpec((1, D_MODEL), lambda i: (0, 0))
    return pl.pallas_call(
        functools.partial(_outproj_kernel, row_blocks=row_blocks),
        grid=(T // tm,),
        in_specs=[
            pl.BlockSpec((tm, NA_WIDTH), lambda i: (i, 0)),
            pl.BlockSpec((tm, RET_WIDTH), lambda i: (i, 0)),
            pl.BlockSpec((tm, D_MODEL), lambda i: (i, 0)),
            pl.BlockSpec((D_MODEL, D_MODEL), lambda i: (0, 0), pipeline_mode=pl.Buffered(1)),
            row_vec, row_vec, row_vec, row_vec,
        ],
        out_specs=pl.BlockSpec((tm, D_MODEL), lambda i: (i, 0)),
        out_shape=jax.ShapeDtypeStruct((T, D_MODEL), F32),
        compiler_params=pltpu.CompilerParams(dimension_semantics=("arbitrary",), vmem_limit_bytes=VMEM_LIMIT),
        name="outproj_ln1",
    )(o_na, o_ret, x2d, w_bf, gin, bin_, g1, b1)


def _ffn_kernel(h_ref, wg_ref, wu_ref, wd_ref, g2_ref, b2_ref, o_ref, hb_ref, *, rb):
    j = pl.program_id(1)
    tm = h_ref.shape[0]

    @pl.when(j == 0)
    def _():
        hb_ref[...] = h_ref[...].astype(BF16)
        o_ref[...] = jnp.zeros_like(o_ref)

    hb = hb_ref[...]
    g = jnp.dot(hb, wg_ref[...], preferred_element_type=F32)
    u = jnp.dot(hb, wu_ref[...], preferred_element_type=F32)
    a = (g / (1.0 + jnp.exp(-g)) * u).astype(BF16)
    o_ref[...] += jnp.dot(a, wd_ref[...], preferred_element_type=F32)

    @pl.when(j == pl.num_programs(1) - 1)
    def _():
        def rows(i, c):
            sl = pl.ds(pl.multiple_of(i * rb, rb), rb)
            o_ref[sl, :] = _layer_norm_rows(ALPHA * h_ref[sl, :] + o_ref[sl, :], g2_ref[...], b2_ref[...])
            return c

        lax.fori_loop(0, tm // rb, rows, 0)


def _ffn(h1, wg_bf, wu_bf, wd_bf, g2, b2, tm=512, tf=512):
    T = h1.shape[0]
    assert T % tm == 0 and D_FF % tf == 0
    row_vec = pl.BlockSpec((1, D_MODEL), lambda i, j: (0, 0))
    return pl.pallas_call(
        functools.partial(_ffn_kernel, rb=128),
        grid=(T // tm, D_FF // tf),
        in_specs=[
            pl.BlockSpec((tm, D_MODEL), lambda i, j: (i, 0)),
            pl.BlockSpec((D_MODEL, tf), lambda i, j: (0, j)),
            pl.BlockSpec((D_MODEL, tf), lambda i, j: (0, j)),
            pl.BlockSpec((tf, D_MODEL), lambda i, j: (j, 0)),
            row_vec, row_vec,
        ],
        out_specs=pl.BlockSpec((tm, D_MODEL), lambda i, j: (i, 0)),
        out_shape=jax.ShapeDtypeStruct((T, D_MODEL), F32),
        scratch_shapes=[pltpu.VMEM((tm, D_MODEL), BF16)],
        compiler_params=pltpu.CompilerParams(
            dimension_semantics=("arbitrary", "arbitrary"), vmem_limit_bytes=VMEM_LIMIT),
        name="ffn_ln2",
    )(h1, wg_bf, wu_bf, wd_bf, g2, b2)


def _rope_tables(pos):
    half = RET_HEAD_DIM // 2
    inv = ROPE_BASE ** (-jnp.arange(half, dtype=F32) / half)
    ang = pos[:, None] * inv[None, :]
    c, s = jnp.cos(ang), jnp.sin(ang)
    cq = jnp.concatenate([c, c], axis=-1)
    sq = jnp.concatenate([-s, s], axis=-1)
    ks = RET_HEAD_DIM ** -0.5
    return (jnp.stack([cq, cq * ks]), jnp.stack([sq, sq * ks]))


def _pick_tm(n, cap):
    tm = cap
    while n % tm:
        tm //= 2
    return tm


def kernel(x_prompt, x_sample, meta_tokens, ln_in_g, ln_in_b, w_in, na_rpb, ret_decay_f, ret_decay_b,
           ret_gn_g, w_out, ln1_g, ln1_b, w_ffn_gate, w_ffn_up, w_ffn_down, ln2_g, ln2_b):
    row = lambda v: v.reshape(1, -1).astype(F32)
    gin, bin_ = row(ln_in_g), row(ln_in_b)
    w_in_bf = w_in[0].astype(BF16)
    w_out_bf = w_out[0].astype(BF16)
    wg_bf, wu_bf, wd_bf = w_ffn_gate[0].astype(BF16), w_ffn_up[0].astype(BF16), w_ffn_down[0].astype(BF16)
    bias = _na_bias(na_rpb[0])
    lg = lambda d: jnp.broadcast_to(jax.nn.log_sigmoid(d[0].astype(F32))[:, None, None], (RET_HEADS, 1, LANES))
    lgf, lgb = lg(ret_decay_f), lg(ret_decay_b)
    gn_g = row(ret_gn_g[0])

    meta_tabs = _rope_tables(jnp.arange(N_META, dtype=F32))
    proj_meta = _inproj(meta_tokens.astype(F32), gin, bin_, w_in_bf, meta_tabs, N_META, N_META)
    proj_meta_pad = jnp.pad(proj_meta, ((CHUNK - N_META, 0), (0, 0)))

    def group(x):
        B, n, _ = x.shape
        x2d = x.reshape(B * n, D_MODEL)
        tabs = _rope_tables(jnp.arange(n, dtype=F32) + float(N_META))
        proj = _inproj(x2d, gin, bin_, w_in_bf, tabs, n, _pick_tm(n, 1024))
        o_na = _na(proj, _na_meta(proj_meta), bias, B, n)
        o_ret = _retention(proj, proj_meta_pad, lgf, lgb, gn_g, B, n)
        h1 = _outproj(o_na, o_ret, x2d, w_out_bf, gin, bin_, row(ln1_g[0]), row(ln1_b[0]), tm=_pick_tm(B * n, 512))
        y = _ffn(h1, wg_bf, wu_bf, wd_bf, row(ln2_g[0]), row(ln2_b[0]), tm=_pick_tm(B * n, 1024))
        return y.reshape(B, n, D_MODEL)

    return (group(x_prompt), group(x_sample))


def _na_meta(proj_meta):
    return jnp.pad(proj_meta, ((0, NA_META_SLOTS - N_META), (0, 0)))
```

```python
import functools
import math

import jax
import jax.numpy as jnp
from jax import lax
from jax.experimental import pallas as pl
from jax.experimental.pallas import tpu as pltpu

D_MODEL = 2048
N_META = 16
GRID_W = 64
NA_HEADS = 16
NA_HEAD_DIM = 64
NA_WIDTH = NA_HEADS * NA_HEAD_DIM
NA_WIN_H = 8
NA_WIN_W = 16
NA_HALF = GRID_W // 2
NA_KEY_COLS = 48
NA_KEY_SHIFT = GRID_W - NA_KEY_COLS
NA_UNION = NA_WIN_H + 1
NA_KEYS = 512
NA_META_SLOTS = NA_KEYS - NA_UNION * NA_KEY_COLS
RET_HEADS = 8
RET_HEAD_DIM = 128
RET_WIDTH = RET_HEADS * RET_HEAD_DIM
CHUNK = 128
ROPE_BASE = 10000.0
IN_WIDTH = 3 * NA_WIDTH + 4 * RET_WIDTH
D_FF = 5632
LN_EPS = 1e-5
ALPHA = 2.0 ** 0.25

COL_QA, COL_KA, COL_VA = 0, NA_WIDTH, 2 * NA_WIDTH
COL_QR = 3 * NA_WIDTH
COL_KR = COL_QR + RET_WIDTH
COL_VR = COL_KR + RET_WIDTH
COL_GR = COL_VR + RET_WIDTH

NA_GROUP_ROWS = 32
RET_GROUP_CHUNKS = 32

LANES = 128
NEG = -1e30
VMEM_LIMIT = 56 * 1024 * 1024

F32 = jnp.float32
BF16 = jnp.bfloat16


def _layer_norm_rows(x, g, b):
    mu = jnp.mean(x, axis=-1, keepdims=True)
    xc = x - mu
    var = jnp.mean(xc * xc, axis=-1, keepdims=True)
    return xc * lax.rsqrt(var + LN_EPS) * g + b


def _inproj_kernel(x_ref, g_ref, b_ref, w_ref, c_ref, s_ref, o_ref, hb_ref, *, tn, rb):
    j = pl.program_id(1)
    tm = x_ref.shape[0]

    col0 = j * tn
    rotary = jnp.logical_and(col0 >= COL_QR, col0 < COL_VR)

    @pl.when(j == 0)
    def _():
        for sb in range(tm // rb):
            rows = slice(sb * rb, (sb + 1) * rb)
            hb_ref[rows, :] = _layer_norm_rows(x_ref[rows, :], g_ref[...], b_ref[...]).astype(BF16)
            o_ref[rows, :] = jnp.dot(hb_ref[rows, :], w_ref[...], preferred_element_type=F32).astype(BF16)

    @pl.when(jnp.logical_and(j != 0, jnp.logical_not(rotary)))
    def _():
        o_ref[...] = jnp.dot(hb_ref[...], w_ref[...], preferred_element_type=F32).astype(BF16)

    @pl.when(rotary)
    def _():
        acc = jnp.dot(hb_ref[...], w_ref[...], preferred_element_type=F32)
        c = c_ref[...]
        s = s_ref[...]
        for t in range(tn // LANES):
            xs = acc[:, t * LANES:(t + 1) * LANES]
            o_ref[:, t * LANES:(t + 1) * LANES] = (xs * c + pltpu.roll(xs, LANES // 2, 1) * s).astype(BF16)


def _inproj(x2d, ln_g, ln_b, w_bf, tabs, n_seq, tm, tn=1024):
    T = x2d.shape[0]
    assert T % tm == 0 and n_seq % tm == 0 and IN_WIDTH % tn == 0 and RET_WIDTH % tn == 0
    blocks_per_seq = n_seq // tm

    def tab_index(i, j):
        col0 = j * tn
        is_k = jnp.logical_and(col0 >= COL_KR, col0 < COL_VR)
        return (jnp.where(is_k, 1, 0), i % blocks_per_seq, 0)

    tab_spec = pl.BlockSpec((None, tm, LANES), tab_index)
    rb = min(tm, 128)
    return pl.pallas_call(
        functools.partial(_inproj_kernel, tn=tn, rb=rb),
        grid=(T // tm, IN_WIDTH // tn),
        in_specs=[
            pl.BlockSpec((tm, D_MODEL), lambda i, j: (i, 0)),
            pl.BlockSpec((1, D_MODEL), lambda i, j: (0, 0)),
            pl.BlockSpec((1, D_MODEL), lambda i, j: (0, 0)),
            pl.BlockSpec((D_MODEL, tn), lambda i, j: (0, j)),
            tab_spec, tab_spec,
        ],
        out_specs=pl.BlockSpec((tm, tn), lambda i, j: (i, j)),
        out_shape=jax.ShapeDtypeStruct((T, IN_WIDTH), BF16),
        scratch_shapes=[pltpu.VMEM((tm, D_MODEL), BF16)],
        compiler_params=pltpu.CompilerParams(
            dimension_semantics=("arbitrary", "arbitrary"), vmem_limit_bytes=VMEM_LIMIT),
        name="inproj",
    )(x2d, ln_g, ln_b, w_bf, *tabs)


def _na_kernel(q_ref, k_ref, v_ref, km_ref, vm_ref, bias_ref, o_ref, *, rows, unroll):
    lane = lax.broadcasted_iota(jnp.int32, (1, LANES), 1)
    scale = NA_HEAD_DIM ** -0.5
    qmask = [jnp.where((lane // NA_HEAD_DIM) == hh, scale, 0.0).astype(BF16) for hh in range(2)]
    km = km_ref[...]
    vm = vm_ref[...]
    nt = (((1,), (1,)), ((), ()))

    def window(ref, meta, b0, hf):
        parts = [ref[pl.ds(pl.multiple_of((b0 + i) * GRID_W + NA_KEY_SHIFT * hf, NA_KEY_SHIFT), NA_KEY_COLS), :]
                 for i in range(NA_UNION)]
        return jnp.concatenate(parts + [meta], axis=0)

    units = [(u, hf) for u in range(unroll // 2) for hf in range(2)]

    def locate(g, idx):
        u, hf = units[idx]
        r0 = g * unroll + 2 * u
        b0 = jnp.clip(r0 - NA_WIN_H // 2, 0, rows - NA_UNION)
        qsl = [pl.ds(pl.multiple_of((r0 + t) * GRID_W + NA_HALF * hf, NA_HALF), NA_HALF) for t in range(2)]
        return r0 - b0, qsl, b0, hf

    def scores(g, idx):
        e0, qsl, b0, hf = locate(g, idx)
        q2 = jnp.concatenate([q_ref[sl, :] * qmask[hh] for sl in qsl for hh in range(2)], axis=0)
        s = lax.dot_general(q2, window(k_ref, km, b0, hf), nt, preferred_element_type=F32)
        bias = jnp.concatenate([bias_ref[0, hf, pl.ds(e0 + t, 1)][0] for t in range(2)], axis=0)
        return s + bias

    def finish(s, g, idx):
        _, qsl, b0, hf = locate(g, idx)
        m = jnp.max(s, axis=-1, keepdims=True)
        p = jnp.exp(s - m)
        den = jnp.sum(p, axis=-1, keepdims=True)
        o2 = jnp.dot(p.astype(BF16), window(v_ref, vm, b0, hf), preferred_element_type=F32) / den
        for t in range(2):
            blk = o2[2 * NA_HALF * t:2 * NA_HALF * (t + 1)]
            o_ref[qsl[t], :] = jnp.where(lane < NA_HEAD_DIM, blk[:NA_HALF], blk[NA_HALF:]).astype(BF16)

    def body(g, c):
        staged = [scores(g, idx) for idx in range(len(units))]
        for idx in range(len(units)):
            finish(staged[idx], g, idx)
        return c

    lax.fori_loop(0, rows // unroll, body, 0)


def _na_bias_kernel(rpb_ref, o_ref):
    c32 = lax.broadcasted_iota(jnp.int32, (NA_HALF, LANES), 0)
    lane = lax.broadcasted_iota(jnp.int32, (NA_HALF, LANES), 1)
    masked = jnp.where(lane + c32 < 0, 0.0, NEG).astype(F32)[:, :NA_KEY_COLS]
    meta = jnp.where(lane + LANES * c32 < N_META + LANES * c32, 0.0, NEG).astype(F32)[:, :NA_META_SLOTS]
    tiles = {}
    for hf in range(2):
        c = c32 + NA_HALF * hf
        kc = lane + NA_KEY_SHIFT * hf
        cs = jnp.clip(c - NA_WIN_W // 2, 0, GRID_W - NA_WIN_W)
        valid = (kc >= cs) & (kc < cs + NA_WIN_W) & (lane < NA_KEY_COLS)
        shift = (LANES - (NA_WIN_W - 1) - NA_KEY_SHIFT * hf + NA_HALF * hf) % LANES
        for h in range(2):
            for dr in range(2 * NA_WIN_H - 1):
                w = jnp.broadcast_to(rpb_ref[h, dr:dr + 1, :], (NA_HALF, LANES))
                tiles[(hf, h, dr)] = jnp.where(valid, pltpu.roll(w, shift, 1, stride=1, stride_axis=0), NEG)
    for hf in range(2):
        for e in range(NA_UNION):
            d = 1 if e > NA_WIN_H // 2 else 0
            heads = []
            for h in range(2):
                parts = [tiles[(hf, h, i - e + NA_WIN_H - 1)][:, :NA_KEY_COLS] if 0 <= i - d < NA_WIN_H else masked
                         for i in range(NA_UNION)]
                heads.append(jnp.concatenate(parts + [meta], axis=1))
            o_ref[0, hf, e] = jnp.concatenate(heads, axis=0)


def _na_bias(rpb):
    n_dr = 2 * NA_WIN_H - 1
    rpb_lanes = jnp.pad(rpb.astype(F32), ((0, 0), (0, 0), (0, LANES - rpb.shape[-1])))
    shape = (NA_HEADS // 2, 2, NA_UNION, 2 * NA_HALF, NA_KEYS)
    return pl.pallas_call(
        _na_bias_kernel,
        grid=(NA_HEADS // 2,),
        in_specs=[pl.BlockSpec((2, n_dr, LANES), lambda p: (p, 0, 0))],
        out_specs=pl.BlockSpec((1,) + shape[1:], lambda p: (p, 0, 0, 0, 0)),
        out_shape=jax.ShapeDtypeStruct(shape, F32),
        compiler_params=pltpu.CompilerParams(dimension_semantics=("arbitrary",), vmem_limit_bytes=VMEM_LIMIT),
        name="na_bias",
    )(rpb_lanes)


def _na(proj, proj_meta_keys, bias, B, n):
    rows = n // GRID_W
    unroll = math.gcd(rows, NA_GROUP_ROWS)
    assert rows >= NA_UNION and unroll % 2 == 0
    pairs = NA_WIDTH // LANES
    seq_spec = lambda c0: pl.BlockSpec((n, LANES), lambda hp, b: (b, c0 // LANES + hp))
    meta_spec = lambda c0: pl.BlockSpec((NA_META_SLOTS, LANES), lambda hp, b: (0, c0 // LANES + hp))
    return pl.pallas_call(
        functools.partial(_na_kernel, rows=rows, unroll=unroll),
        grid=(pairs, B),
        in_specs=[
            seq_spec(COL_QA), seq_spec(COL_KA), seq_spec(COL_VA),
            meta_spec(COL_KA), meta_spec(COL_VA),
            pl.BlockSpec((1, 2, NA_UNION, 2 * NA_HALF, NA_KEYS), lambda hp, b: (hp, 0, 0, 0, 0)),
        ],
        out_specs=pl.BlockSpec((n, LANES), lambda hp, b: (b, hp)),
        out_shape=jax.ShapeDtypeStruct((B * n, NA_WIDTH), BF16),
        compiler_params=pltpu.CompilerParams(
            dimension_semantics=("arbitrary", "arbitrary"), vmem_limit_bytes=VMEM_LIMIT),
        name="na_attn",
    )(proj, proj, proj, proj_meta_keys, proj_meta_keys, bias)


def _ret_kernel(q_ref, k_ref, v_ref, g_ref, km_ref, vm_ref, lgf_ref, lgb_ref, gn_ref, o_ref,
                ds_ref, sf_ref, sb_ref, *, n_chunks, group):
    lgf = lgf_ref[0]
    lgb = lgb_ref[0]
    ii = lax.broadcasted_iota(jnp.int32, (CHUNK, CHUNK), 0).astype(F32)
    jj = lax.broadcasted_iota(jnp.int32, (CHUNK, CHUNK), 1).astype(F32)
    diff = ii - jj
    dmat = jnp.where(diff >= 0, jnp.exp(jnp.maximum(diff, 0.0) * lgf), jnp.exp(jnp.maximum(-diff, 0.0) * lgb))
    xi_f = jnp.exp((ii + 1.0) * lgf)
    zeta_f = jnp.exp((CHUNK - 1.0 - ii) * lgf)
    cd_f = jnp.exp(CHUNK * lgf)
    xi_b = jnp.exp((CHUNK - ii) * lgb)
    zeta_b = jnp.exp(ii * lgb)
    cd_b = jnp.exp(CHUNK * lgb)
    nt = (((1,), (1,)), ((), ()))
    tn_dims = (((0,), (0,)), ((), ()))
    gn = gn_ref[...]

    def chunk_slice(ch):
        return pl.ds(pl.multiple_of(ch * CHUNK, CHUNK), CHUNK)

    def increments(k, v):
        kf = k.astype(F32)
        kz = jnp.concatenate([(kf * zeta_f).astype(BF16), (kf * zeta_b).astype(BF16)], axis=1)
        return lax.dot_general(kz, v, tn_dims, preferred_element_type=F32)

    def stage1(gi, c):
        for u in range(group):
            ch = gi * group + u
            sl = chunk_slice(ch)
            ds_ref[ch] = increments(k_ref[sl, :], v_ref[sl, :])
        return c

    lax.fori_loop(0, n_chunks // group, stage1, 0)

    def scan_f(ch, S):
        sf_ref[ch] = S.astype(BF16)
        return S * cd_f + ds_ref[ch, :CHUNK, :]

    def scan_b(t, S):
        ch = n_chunks - 1 - t
        sb_ref[ch] = S.astype(BF16)
        return S * cd_b + ds_ref[ch, CHUNK:, :]

    S0 = increments(km_ref[...], vm_ref[...])[:CHUNK]
    lax.fori_loop(0, n_chunks, scan_f, S0, unroll=4)
    lax.fori_loop(0, n_chunks, scan_b, jnp.zeros((CHUNK, CHUNK), F32), unroll=4)

    def scores(ch):
        sl = chunk_slice(ch)
        q = q_ref[sl, :]
        s = lax.dot_general(q, k_ref[sl, :], nt, preferred_element_type=F32) * dmat
        qf = q.astype(F32)
        return jnp.concatenate([s.astype(BF16), (qf * xi_f).astype(BF16), (qf * xi_b).astype(BF16)], axis=1)

    def mix(ch, lhs):
        rhs = jnp.concatenate([v_ref[chunk_slice(ch), :], sf_ref[ch], sb_ref[ch]], axis=0)
        return jnp.dot(lhs, rhs, preferred_element_type=F32)

    def finish(ch, o):
        sl = chunk_slice(ch)
        mu = jnp.mean(o, axis=-1, keepdims=True)
        oc = o - mu
        var = jnp.mean(oc * oc, axis=-1, keepdims=True)
        on = oc * lax.rsqrt(var + LN_EPS) * gn
        g = g_ref[sl, :].astype(F32)
        o_ref[sl, :] = (g / (1.0 + jnp.exp(-g)) * on).astype(BF16)

    def stage3(gi, c):
        chs = [gi * group + u for u in range(group)]
        lhs = [scores(ch) for ch in chs]
        outs = [mix(ch, l) for ch, l in zip(chs, lhs)]
        for ch, o in zip(chs, outs):
            finish(ch, o)
        return c

    lax.fori_loop(0, n_chunks // group, stage3, 0)


def _retention(proj, proj_meta_pad, lgf, lgb, gn_g, B, n):
    assert n % CHUNK == 0
    n_chunks = n // CHUNK
    group = math.gcd(n_chunks, RET_GROUP_CHUNKS)
    seq_spec = lambda c0: pl.BlockSpec((n, LANES), lambda b, h: (b, c0 // LANES + h))
    meta_spec = lambda c0: pl.BlockSpec((CHUNK, LANES), lambda b, h: (0, c0 // LANES + h))
    lg_spec = pl.BlockSpec((1, 1, LANES), lambda b, h: (h, 0, 0))
    return pl.pallas_call(
        functools.partial(_ret_kernel, n_chunks=n_chunks, group=group),
        grid=(B, RET_HEADS),
        in_specs=[
            seq_spec(COL_QR), seq_spec(COL_KR), seq_spec(COL_VR), seq_spec(COL_GR),
            meta_spec(COL_KR), meta_spec(COL_VR),
            lg_spec, lg_spec,
            pl.BlockSpec((1, LANES), lambda b, h: (0, h)),
        ],
        out_specs=pl.BlockSpec((n, LANES), lambda b, h: (b, h)),
        out_shape=jax.ShapeDtypeStruct((B * n, RET_WIDTH), BF16),
        scratch_shapes=[pltpu.VMEM((n_chunks, 2 * CHUNK, LANES), F32),
                        pltpu.VMEM((n_chunks, CHUNK, LANES), BF16),
                        pltpu.VMEM((n_chunks, CHUNK, LANES), BF16)],
        compiler_params=pltpu.CompilerParams(
            dimension_semantics=("arbitrary", "arbitrary"), vmem_limit_bytes=VMEM_LIMIT),
        name="retention",
    )(proj, proj, proj, proj, proj_meta_pad, proj_meta_pad, lgf, lgb, gn_g)


def _outproj_kernel(ona_ref, oret_ref, x_ref, w_ref, gin_ref, bin_ref, g1_ref, b1_ref, o_ref, *, row_blocks):
    def norm_store(rows, mix):
        h = _layer_norm_rows(x_ref[rows, :], gin_ref[...], bin_ref[...])
        o_ref[rows, :] = _layer_norm_rows(ALPHA * h + mix, g1_ref[...], b1_ref[...])

    pending = None
    start = 0
    for size in row_blocks:
        rows = slice(start, start + size)
        start += size
        lhs = jnp.concatenate([ona_ref[rows, :], oret_ref[rows, :]], axis=1)
        mix = jnp.dot(lhs, w_ref[...], preferred_element_type=F32)
        if pending is not None:
            norm_store(*pending)
        pending = (rows, mix)
    norm_store(*pending)


def _outproj(o_na, o_ret, x2d, w_bf, gin, bin_, g1, b1, tm=512):
    T = x2d.shape[0]
    assert T % tm == 0 and tm % 4 == 0
    row_blocks = (tm // 2, tm // 4, tm // 4)
    row_vec = pl.BlockSpec((1, D_MODEL), lambda i: (0, 0))
    return pl.pallas_call(
        functools.partial(_outproj_kernel, row_blocks=row_blocks),
        grid=(T // tm,),
        in_specs=[
            pl.BlockSpec((tm, NA_WIDTH), lambda i: (i, 0)),
            pl.BlockSpec((tm, RET_WIDTH), lambda i: (i, 0)),
            pl.BlockSpec((tm, D_MODEL), lambda i: (i, 0)),
            pl.BlockSpec((D_MODEL, D_MODEL), lambda i: (0, 0), pipeline_mode=pl.Buffered(1)),
            row_vec, row_vec, row_vec, row_vec,
        ],
        out_specs=pl.BlockSpec((tm, D_MODEL), lambda i: (i, 0)),
        out_shape=jax.ShapeDtypeStruct((T, D_MODEL), F32),
        compiler_params=pltpu.CompilerParams(dimension_semantics=("arbitrary",), vmem_limit_bytes=VMEM_LIMIT),
        name="outproj_ln1",
    )(o_na, o_ret, x2d, w_bf, gin, bin_, g1, b1)


def _ffn_kernel(h_ref, wg_hbm, wu_hbm, wd_hbm, g2_ref, b2_ref, o_ref, hb_ref, wg_buf, wu_buf, wd_buf, sem,
                *, rb, tf):
    i = pl.program_id(0)
    n_blocks = pl.num_programs(0)
    tm = h_ref.shape[0]
    n_tiles = D_FF // tf

    def tile_copies(t, slot):
        cols = pl.ds(pl.multiple_of(t * tf, tf), tf)
        return (pltpu.make_async_copy(wg_hbm.at[:, cols], wg_buf.at[slot], sem.at[slot, 0]),
                pltpu.make_async_copy(wu_hbm.at[:, cols], wu_buf.at[slot], sem.at[slot, 1]),
                pltpu.make_async_copy(wd_hbm.at[cols, :], wd_buf.at[slot], sem.at[slot, 2]))

    def start(t, slot):
        for copy in tile_copies(t, slot):
            copy.start()

    def wait(t, slot):
        for copy in tile_copies(t, slot):
            copy.wait()

    first_slot = (i * n_tiles) % 2

    @pl.when(i == 0)
    def _():
        start(0, 0)

    hb_ref[...] = h_ref[...].astype(BF16)
    o_ref[...] = jnp.zeros_like(o_ref)

    def tile(t, c):
        slot = (first_slot + t) % 2
        wait(t, slot)

        @pl.when(t + 1 < n_tiles)
        def _():
            start(t + 1, 1 - slot)

        @pl.when(jnp.logical_and(t + 1 == n_tiles, i + 1 < n_blocks))
        def _():
            start(0, 1 - slot)

        hb = hb_ref[...]
        g = jnp.dot(hb, wg_buf[slot], preferred_element_type=F32)
        u = jnp.dot(hb, wu_buf[slot], preferred_element_type=F32)
        a = (g / (1.0 + jnp.exp(-g)) * u).astype(BF16)
        o_ref[...] += jnp.dot(a, wd_buf[slot], preferred_element_type=F32)
        return c

    lax.fori_loop(0, n_tiles, tile, 0)

    def rows(r, c):
        sl = pl.ds(pl.multiple_of(r * rb, rb), rb)
        o_ref[sl, :] = _layer_norm_rows(ALPHA * h_ref[sl, :] + o_ref[sl, :], g2_ref[...], b2_ref[...])
        return c

    lax.fori_loop(0, tm // rb, rows, 0)


def _ffn(h1, wg_bf, wu_bf, wd_bf, g2, b2, tm=512, tf=512):
    T = h1.shape[0]
    assert T % tm == 0 and D_FF % tf == 0
    row_vec = pl.BlockSpec((1, D_MODEL), lambda i: (0, 0))
    hbm = pl.BlockSpec(memory_space=pl.ANY)
    return pl.pallas_call(
        functools.partial(_ffn_kernel, rb=128, tf=tf),
        grid=(T // tm,),
        in_specs=[pl.BlockSpec((tm, D_MODEL), lambda i: (i, 0)), hbm, hbm, hbm, row_vec, row_vec],
        out_specs=pl.BlockSpec((tm, D_MODEL), lambda i: (i, 0)),
        out_shape=jax.ShapeDtypeStruct((T, D_MODEL), F32),
        scratch_shapes=[pltpu.VMEM((tm, D_MODEL), BF16),
                        pltpu.VMEM((2, D_MODEL, tf), BF16),
                        pltpu.VMEM((2, D_MODEL, tf), BF16),
                        pltpu.VMEM((2, tf, D_MODEL), BF16),
                        pltpu.SemaphoreType.DMA((2, 3))],
        compiler_params=pltpu.CompilerParams(dimension_semantics=("arbitrary",), vmem_limit_bytes=VMEM_LIMIT),
        name="ffn_ln2",
    )(h1, wg_bf, wu_bf, wd_bf, g2, b2)


def _rope_tables(pos):
    half = RET_HEAD_DIM // 2
    inv = ROPE_BASE ** (-jnp.arange(half, dtype=F32) / half)
    ang = pos[:, None] * inv[None, :]
    c, s = jnp.cos(ang), jnp.sin(ang)
    cq = jnp.concatenate([c, c], axis=-1)
    sq = jnp.concatenate([-s, s], axis=-1)
    ks = RET_HEAD_DIM ** -0.5
    return (jnp.stack([cq, cq * ks]), jnp.stack([sq, sq * ks]))


def _pick_tm(n, cap):
    tm = cap
    while n % tm:
        tm //= 2
    return tm


def kernel(x_prompt, x_sample, meta_tokens, ln_in_g, ln_in_b, w_in, na_rpb, ret_decay_f, ret_decay_b,
           ret_gn_g, w_out, ln1_g, ln1_b, w_ffn_gate, w_ffn_up, w_ffn_down, ln2_g, ln2_b):
    row = lambda v: v.reshape(1, -1).astype(F32)
    gin, bin_ = row(ln_in_g), row(ln_in_b)
    w_in_bf = w_in[0].astype(BF16)
    w_out_bf = w_out[0].astype(BF16)
    wg_bf, wu_bf, wd_bf = w_ffn_gate[0].astype(BF16), w_ffn_up[0].astype(BF16), w_ffn_down[0].astype(BF16)
    bias = _na_bias(na_rpb[0])
    lg = lambda d: jnp.broadcast_to(jax.nn.log_sigmoid(d[0].astype(F32))[:, None, None], (RET_HEADS, 1, LANES))
    lgf, lgb = lg(ret_decay_f), lg(ret_decay_b)
    gn_g = row(ret_gn_g[0])

    meta_tabs = _rope_tables(jnp.arange(N_META, dtype=F32))
    proj_meta = _inproj(meta_tokens.astype(F32), gin, bin_, w_in_bf, meta_tabs, N_META, N_META)
    proj_meta_pad = jnp.pad(proj_meta, ((CHUNK - N_META, 0), (0, 0)))

    def group(x):
        B, n, _ = x.shape
        x2d = x.reshape(B * n, D_MODEL)
        tabs = _rope_tables(jnp.arange(n, dtype=F32) + float(N_META))
        proj = _inproj(x2d, gin, bin_, w_in_bf, tabs, n, _pick_tm(n, 1024))
        o_na = _na(proj, _na_meta(proj_meta), bias, B, n)
        o_ret = _retention(proj, proj_meta_pad, lgf, lgb, gn_g, B, n)
        h1 = _outproj(o_na, o_ret, x2d, w_out_bf, gin, bin_, row(ln1_g[0]), row(ln1_b[0]), tm=_pick_tm(B * n, 512))
        y = _ffn(h1, wg_bf, wu_bf, wd_bf, row(ln2_g[0]), row(ln2_b[0]), tm=_pick_tm(B * n, 1024))
        return y.reshape(B, n, D_MODEL)

    return (group(x_prompt), group(x_sample))


def _na_meta(proj_meta):
    return jnp.pad(proj_meta, ((0, NA_META_SLOTS - N_META), (0, 0)))
```

```python
import functools
import math

import jax
import jax.numpy as jnp
from jax import lax
from jax.experimental import pallas as pl
from jax.experimental.pallas import tpu as pltpu

D_MODEL = 2048
N_META = 16
GRID_W = 64
NA_HEADS = 16
NA_HEAD_DIM = 64
NA_WIDTH = NA_HEADS * NA_HEAD_DIM
NA_WIN_H = 8
NA_WIN_W = 16
NA_HALF = GRID_W // 2
NA_KEY_COLS = 48
NA_KEY_SHIFT = GRID_W - NA_KEY_COLS
NA_UNION = NA_WIN_H + 1
NA_KEYS = 512
NA_META_SLOTS = NA_KEYS - NA_UNION * NA_KEY_COLS
RET_HEADS = 8
RET_HEAD_DIM = 128
RET_WIDTH = RET_HEADS * RET_HEAD_DIM
CHUNK = 128
ROPE_BASE = 10000.0
IN_WIDTH = 3 * NA_WIDTH + 4 * RET_WIDTH
D_FF = 5632
LN_EPS = 1e-5
ALPHA = 2.0 ** 0.25

COL_QA, COL_KA, COL_VA = 0, NA_WIDTH, 2 * NA_WIDTH
COL_QR = 3 * NA_WIDTH
COL_KR = COL_QR + RET_WIDTH
COL_VR = COL_KR + RET_WIDTH
COL_GR = COL_VR + RET_WIDTH

NA_GROUP_ROWS = 32
RET_GROUP_CHUNKS = 32

LANES = 128
NEG = -1e30
VMEM_LIMIT = 56 * 1024 * 1024

F32 = jnp.float32
BF16 = jnp.bfloat16


def _layer_norm_rows(x, g, b):
    mu = jnp.mean(x, axis=-1, keepdims=True)
    xc = x - mu
    var = jnp.mean(xc * xc, axis=-1, keepdims=True)
    return xc * lax.rsqrt(var + LN_EPS) * g + b


def _inproj_kernel(x_ref, g_ref, b_ref, w_ref, c_ref, s_ref, o_ref, hb_ref, *, tn, rb):
    j = pl.program_id(1)
    tm = x_ref.shape[0]

    col0 = j * tn
    rotary = jnp.logical_and(col0 >= COL_QR, col0 < COL_VR)

    @pl.when(j == 0)
    def _():
        for sb in range(tm // rb):
            rows = slice(sb * rb, (sb + 1) * rb)
            hb_ref[rows, :] = _layer_norm_rows(x_ref[rows, :], g_ref[...], b_ref[...]).astype(BF16)
            o_ref[rows, :] = jnp.dot(hb_ref[rows, :], w_ref[...], preferred_element_type=F32).astype(BF16)

    @pl.when(jnp.logical_and(j != 0, jnp.logical_not(rotary)))
    def _():
        o_ref[...] = jnp.dot(hb_ref[...], w_ref[...], preferred_element_type=F32).astype(BF16)

    @pl.when(rotary)
    def _():
        acc = jnp.dot(hb_ref[...], w_ref[...], preferred_element_type=F32)
        c = c_ref[...]
        s = s_ref[...]
        for t in range(tn // LANES):
            xs = acc[:, t * LANES:(t + 1) * LANES]
            o_ref[:, t * LANES:(t + 1) * LANES] = (xs * c + pltpu.roll(xs, LANES // 2, 1) * s).astype(BF16)


def _inproj(x2d, ln_g, ln_b, w_bf, tabs, n_seq, tm, tn=1024):
    T = x2d.shape[0]
    assert T % tm == 0 and n_seq % tm == 0 and IN_WIDTH % tn == 0 and RET_WIDTH % tn == 0
    blocks_per_seq = n_seq // tm

    def tab_index(i, j):
        col0 = j * tn
        is_k = jnp.logical_and(col0 >= COL_KR, col0 < COL_VR)
        return (jnp.where(is_k, 1, 0), i % blocks_per_seq, 0)

    tab_spec = pl.BlockSpec((None, tm, LANES), tab_index)
    rb = min(tm, 128)
    return pl.pallas_call(
        functools.partial(_inproj_kernel, tn=tn, rb=rb),
        grid=(T // tm, IN_WIDTH // tn),
        in_specs=[
            pl.BlockSpec((tm, D_MODEL), lambda i, j: (i, 0)),
            pl.BlockSpec((1, D_MODEL), lambda i, j: (0, 0)),
            pl.BlockSpec((1, D_MODEL), lambda i, j: (0, 0)),
            pl.BlockSpec((D_MODEL, tn), lambda i, j: (0, j)),
            tab_spec, tab_spec,
        ],
        out_specs=pl.BlockSpec((tm, tn), lambda i, j: (i, j)),
        out_shape=jax.ShapeDtypeStruct((T, IN_WIDTH), BF16),
        scratch_shapes=[pltpu.VMEM((tm, D_MODEL), BF16)],
        compiler_params=pltpu.CompilerParams(
            dimension_semantics=("arbitrary", "arbitrary"), vmem_limit_bytes=VMEM_LIMIT),
        name="inproj",
    )(x2d, ln_g, ln_b, w_bf, *tabs)


def _na_kernel(q_ref, k_ref, v_ref, km_ref, vm_ref, bias_ref, o_ref, *, rows, unroll):
    lane = lax.broadcasted_iota(jnp.int32, (1, LANES), 1)
    scale = NA_HEAD_DIM ** -0.5
    qmask = [jnp.where((lane // NA_HEAD_DIM) == hh, scale, 0.0).astype(BF16) for hh in range(2)]
    km = km_ref[...]
    vm = vm_ref[...]
    nt = (((1,), (1,)), ((), ()))

    def window(ref, meta, b0, hf):
        parts = [ref[pl.ds(pl.multiple_of((b0 + i) * GRID_W + NA_KEY_SHIFT * hf, NA_KEY_SHIFT), NA_KEY_COLS), :]
                 for i in range(NA_UNION)]
        return jnp.concatenate(parts + [meta], axis=0)

    units = [(u, hf) for u in range(unroll // 2) for hf in range(2)]

    def locate(g, idx):
        u, hf = units[idx]
        r0 = g * unroll + 2 * u
        b0 = jnp.clip(r0 - NA_WIN_H // 2, 0, rows - NA_UNION)
        qsl = [pl.ds(pl.multiple_of((r0 + t) * GRID_W + NA_HALF * hf, NA_HALF), NA_HALF) for t in range(2)]
        return r0 - b0, qsl, b0, hf

    def scores(g, idx):
        e0, qsl, b0, hf = locate(g, idx)
        q2 = jnp.concatenate([q_ref[sl, :] * qmask[hh] for sl in qsl for hh in range(2)], axis=0)
        s = lax.dot_general(q2, window(k_ref, km, b0, hf), nt, preferred_element_type=F32)
        bias = jnp.concatenate([bias_ref[0, hf, pl.ds(e0 + t, 1)][0] for t in range(2)], axis=0)
        return s + bias

    def finish(s, g, idx):
        _, qsl, b0, hf = locate(g, idx)
        m = jnp.max(s, axis=-1, keepdims=True)
        p = jnp.exp(s - m)
        den = jnp.sum(p, axis=-1, keepdims=True)
        o2 = jnp.dot(p.astype(BF16), window(v_ref, vm, b0, hf), preferred_element_type=F32) / den
        for t in range(2):
            blk = o2[2 * NA_HALF * t:2 * NA_HALF * (t + 1)]
            o_ref[qsl[t], :] = jnp.where(lane < NA_HEAD_DIM, blk[:NA_HALF], blk[NA_HALF:]).astype(BF16)

    def body(g, c):
        staged = [scores(g, idx) for idx in range(len(units))]
        for idx in range(len(units)):
            finish(staged[idx], g, idx)
        return c

    lax.fori_loop(0, rows // unroll, body, 0)


def _na_bias_kernel(rpb_ref, o_ref):
    c32 = lax.broadcasted_iota(jnp.int32, (NA_HALF, LANES), 0)
    lane = lax.broadcasted_iota(jnp.int32, (NA_HALF, LANES), 1)
    masked = jnp.where(lane + c32 < 0, 0.0, NEG).astype(F32)[:, :NA_KEY_COLS]
    meta = jnp.where(lane + LANES * c32 < N_META + LANES * c32, 0.0, NEG).astype(F32)[:, :NA_META_SLOTS]
    tiles = {}
    for hf in range(2):
        c = c32 + NA_HALF * hf
        kc = lane + NA_KEY_SHIFT * hf
        cs = jnp.clip(c - NA_WIN_W // 2, 0, GRID_W - NA_WIN_W)
        valid = (kc >= cs) & (kc < cs + NA_WIN_W) & (lane < NA_KEY_COLS)
        shift = (LANES - (NA_WIN_W - 1) - NA_KEY_SHIFT * hf + NA_HALF * hf) % LANES
        for h in range(2):
            for dr in range(2 * NA_WIN_H - 1):
                w = jnp.broadcast_to(rpb_ref[h, dr:dr + 1, :], (NA_HALF, LANES))
                tiles[(hf, h, dr)] = jnp.where(valid, pltpu.roll(w, shift, 1, stride=1, stride_axis=0), NEG)
    for hf in range(2):
        for e in range(NA_UNION):
            d = 1 if e > NA_WIN_H // 2 else 0
            heads = []
            for h in range(2):
                parts = [tiles[(hf, h, i - e + NA_WIN_H - 1)][:, :NA_KEY_COLS] if 0 <= i - d < NA_WIN_H else masked
                         for i in range(NA_UNION)]
                heads.append(jnp.concatenate(parts + [meta], axis=1))
            o_ref[0, hf, e] = jnp.concatenate(heads, axis=0)


def _na_bias(rpb):
    n_dr = 2 * NA_WIN_H - 1
    rpb_lanes = jnp.pad(rpb.astype(F32), ((0, 0), (0, 0), (0, LANES - rpb.shape[-1])))
    shape = (NA_HEADS // 2, 2, NA_UNION, 2 * NA_HALF, NA_KEYS)
    return pl.pallas_call(
        _na_bias_kernel,
        grid=(NA_HEADS // 2,),
        in_specs=[pl.BlockSpec((2, n_dr, LANES), lambda p: (p, 0, 0))],
        out_specs=pl.BlockSpec((1,) + shape[1:], lambda p: (p, 0, 0, 0, 0)),
        out_shape=jax.ShapeDtypeStruct(shape, F32),
        compiler_params=pltpu.CompilerParams(dimension_semantics=("arbitrary",), vmem_limit_bytes=VMEM_LIMIT),
        name="na_bias",
    )(rpb_lanes)


def _na(proj, proj_meta_keys, bias, B, n):
    rows = n // GRID_W
    unroll = math.gcd(rows, NA_GROUP_ROWS)
    assert rows >= NA_UNION and unroll % 2 == 0
    pairs = NA_WIDTH // LANES
    seq_spec = lambda c0: pl.BlockSpec((n, LANES), lambda hp, b: (b, c0 // LANES + hp))
    meta_spec = lambda c0: pl.BlockSpec((NA_META_SLOTS, LANES), lambda hp, b: (0, c0 // LANES + hp))
    return pl.pallas_call(
        functools.partial(_na_kernel, rows=rows, unroll=unroll),
        grid=(pairs, B),
        in_specs=[
            seq_spec(COL_QA), seq_spec(COL_KA), seq_spec(COL_VA),
            meta_spec(COL_KA), meta_spec(COL_VA),
            pl.BlockSpec((1, 2, NA_UNION, 2 * NA_HALF, NA_KEYS), lambda hp, b: (hp, 0, 0, 0, 0)),
        ],
        out_specs=pl.BlockSpec((n, LANES), lambda hp, b: (b, hp)),
        out_shape=jax.ShapeDtypeStruct((B * n, NA_WIDTH), BF16),
        compiler_params=pltpu.CompilerParams(
            dimension_semantics=("arbitrary", "arbitrary"), vmem_limit_bytes=VMEM_LIMIT),
        name="na_attn",
    )(proj, proj, proj, proj_meta_keys, proj_meta_keys, bias)


def _ret_kernel(q_ref, k_ref, v_ref, g_ref, km_ref, vm_ref, lgf_ref, lgb_ref, gn_ref, o_ref,
                ds_ref, sf_ref, sb_ref, *, n_chunks, group):
    lgf = lgf_ref[0]
    lgb = lgb_ref[0]
    ii = lax.broadcasted_iota(jnp.int32, (CHUNK, CHUNK), 0).astype(F32)
    jj = lax.broadcasted_iota(jnp.int32, (CHUNK, CHUNK), 1).astype(F32)
    diff = ii - jj
    dmat = jnp.where(diff >= 0, jnp.exp(jnp.maximum(diff, 0.0) * lgf), jnp.exp(jnp.maximum(-diff, 0.0) * lgb))
    xi_f = jnp.exp((ii + 1.0) * lgf)
    zeta_f = jnp.exp((CHUNK - 1.0 - ii) * lgf)
    cd_f = jnp.exp(CHUNK * lgf)
    xi_b = jnp.exp((CHUNK - ii) * lgb)
    zeta_b = jnp.exp(ii * lgb)
    cd_b = jnp.exp(CHUNK * lgb)
    nt = (((1,), (1,)), ((), ()))
    tn_dims = (((0,), (0,)), ((), ()))
    gn = gn_ref[...]

    def chunk_slice(ch):
        return pl.ds(pl.multiple_of(ch * CHUNK, CHUNK), CHUNK)

    def increments(k, v):
        kf = k.astype(F32)
        kz = jnp.concatenate([(kf * zeta_f).astype(BF16), (kf * zeta_b).astype(BF16)], axis=1)
        return lax.dot_general(kz, v, tn_dims, preferred_element_type=F32)

    def stage1(gi, c):
        for u in range(group):
            ch = gi * group + u
            sl = chunk_slice(ch)
            ds_ref[ch] = increments(k_ref[sl, :], v_ref[sl, :])
        return c

    lax.fori_loop(0, n_chunks // group, stage1, 0)

    def scan_f(ch, S):
        sf_ref[ch] = S.astype(BF16)
        return S * cd_f + ds_ref[ch, :CHUNK, :]

    def scan_b(t, S):
        ch = n_chunks - 1 - t
        sb_ref[ch] = S.astype(BF16)
        return S * cd_b + ds_ref[ch, CHUNK:, :]

    S0 = increments(km_ref[...], vm_ref[...])[:CHUNK]
    lax.fori_loop(0, n_chunks, scan_f, S0, unroll=4)
    lax.fori_loop(0, n_chunks, scan_b, jnp.zeros((CHUNK, CHUNK), F32), unroll=4)

    def scores(ch):
        sl = chunk_slice(ch)
        q = q_ref[sl, :]
        s = lax.dot_general(q, k_ref[sl, :], nt, preferred_element_type=F32) * dmat
        qf = q.astype(F32)
        return jnp.concatenate([s.astype(BF16), (qf * xi_f).astype(BF16), (qf * xi_b).astype(BF16)], axis=1)

    def mix(ch, lhs):
        rhs = jnp.concatenate([v_ref[chunk_slice(ch), :], sf_ref[ch], sb_ref[ch]], axis=0)
        return jnp.dot(lhs, rhs, preferred_element_type=F32)

    def finish(ch, o):
        sl = chunk_slice(ch)
        mu = jnp.mean(o, axis=-1, keepdims=True)
        oc = o - mu
        var = jnp.mean(oc * oc, axis=-1, keepdims=True)
        on = oc * lax.rsqrt(var + LN_EPS) * gn
        g = g_ref[sl, :].astype(F32)
        o_ref[sl, :] = (g / (1.0 + jnp.exp(-g)) * on).astype(BF16)

    def stage3(gi, c):
        chs = [gi * group + u for u in range(group)]
        lhs = [scores(ch) for ch in chs]
        outs = [mix(ch, l) for ch, l in zip(chs, lhs)]
        for ch, o in zip(chs, outs):
            finish(ch, o)
        return c

    lax.fori_loop(0, n_chunks // group, stage3, 0)


def _retention(proj, proj_meta_pad, lgf, lgb, gn_g, B, n):
    assert n % CHUNK == 0
    n_chunks = n // CHUNK
    group = math.gcd(n_chunks, RET_GROUP_CHUNKS)
    seq_spec = lambda c0: pl.BlockSpec((n, LANES), lambda b, h: (b, c0 // LANES + h))
    meta_spec = lambda c0: pl.BlockSpec((CHUNK, LANES), lambda b, h: (0, c0 // LANES + h))
    lg_spec = pl.BlockSpec((1, 1, LANES), lambda b, h: (h, 0, 0))
    return pl.pallas_call(
        functools.partial(_ret_kernel, n_chunks=n_chunks, group=group),
        grid=(B, RET_HEADS),
        in_specs=[
            seq_spec(COL_QR), seq_spec(COL_KR), seq_spec(COL_VR), seq_spec(COL_GR),
            meta_spec(COL_KR), meta_spec(COL_VR),
            lg_spec, lg_spec,
            pl.BlockSpec((1, LANES), lambda b, h: (0, h)),
        ],
        out_specs=pl.BlockSpec((n, LANES), lambda b, h: (b, h)),
        out_shape=jax.ShapeDtypeStruct((B * n, RET_WIDTH), BF16),
        scratch_shapes=[pltpu.VMEM((n_chunks, 2 * CHUNK, LANES), F32),
                        pltpu.VMEM((n_chunks, CHUNK, LANES), BF16),
                        pltpu.VMEM((n_chunks, CHUNK, LANES), BF16)],
        compiler_params=pltpu.CompilerParams(
            dimension_semantics=("arbitrary", "arbitrary"), vmem_limit_bytes=VMEM_LIMIT),
        name="retention",
    )(proj, proj, proj, proj, proj_meta_pad, proj_meta_pad, lgf, lgb, gn_g)


def _outproj_kernel(ona_ref, oret_ref, x_ref, w_ref, gin_ref, bin_ref, g1_ref, b1_ref, o_ref, *, row_blocks):
    def norm_store(rows, mix):
        h = _layer_norm_rows(x_ref[rows, :], gin_ref[...], bin_ref[...])
        o_ref[rows, :] = _layer_norm_rows(ALPHA * h + mix, g1_ref[...], b1_ref[...])

    pending = None
    start = 0
    for size in row_blocks:
        rows = slice(start, start + size)
        start += size
        lhs = jnp.concatenate([ona_ref[rows, :], oret_ref[rows, :]], axis=1)
        mix = jnp.dot(lhs, w_ref[...], preferred_element_type=F32)
        if pending is not None:
            norm_store(*pending)
        pending = (rows, mix)
    norm_store(*pending)


def _outproj(o_na, o_ret, x2d, w_bf, gin, bin_, g1, b1, tm=512):
    T = x2d.shape[0]
    assert T % tm == 0 and tm % 4 == 0
    row_blocks = (tm // 2, tm // 4, tm // 4)
    row_vec = pl.BlockSpec((1, D_MODEL), lambda i: (0, 0))
    return pl.pallas_call(
        functools.partial(_outproj_kernel, row_blocks=row_blocks),
        grid=(T // tm,),
        in_specs=[
            pl.BlockSpec((tm, NA_WIDTH), lambda i: (i, 0)),
            pl.BlockSpec((tm, RET_WIDTH), lambda i: (i, 0)),
            pl.BlockSpec((tm, D_MODEL), lambda i: (i, 0)),
            pl.BlockSpec((D_MODEL, D_MODEL), lambda i: (0, 0), pipeline_mode=pl.Buffered(1)),
            row_vec, row_vec, row_vec, row_vec,
        ],
        out_specs=pl.BlockSpec((tm, D_MODEL), lambda i: (i, 0)),
        out_shape=jax.ShapeDtypeStruct((T, D_MODEL), F32),
        compiler_params=pltpu.CompilerParams(dimension_semantics=("arbitrary",), vmem_limit_bytes=VMEM_LIMIT),
        name="outproj_ln1",
    )(o_na, o_ret, x2d, w_bf, gin, bin_, g1, b1)


def _ffn_kernel(h_ref, wg_hbm, wu_hbm, wd_hbm, g2_ref, b2_ref, o_ref,
                hb_ref, acc_even, acc_odd, wg_buf, wu_buf, wd_buf, sem, *, rb, tf):
    i = pl.program_id(0)
    n_blocks = pl.num_programs(0) - 1
    tm = h_ref.shape[0]
    n_tiles = D_FF // tf

    def tile_copies(t, slot):
        cols = pl.ds(pl.multiple_of(t * tf, tf), tf)
        return (pltpu.make_async_copy(wg_hbm.at[:, cols], wg_buf.at[slot], sem.at[slot, 0]),
                pltpu.make_async_copy(wu_hbm.at[:, cols], wu_buf.at[slot], sem.at[slot, 1]),
                pltpu.make_async_copy(wd_hbm.at[cols, :], wd_buf.at[slot], sem.at[slot, 2]))

    def start(t, slot):
        for copy in tile_copies(t, slot):
            copy.start()

    def wait(t, slot):
        for copy in tile_copies(t, slot):
            copy.wait()

    def normalise(acc_ref):
        def rows(r, c):
            sl = pl.ds(pl.multiple_of(r * rb, rb), rb)
            o_ref[sl, :] = _layer_norm_rows(acc_ref[sl, :], g2_ref[...], b2_ref[...])
            return c

        lax.fori_loop(0, tm // rb, rows, 0, unroll=True)

    def accumulate(t, slot, acc_ref):
        hb = hb_ref[...]
        g = jnp.dot(hb, wg_buf[slot], preferred_element_type=F32)
        u = jnp.dot(hb, wu_buf[slot], preferred_element_type=F32)
        a = (g / (1.0 + jnp.exp(-g)) * u).astype(BF16)
        acc_ref[...] += jnp.dot(a, wd_buf[slot], preferred_element_type=F32)

    def block(acc_ref, prev_ref):
        first_slot = (i * n_tiles) % 2
        h = h_ref[...]
        hb_ref[...] = h.astype(BF16)
        acc_ref[...] = ALPHA * h

        wait(0, first_slot)
        start(1, 1 - first_slot)
        accumulate(0, first_slot, acc_ref)
        normalise(prev_ref)

        def tile(t, c):
            slot = (first_slot + t) % 2
            wait(t, slot)

            @pl.when(t + 1 < n_tiles)
            def _():
                start(t + 1, 1 - slot)

            @pl.when(jnp.logical_and(t + 1 == n_tiles, i + 1 < n_blocks))
            def _():
                start(0, 1 - slot)

            accumulate(t, slot, acc_ref)
            return c

        lax.fori_loop(1, n_tiles, tile, 0)

    @pl.when(i == 0)
    def _():
        start(0, 0)
        acc_odd[...] = jnp.zeros_like(acc_odd)

    @pl.when(jnp.logical_and(i < n_blocks, i % 2 == 0))
    def _():
        block(acc_even, acc_odd)

    @pl.when(jnp.logical_and(i < n_blocks, i % 2 == 1))
    def _():
        block(acc_odd, acc_even)

    @pl.when(jnp.logical_and(i == n_blocks, n_blocks % 2 == 0))
    def _():
        normalise(acc_odd)

    @pl.when(jnp.logical_and(i == n_blocks, n_blocks % 2 == 1))
    def _():
        normalise(acc_even)


def _ffn(h1, wg_bf, wu_bf, wd_bf, g2, b2, tm=512, tf=512):
    T = h1.shape[0]
    assert T % tm == 0 and D_FF % tf == 0 and D_FF // tf >= 2
    n_blocks = T // tm
    row_vec = pl.BlockSpec((1, D_MODEL), lambda i: (0, 0))
    hbm = pl.BlockSpec(memory_space=pl.ANY)
    return pl.pallas_call(
        functools.partial(_ffn_kernel, rb=128, tf=tf),
        grid=(n_blocks + 1,),
        in_specs=[pl.BlockSpec((tm, D_MODEL), lambda i: (jnp.minimum(i, n_blocks - 1), 0)),
                  hbm, hbm, hbm, row_vec, row_vec],
        out_specs=pl.BlockSpec((tm, D_MODEL), lambda i: (jnp.maximum(i - 1, 0), 0)),
        out_shape=jax.ShapeDtypeStruct((T, D_MODEL), F32),
        scratch_shapes=[pltpu.VMEM((tm, D_MODEL), BF16),
                        pltpu.VMEM((tm, D_MODEL), F32),
                        pltpu.VMEM((tm, D_MODEL), F32),
                        pltpu.VMEM((2, D_MODEL, tf), BF16),
                        pltpu.VMEM((2, D_MODEL, tf), BF16),
                        pltpu.VMEM((2, tf, D_MODEL), BF16),
                        pltpu.SemaphoreType.DMA((2, 3))],
        compiler_params=pltpu.CompilerParams(dimension_semantics=("arbitrary",), vmem_limit_bytes=VMEM_LIMIT),
        name="ffn_ln2",
    )(h1, wg_bf, wu_bf, wd_bf, g2, b2)


def _rope_tables(pos):
    half = RET_HEAD_DIM // 2
    inv = ROPE_BASE ** (-jnp.arange(half, dtype=F32) / half)
    ang = pos[:, None] * inv[None, :]
    c, s = jnp.cos(ang), jnp.sin(ang)
    cq = jnp.concatenate([c, c], axis=-1)
    sq = jnp.concatenate([-s, s], axis=-1)
    ks = RET_HEAD_DIM ** -0.5
    return (jnp.stack([cq, cq * ks]), jnp.stack([sq, sq * ks]))


def _pick_tm(n, cap):
    tm = cap
    while n % tm:
        tm //= 2
    return tm


def kernel(x_prompt, x_sample, meta_tokens, ln_in_g, ln_in_b, w_in, na_rpb, ret_decay_f, ret_decay_b,
           ret_gn_g, w_out, ln1_g, ln1_b, w_ffn_gate, w_ffn_up, w_ffn_down, ln2_g, ln2_b):
    row = lambda v: v.reshape(1, -1).astype(F32)
    gin, bin_ = row(ln_in_g), row(ln_in_b)
    w_in_bf = w_in[0].astype(BF16)
    w_out_bf = w_out[0].astype(BF16)
    wg_bf, wu_bf, wd_bf = w_ffn_gate[0].astype(BF16), w_ffn_up[0].astype(BF16), w_ffn_down[0].astype(BF16)
    bias = _na_bias(na_rpb[0])
    lg = lambda d: jnp.broadcast_to(jax.nn.log_sigmoid(d[0].astype(F32))[:, None, None], (RET_HEADS, 1, LANES))
    lgf, lgb = lg(ret_decay_f), lg(ret_decay_b)
    gn_g = row(ret_gn_g[0])

    meta_tabs = _rope_tables(jnp.arange(N_META, dtype=F32))
    proj_meta = _inproj(meta_tokens.astype(F32), gin, bin_, w_in_bf, meta_tabs, N_META, N_META)
    proj_meta_pad = jnp.pad(proj_meta, ((CHUNK - N_META, 0), (0, 0)))

    def group(x):
        B, n, _ = x.shape
        x2d = x.reshape(B * n, D_MODEL)
        tabs = _rope_tables(jnp.arange(n, dtype=F32) + float(N_META))
        proj = _inproj(x2d, gin, bin_, w_in_bf, tabs, n, _pick_tm(n, 1024))
        o_na = _na(proj, _na_meta(proj_meta), bias, B, n)
        o_ret = _retention(proj, proj_meta_pad, lgf, lgb, gn_g, B, n)
        h1 = _outproj(o_na, o_ret, x2d, w_out_bf, gin, bin_, row(ln1_g[0]), row(ln1_b[0]), tm=_pick_tm(B * n, 512))
        y = _ffn(h1, wg_bf, wu_bf, wd_bf, row(ln2_g[0]), row(ln2_b[0]), tm=_pick_tm(B * n, 512))
        return y.reshape(B, n, D_MODEL)

    return (group(x_prompt), group(x_sample))


def _na_meta(proj_meta):
    return jnp.pad(proj_meta, ((0, NA_META_SLOTS - N_META), (0, 0)))
```

```python
import functools
import math

import jax
import jax.numpy as jnp
from jax import lax
from jax.experimental import pallas as pl
from jax.experimental.pallas import tpu as pltpu

D_MODEL = 2048
N_META = 16
GRID_W = 64
NA_HEADS = 16
NA_HEAD_DIM = 64
NA_WIDTH = NA_HEADS * NA_HEAD_DIM
NA_WIN_H = 8
NA_WIN_W = 16
NA_HALF = GRID_W // 2
NA_KEY_COLS = 48
NA_KEY_SHIFT = GRID_W - NA_KEY_COLS
NA_UNION = NA_WIN_H + 1
NA_KEYS = 512
NA_META_SLOTS = NA_KEYS - NA_UNION * NA_KEY_COLS
RET_HEADS = 8
RET_HEAD_DIM = 128
RET_WIDTH = RET_HEADS * RET_HEAD_DIM
CHUNK = 128
ROPE_BASE = 10000.0
IN_WIDTH = 3 * NA_WIDTH + 4 * RET_WIDTH
D_FF = 5632
LN_EPS = 1e-5
ALPHA = 2.0 ** 0.25

COL_QA, COL_KA, COL_VA = 0, NA_WIDTH, 2 * NA_WIDTH
COL_QR = 3 * NA_WIDTH
COL_KR = COL_QR + RET_WIDTH
COL_VR = COL_KR + RET_WIDTH
COL_GR = COL_VR + RET_WIDTH

NA_GROUP_ROWS = 32
RET_GROUP_CHUNKS = 32

LANES = 128
NEG = -1e30
VMEM_LIMIT = 56 * 1024 * 1024

F32 = jnp.float32
BF16 = jnp.bfloat16


def _layer_norm_rows(x, g, b):
    mu = jnp.mean(x, axis=-1, keepdims=True)
    xc = x - mu
    var = jnp.mean(xc * xc, axis=-1, keepdims=True)
    return xc * lax.rsqrt(var + LN_EPS) * g + b


def _inproj_kernel(x_ref, g_ref, b_ref, w_ref, c_ref, s_ref, o_ref, hb_ref, *, tn, rb):
    j = pl.program_id(1)
    tm = x_ref.shape[0]

    def store(acc, rows, tile):
        for t in range(tn // LANES):
            col = tile * tn + t * LANES
            xs = acc[:, t * LANES:(t + 1) * LANES]
            if COL_QR <= col < COL_VR:
                which = 0 if col < COL_KR else 1
                xs = xs * c_ref[which, rows, :] + pltpu.roll(xs, LANES // 2, 1) * s_ref[which, rows, :]
            o_ref[rows, t * LANES:(t + 1) * LANES] = xs.astype(BF16)

    @pl.when(j == 0)
    def _():
        for sb in range(tm // rb):
            rows = slice(sb * rb, (sb + 1) * rb)
            hb_ref[rows, :] = _layer_norm_rows(x_ref[rows, :], g_ref[...], b_ref[...]).astype(BF16)
            store(jnp.dot(hb_ref[rows, :], w_ref[...], preferred_element_type=F32), rows, 0)

    for tile in range(1, IN_WIDTH // tn):
        @pl.when(j == tile)
        def _(tile=tile):
            store(jnp.dot(hb_ref[...], w_ref[...], preferred_element_type=F32), slice(None), tile)


def _inproj(x2d, ln_g, ln_b, w_bf, tabs, n_seq, tm, tn=1792):
    T = x2d.shape[0]
    assert T % tm == 0 and n_seq % tm == 0 and IN_WIDTH % tn == 0 and tn % LANES == 0
    blocks_per_seq = n_seq // tm
    tab_spec = pl.BlockSpec((2, tm, LANES), lambda i, j: (0, i % blocks_per_seq, 0))
    rb = min(tm, 256)
    return pl.pallas_call(
        functools.partial(_inproj_kernel, tn=tn, rb=rb),
        grid=(T // tm, IN_WIDTH // tn),
        in_specs=[
            pl.BlockSpec((tm, D_MODEL), lambda i, j: (i, 0)),
            pl.BlockSpec((1, D_MODEL), lambda i, j: (0, 0)),
            pl.BlockSpec((1, D_MODEL), lambda i, j: (0, 0)),
            pl.BlockSpec((D_MODEL, tn), lambda i, j: (0, j)),
            tab_spec, tab_spec,
        ],
        out_specs=pl.BlockSpec((tm, tn), lambda i, j: (i, j)),
        out_shape=jax.ShapeDtypeStruct((T, IN_WIDTH), BF16),
        scratch_shapes=[pltpu.VMEM((tm, D_MODEL), BF16)],
        compiler_params=pltpu.CompilerParams(
            dimension_semantics=("arbitrary", "arbitrary"), vmem_limit_bytes=VMEM_LIMIT),
        name="inproj",
    )(x2d, ln_g, ln_b, w_bf, *tabs)


def _na_kernel(q_ref, k_ref, v_ref, km_ref, vm_ref, bias_ref, o_ref, *, rows, unroll):
    lane = lax.broadcasted_iota(jnp.int32, (1, LANES), 1)
    scale = NA_HEAD_DIM ** -0.5
    qmask = [jnp.where((lane // NA_HEAD_DIM) == hh, scale, 0.0).astype(BF16) for hh in range(2)]
    km = km_ref[...]
    vm = vm_ref[...]
    nt = (((1,), (1,)), ((), ()))

    def window(ref, meta, b0, hf):
        parts = [ref[pl.ds(pl.multiple_of((b0 + i) * GRID_W + NA_KEY_SHIFT * hf, NA_KEY_SHIFT), NA_KEY_COLS), :]
                 for i in range(NA_UNION)]
        return jnp.concatenate(parts + [meta], axis=0)

    units = [(u, hf) for u in range(unroll // 2) for hf in range(2)]

    def locate(g, idx):
        u, hf = units[idx]
        r0 = g * unroll + 2 * u
        b0 = jnp.clip(r0 - NA_WIN_H // 2, 0, rows - NA_UNION)
        qsl = [pl.ds(pl.multiple_of((r0 + t) * GRID_W + NA_HALF * hf, NA_HALF), NA_HALF) for t in range(2)]
        return r0 - b0, qsl, b0, hf

    def scores(g, idx):
        e0, qsl, b0, hf = locate(g, idx)
        q2 = jnp.concatenate([q_ref[sl, :] * qmask[hh] for sl in qsl for hh in range(2)], axis=0)
        s = lax.dot_general(q2, window(k_ref, km, b0, hf), nt, preferred_element_type=F32)
        bias = jnp.concatenate([bias_ref[0, hf, pl.ds(e0 + t, 1)][0] for t in range(2)], axis=0)
        return s + bias

    def finish(s, g, idx):
        _, qsl, b0, hf = locate(g, idx)
        m = jnp.max(s, axis=-1, keepdims=True)
        p = jnp.exp(s - m)
        den = jnp.sum(p, axis=-1, keepdims=True)
        o2 = jnp.dot(p.astype(BF16), window(v_ref, vm, b0, hf), preferred_element_type=F32) / den
        for t in range(2):
            blk = o2[2 * NA_HALF * t:2 * NA_HALF * (t + 1)]
            o_ref[qsl[t], :] = jnp.where(lane < NA_HEAD_DIM, blk[:NA_HALF], blk[NA_HALF:]).astype(BF16)

    def body(g, c):
        staged = [scores(g, idx) for idx in range(len(units))]
        for idx in range(len(units)):
            finish(staged[idx], g, idx)
        return c

    lax.fori_loop(0, rows // unroll, body, 0)


def _na_bias_kernel(rpb_ref, o_ref):
    c32 = lax.broadcasted_iota(jnp.int32, (NA_HALF, LANES), 0)
    lane = lax.broadcasted_iota(jnp.int32, (NA_HALF, LANES), 1)
    masked = jnp.where(lane + c32 < 0, 0.0, NEG).astype(F32)[:, :NA_KEY_COLS]
    meta = jnp.where(lane + LANES * c32 < N_META + LANES * c32, 0.0, NEG).astype(F32)[:, :NA_META_SLOTS]
    tiles = {}
    for hf in range(2):
        c = c32 + NA_HALF * hf
        kc = lane + NA_KEY_SHIFT * hf
        cs = jnp.clip(c - NA_WIN_W // 2, 0, GRID_W - NA_WIN_W)
        valid = (kc >= cs) & (kc < cs + NA_WIN_W) & (lane < NA_KEY_COLS)
        shift = (LANES - (NA_WIN_W - 1) - NA_KEY_SHIFT * hf + NA_HALF * hf) % LANES
        for h in range(2):
            for dr in range(2 * NA_WIN_H - 1):
                w = jnp.broadcast_to(rpb_ref[h, dr:dr + 1, :], (NA_HALF, LANES))
                tiles[(hf, h, dr)] = jnp.where(valid, pltpu.roll(w, shift, 1, stride=1, stride_axis=0), NEG)
    for hf in range(2):
        for e in range(NA_UNION):
            d = 1 if e > NA_WIN_H // 2 else 0
            heads = []
            for h in range(2):
                parts = [tiles[(hf, h, i - e + NA_WIN_H - 1)][:, :NA_KEY_COLS] if 0 <= i - d < NA_WIN_H else masked
                         for i in range(NA_UNION)]
                heads.append(jnp.concatenate(parts + [meta], axis=1))
            o_ref[0, hf, e] = jnp.concatenate(heads, axis=0)


def _na_bias(rpb):
    n_dr = 2 * NA_WIN_H - 1
    rpb_lanes = jnp.pad(rpb.astype(F32), ((0, 0), (0, 0), (0, LANES - rpb.shape[-1])))
    shape = (NA_HEADS // 2, 2, NA_UNION, 2 * NA_HALF, NA_KEYS)
    return pl.pallas_call(
        _na_bias_kernel,
        grid=(NA_HEADS // 2,),
        in_specs=[pl.BlockSpec((2, n_dr, LANES), lambda p: (p, 0, 0))],
        out_specs=pl.BlockSpec((1,) + shape[1:], lambda p: (p, 0, 0, 0, 0)),
        out_shape=jax.ShapeDtypeStruct(shape, F32),
        compiler_params=pltpu.CompilerParams(dimension_semantics=("arbitrary",), vmem_limit_bytes=VMEM_LIMIT),
        name="na_bias",
    )(rpb_lanes)


def _na(proj, proj_meta_keys, bias, B, n):
    rows = n // GRID_W
    unroll = math.gcd(rows, NA_GROUP_ROWS)
    assert rows >= NA_UNION and unroll % 2 == 0
    pairs = NA_WIDTH // LANES
    seq_spec = lambda c0: pl.BlockSpec((n, LANES), lambda hp, b: (b, c0 // LANES + hp))
    meta_spec = lambda c0: pl.BlockSpec((NA_META_SLOTS, LANES), lambda hp, b: (0, c0 // LANES + hp))
    return pl.pallas_call(
        functools.partial(_na_kernel, rows=rows, unroll=unroll),
        grid=(pairs, B),
        in_specs=[
            seq_spec(COL_QA), seq_spec(COL_KA), seq_spec(COL_VA),
            meta_spec(COL_KA), meta_spec(COL_VA),
            pl.BlockSpec((1, 2, NA_UNION, 2 * NA_HALF, NA_KEYS), lambda hp, b: (hp, 0, 0, 0, 0)),
        ],
        out_specs=pl.BlockSpec((n, LANES), lambda hp, b: (b, hp)),
        out_shape=jax.ShapeDtypeStruct((B * n, NA_WIDTH), BF16),
        compiler_params=pltpu.CompilerParams(
            dimension_semantics=("arbitrary", "arbitrary"), vmem_limit_bytes=VMEM_LIMIT),
        name="na_attn",
    )(proj, proj, proj, proj_meta_keys, proj_meta_keys, bias)


def _ret_kernel(q_ref, k_ref, v_ref, g_ref, km_ref, vm_ref, lgf_ref, lgb_ref, gn_ref, o_ref,
                ds_ref, sf_ref, sb_ref, *, n_chunks, group):
    lgf = lgf_ref[0]
    lgb = lgb_ref[0]
    ii = lax.broadcasted_iota(jnp.int32, (CHUNK, CHUNK), 0).astype(F32)
    jj = lax.broadcasted_iota(jnp.int32, (CHUNK, CHUNK), 1).astype(F32)
    diff = ii - jj
    dmat = jnp.where(diff >= 0, jnp.exp(jnp.maximum(diff, 0.0) * lgf), jnp.exp(jnp.maximum(-diff, 0.0) * lgb))
    xi_f = jnp.exp((ii + 1.0) * lgf)
    zeta_f = jnp.exp((CHUNK - 1.0 - ii) * lgf)
    cd_f = jnp.exp(CHUNK * lgf)
    xi_b = jnp.exp((CHUNK - ii) * lgb)
    zeta_b = jnp.exp(ii * lgb)
    cd_b = jnp.exp(CHUNK * lgb)
    nt = (((1,), (1,)), ((), ()))
    tn_dims = (((0,), (0,)), ((), ()))
    gn = gn_ref[...]

    def chunk_slice(ch):
        return pl.ds(pl.multiple_of(ch * CHUNK, CHUNK), CHUNK)

    def increments(k, v):
        kf = k.astype(F32)
        kz = jnp.concatenate([(kf * zeta_f).astype(BF16), (kf * zeta_b).astype(BF16)], axis=1)
        return lax.dot_general(kz, v, tn_dims, preferred_element_type=F32)

    def stage1(gi, c):
        for u in range(group):
            ch = gi * group + u
            sl = chunk_slice(ch)
            ds_ref[ch] = increments(k_ref[sl, :], v_ref[sl, :])
        return c

    lax.fori_loop(0, n_chunks // group, stage1, 0)

    def scan_f(ch, S):
        sf_ref[ch] = S.astype(BF16)
        return S * cd_f + ds_ref[ch, :CHUNK, :]

    def scan_b(t, S):
        ch = n_chunks - 1 - t
        sb_ref[ch] = S.astype(BF16)
        return S * cd_b + ds_ref[ch, CHUNK:, :]

    S0 = increments(km_ref[...], vm_ref[...])[:CHUNK]
    lax.fori_loop(0, n_chunks, scan_f, S0, unroll=4)
    lax.fori_loop(0, n_chunks, scan_b, jnp.zeros((CHUNK, CHUNK), F32), unroll=4)

    def scores(ch):
        sl = chunk_slice(ch)
        q = q_ref[sl, :]
        s = lax.dot_general(q, k_ref[sl, :], nt, preferred_element_type=F32) * dmat
        qf = q.astype(F32)
        return jnp.concatenate([s.astype(BF16), (qf * xi_f).astype(BF16), (qf * xi_b).astype(BF16)], axis=1)

    def mix(ch, lhs):
        rhs = jnp.concatenate([v_ref[chunk_slice(ch), :], sf_ref[ch], sb_ref[ch]], axis=0)
        return jnp.dot(lhs, rhs, preferred_element_type=F32)

    def finish(ch, o):
        sl = chunk_slice(ch)
        mu = jnp.mean(o, axis=-1, keepdims=True)
        oc = o - mu
        var = jnp.mean(oc * oc, axis=-1, keepdims=True)
        on = oc * lax.rsqrt(var + LN_EPS) * gn
        g = g_ref[sl, :].astype(F32)
        o_ref[sl, :] = (g / (1.0 + jnp.exp(-g)) * on).astype(BF16)

    def stage3(gi, c):
        chs = [gi * group + u for u in range(group)]
        lhs = [scores(ch) for ch in chs]
        outs = [mix(ch, l) for ch, l in zip(chs, lhs)]
        for ch, o in zip(chs, outs):
            finish(ch, o)
        return c

    lax.fori_loop(0, n_chunks // group, stage3, 0)


def _retention(proj, proj_meta_pad, lgf, lgb, gn_g, B, n):
    assert n % CHUNK == 0
    n_chunks = n // CHUNK
    group = math.gcd(n_chunks, RET_GROUP_CHUNKS)
    seq_spec = lambda c0: pl.BlockSpec((n, LANES), lambda b, h: (b, c0 // LANES + h))
    meta_spec = lambda c0: pl.BlockSpec((CHUNK, LANES), lambda b, h: (0, c0 // LANES + h))
    lg_spec = pl.BlockSpec((1, 1, LANES), lambda b, h: (h, 0, 0))
    return pl.pallas_call(
        functools.partial(_ret_kernel, n_chunks=n_chunks, group=group),
        grid=(B, RET_HEADS),
        in_specs=[
            seq_spec(COL_QR), seq_spec(COL_KR), seq_spec(COL_VR), seq_spec(COL_GR),
            meta_spec(COL_KR), meta_spec(COL_VR),
            lg_spec, lg_spec,
            pl.BlockSpec((1, LANES), lambda b, h: (0, h)),
        ],
        out_specs=pl.BlockSpec((n, LANES), lambda b, h: (b, h)),
        out_shape=jax.ShapeDtypeStruct((B * n, RET_WIDTH), BF16),
        scratch_shapes=[pltpu.VMEM((n_chunks, 2 * CHUNK, LANES), F32),
                        pltpu.VMEM((n_chunks, CHUNK, LANES), BF16),
                        pltpu.VMEM((n_chunks, CHUNK, LANES), BF16)],
        compiler_params=pltpu.CompilerParams(
            dimension_semantics=("arbitrary", "arbitrary"), vmem_limit_bytes=VMEM_LIMIT),
        name="retention",
    )(proj, proj, proj, proj, proj_meta_pad, proj_meta_pad, lgf, lgb, gn_g)


def _outproj_kernel(ona_ref, oret_ref, x_ref, w_ref, gin_ref, bin_ref, g1_ref, b1_ref, o_ref, *, row_blocks):
    def norm_store(rows, mix):
        h = _layer_norm_rows(x_ref[rows, :], gin_ref[...], bin_ref[...])
        o_ref[rows, :] = _layer_norm_rows(ALPHA * h + mix, g1_ref[...], b1_ref[...])

    pending = None
    start = 0
    for size in row_blocks:
        rows = slice(start, start + size)
        start += size
        lhs = jnp.concatenate([ona_ref[rows, :], oret_ref[rows, :]], axis=1)
        mix = jnp.dot(lhs, w_ref[...], preferred_element_type=F32)
        if pending is not None:
            norm_store(*pending)
        pending = (rows, mix)
    norm_store(*pending)


def _outproj(o_na, o_ret, x2d, w_bf, gin, bin_, g1, b1, tm=512):
    T = x2d.shape[0]
    assert T % tm == 0 and tm % 4 == 0
    row_blocks = (tm // 2, tm // 4, tm // 4)
    row_vec = pl.BlockSpec((1, D_MODEL), lambda i: (0, 0))
    return pl.pallas_call(
        functools.partial(_outproj_kernel, row_blocks=row_blocks),
        grid=(T // tm,),
        in_specs=[
            pl.BlockSpec((tm, NA_WIDTH), lambda i: (i, 0)),
            pl.BlockSpec((tm, RET_WIDTH), lambda i: (i, 0)),
            pl.BlockSpec((tm, D_MODEL), lambda i: (i, 0)),
            pl.BlockSpec((D_MODEL, D_MODEL), lambda i: (0, 0), pipeline_mode=pl.Buffered(1)),
            row_vec, row_vec, row_vec, row_vec,
        ],
        out_specs=pl.BlockSpec((tm, D_MODEL), lambda i: (i, 0)),
        out_shape=jax.ShapeDtypeStruct((T, D_MODEL), F32),
        compiler_params=pltpu.CompilerParams(dimension_semantics=("arbitrary",), vmem_limit_bytes=VMEM_LIMIT),
        name="outproj_ln1",
    )(o_na, o_ret, x2d, w_bf, gin, bin_, g1, b1)


def _ffn_kernel(h_ref, wg_ref, wu_ref, wd_ref, g2_ref, b2_ref, o_ref, hb_ref, *, rb):
    j = pl.program_id(1)
    tm = h_ref.shape[0]

    @pl.when(j == 0)
    def _():
        hb_ref[...] = h_ref[...].astype(BF16)
        o_ref[...] = jnp.zeros_like(o_ref)

    hb = hb_ref[...]
    g = jnp.dot(hb, wg_ref[...], preferred_element_type=F32)
    u = jnp.dot(hb, wu_ref[...], preferred_element_type=F32)
    a = (g / (1.0 + jnp.exp(-g)) * u).astype(BF16)
    o_ref[...] += jnp.dot(a, wd_ref[...], preferred_element_type=F32)

    @pl.when(j == pl.num_programs(1) - 1)
    def _():
        def rows(i, c):
            sl = pl.ds(pl.multiple_of(i * rb, rb), rb)
            o_ref[sl, :] = _layer_norm_rows(ALPHA * h_ref[sl, :] + o_ref[sl, :], g2_ref[...], b2_ref[...])
            return c

        lax.fori_loop(0, tm // rb, rows, 0)


def _ffn(h1, wg_bf, wu_bf, wd_bf, g2, b2, tm=512, tf=512):
    T = h1.shape[0]
    assert T % tm == 0 and D_FF % tf == 0
    row_vec = pl.BlockSpec((1, D_MODEL), lambda i, j: (0, 0))
    return pl.pallas_call(
        functools.partial(_ffn_kernel, rb=128),
        grid=(T // tm, D_FF // tf),
        in_specs=[
            pl.BlockSpec((tm, D_MODEL), lambda i, j: (i, 0)),
            pl.BlockSpec((D_MODEL, tf), lambda i, j: (0, j)),
            pl.BlockSpec((D_MODEL, tf), lambda i, j: (0, j)),
            pl.BlockSpec((tf, D_MODEL), lambda i, j: (j, 0)),
            row_vec, row_vec,
        ],
        out_specs=pl.BlockSpec((tm, D_MODEL), lambda i, j: (i, 0)),
        out_shape=jax.ShapeDtypeStruct((T, D_MODEL), F32),
        scratch_shapes=[pltpu.VMEM((tm, D_MODEL), BF16)],
        compiler_params=pltpu.CompilerParams(
            dimension_semantics=("arbitrary", "arbitrary"), vmem_limit_bytes=VMEM_LIMIT),
        name="ffn_ln2",
    )(h1, wg_bf, wu_bf, wd_bf, g2, b2)


def _rope_tables(pos):
    half = RET_HEAD_DIM // 2
    inv = ROPE_BASE ** (-jnp.arange(half, dtype=F32) / half)
    ang = pos[:, None] * inv[None, :]
    c, s = jnp.cos(ang), jnp.sin(ang)
    cq = jnp.concatenate([c, c], axis=-1)
    sq = jnp.concatenate([-s, s], axis=-1)
    ks = RET_HEAD_DIM ** -0.5
    return (jnp.stack([cq, cq * ks]), jnp.stack([sq, sq * ks]))


def _pick_tm(n, cap):
    tm = cap
    while n % tm:
        tm //= 2
    return tm


def kernel(x_prompt, x_sample, meta_tokens, ln_in_g, ln_in_b, w_in, na_rpb, ret_decay_f, ret_decay_b,
           ret_gn_g, w_out, ln1_g, ln1_b, w_ffn_gate, w_ffn_up, w_ffn_down, ln2_g, ln2_b):
    row = lambda v: v.reshape(1, -1).astype(F32)
    gin, bin_ = row(ln_in_g), row(ln_in_b)
    w_in_bf = w_in[0].astype(BF16)
    w_out_bf = w_out[0].astype(BF16)
    wg_bf, wu_bf, wd_bf = w_ffn_gate[0].astype(BF16), w_ffn_up[0].astype(BF16), w_ffn_down[0].astype(BF16)
    bias = _na_bias(na_rpb[0])
    lg = lambda d: jnp.broadcast_to(jax.nn.log_sigmoid(d[0].astype(F32))[:, None, None], (RET_HEADS, 1, LANES))
    lgf, lgb = lg(ret_decay_f), lg(ret_decay_b)
    gn_g = row(ret_gn_g[0])

    meta_tabs = _rope_tables(jnp.arange(N_META, dtype=F32))
    proj_meta = _inproj(meta_tokens.astype(F32), gin, bin_, w_in_bf, meta_tabs, N_META, N_META)
    proj_meta_pad = jnp.pad(proj_meta, ((CHUNK - N_META, 0), (0, 0)))

    n_max = max(x_prompt.shape[1], x_sample.shape[1])
    all_tabs = _rope_tables(jnp.arange(n_max, dtype=F32) + float(N_META))

    def group(x):
        B, n, _ = x.shape
        x2d = x.reshape(B * n, D_MODEL)
        tabs = tuple(t[:, :n] for t in all_tabs)
        proj = _inproj(x2d, gin, bin_, w_in_bf, tabs, n, _pick_tm(n, 1024))
        o_na = _na(proj, _na_meta(proj_meta), bias, B, n)
        o_ret = _retention(proj, proj_meta_pad, lgf, lgb, gn_g, B, n)
        h1 = _outproj(o_na, o_ret, x2d, w_out_bf, gin, bin_, row(ln1_g[0]), row(ln1_b[0]), tm=_pick_tm(B * n, 512))
        y = _ffn(h1, wg_bf, wu_bf, wd_bf, row(ln2_g[0]), row(ln2_b[0]), tm=_pick_tm(B * n, 1024))
        return y.reshape(B, n, D_MODEL)

    return (group(x_prompt), group(x_sample))


def _na_meta(proj_meta):
    return jnp.pad(proj_meta, ((0, NA_META_SLOTS - N_META), (0, 0)))
```

```python
import functools
import math

import jax
import jax.numpy as jnp
from jax import lax
from jax.experimental import pallas as pl
from jax.experimental.pallas import tpu as pltpu

D_MODEL = 2048
N_META = 16
GRID_W = 64
NA_HEADS = 16
NA_HEAD_DIM = 64
NA_WIDTH = NA_HEADS * NA_HEAD_DIM
NA_WIN_H = 8
NA_WIN_W = 16
NA_HALF = GRID_W // 2
NA_KEY_COLS = 48
NA_KEY_SHIFT = GRID_W - NA_KEY_COLS
NA_UNION = NA_WIN_H + 1
NA_KEYS = 512
NA_META_SLOTS = NA_KEYS - NA_UNION * NA_KEY_COLS
RET_HEADS = 8
RET_HEAD_DIM = 128
RET_WIDTH = RET_HEADS * RET_HEAD_DIM
CHUNK = 128
ROPE_BASE = 10000.0
ROPE_STRIDE = 128
IN_WIDTH = 3 * NA_WIDTH + 4 * RET_WIDTH
D_FF = 5632
LN_EPS = 1e-5
ALPHA = 2.0 ** 0.25

COL_QA, COL_KA, COL_VA = 0, NA_WIDTH, 2 * NA_WIDTH
COL_QR = 3 * NA_WIDTH
COL_KR = COL_QR + RET_WIDTH
COL_VR = COL_KR + RET_WIDTH
COL_GR = COL_VR + RET_WIDTH

NA_GROUP_ROWS = 32
RET_GROUP_CHUNKS = 32

LANES = 128
NEG = -1e30
VMEM_LIMIT = 56 * 1024 * 1024

F32 = jnp.float32
BF16 = jnp.bfloat16


def _layer_norm_rows(x, g, b):
    mu = jnp.mean(x, axis=-1, keepdims=True)
    xc = x - mu
    var = jnp.mean(xc * xc, axis=-1, keepdims=True)
    return xc * lax.rsqrt(var + LN_EPS) * g + b


def _inproj_kernel(x_ref, g_ref, b_ref, w_ref, c_ref, s_ref, o_ref, hb_ref, *, tn, rb):
    j = pl.program_id(1)
    tm = x_ref.shape[0]

    def store(acc, rows, tile):
        for t in range(tn // LANES):
            col = tile * tn + t * LANES
            xs = acc[:, t * LANES:(t + 1) * LANES]
            if COL_QR <= col < COL_VR:
                which = 0 if col < COL_KR else 1
                xs = xs * c_ref[which, rows, :] + pltpu.roll(xs, LANES // 2, 1) * s_ref[which, rows, :]
            o_ref[rows, t * LANES:(t + 1) * LANES] = xs.astype(BF16)

    @pl.when(j == 0)
    def _():
        for sb in range(tm // rb):
            rows = slice(sb * rb, (sb + 1) * rb)
            hb_ref[rows, :] = _layer_norm_rows(x_ref[rows, :], g_ref[...], b_ref[...]).astype(BF16)
            store(jnp.dot(hb_ref[rows, :], w_ref[...], preferred_element_type=F32), rows, 0)

    for tile in range(1, IN_WIDTH // tn):
        @pl.when(j == tile)
        def _(tile=tile):
            store(jnp.dot(hb_ref[...], w_ref[...], preferred_element_type=F32), slice(None), tile)


def _inproj(x2d, ln_g, ln_b, w_bf, tabs, n_seq, tm, tn=1792):
    T = x2d.shape[0]
    assert T % tm == 0 and n_seq % tm == 0 and IN_WIDTH % tn == 0 and tn % LANES == 0
    blocks_per_seq = n_seq // tm
    tab_spec = pl.BlockSpec((2, tm, LANES), lambda i, j: (0, i % blocks_per_seq, 0))
    rb = min(tm, 256)
    return pl.pallas_call(
        functools.partial(_inproj_kernel, tn=tn, rb=rb),
        grid=(T // tm, IN_WIDTH // tn),
        in_specs=[
            pl.BlockSpec((tm, D_MODEL), lambda i, j: (i, 0)),
            pl.BlockSpec((1, D_MODEL), lambda i, j: (0, 0)),
            pl.BlockSpec((1, D_MODEL), lambda i, j: (0, 0)),
            pl.BlockSpec((D_MODEL, tn), lambda i, j: (0, j)),
            tab_spec, tab_spec,
        ],
        out_specs=pl.BlockSpec((tm, tn), lambda i, j: (i, j)),
        out_shape=jax.ShapeDtypeStruct((T, IN_WIDTH), BF16),
        scratch_shapes=[pltpu.VMEM((tm, D_MODEL), BF16)],
        compiler_params=pltpu.CompilerParams(
            dimension_semantics=("arbitrary", "arbitrary"), vmem_limit_bytes=VMEM_LIMIT),
        name="inproj",
    )(x2d, ln_g, ln_b, w_bf, *tabs)


def _na_kernel(q_ref, k_ref, v_ref, km_ref, vm_ref, bias_ref, o_ref, *, rows, unroll):
    lane = lax.broadcasted_iota(jnp.int32, (1, LANES), 1)
    scale = NA_HEAD_DIM ** -0.5
    qmask = [jnp.where((lane // NA_HEAD_DIM) == hh, scale, 0.0).astype(BF16) for hh in range(2)]
    km = km_ref[...]
    vm = vm_ref[...]
    nt = (((1,), (1,)), ((), ()))

    def window(ref, meta, b0, hf):
        parts = [ref[pl.ds(pl.multiple_of((b0 + i) * GRID_W + NA_KEY_SHIFT * hf, NA_KEY_SHIFT), NA_KEY_COLS), :]
                 for i in range(NA_UNION)]
        return jnp.concatenate(parts + [meta], axis=0)

    units = [(u, hf) for u in range(unroll // 2) for hf in range(2)]

    def locate(g, idx):
        u, hf = units[idx]
        r0 = g * unroll + 2 * u
        b0 = jnp.clip(r0 - NA_WIN_H // 2, 0, rows - NA_UNION)
        qsl = [pl.ds(pl.multiple_of((r0 + t) * GRID_W + NA_HALF * hf, NA_HALF), NA_HALF) for t in range(2)]
        return r0 - b0, qsl, b0, hf

    def scores(g, idx):
        e0, qsl, b0, hf = locate(g, idx)
        q2 = jnp.concatenate([q_ref[sl, :] * qmask[hh] for sl in qsl for hh in range(2)], axis=0)
        s = lax.dot_general(q2, window(k_ref, km, b0, hf), nt, preferred_element_type=F32)
        bias = jnp.concatenate([bias_ref[0, hf, pl.ds(e0 + t, 1)][0] for t in range(2)], axis=0)
        return s + bias

    def finish(s, g, idx):
        _, qsl, b0, hf = locate(g, idx)
        m = jnp.max(s, axis=-1, keepdims=True)
        p = jnp.exp(s - m)
        den = jnp.sum(p, axis=-1, keepdims=True)
        o2 = jnp.dot(p.astype(BF16), window(v_ref, vm, b0, hf), preferred_element_type=F32) / den
        for t in range(2):
            blk = o2[2 * NA_HALF * t:2 * NA_HALF * (t + 1)]
            o_ref[qsl[t], :] = jnp.where(lane < NA_HEAD_DIM, blk[:NA_HALF], blk[NA_HALF:]).astype(BF16)

    def body(g, c):
        staged = [scores(g, idx) for idx in range(len(units))]
        for idx in range(len(units)):
            finish(staged[idx], g, idx)
        return c

    lax.fori_loop(0, rows // unroll, body, 0)


def _na_bias_kernel(rpb_ref, o_ref):
    c32 = lax.broadcasted_iota(jnp.int32, (NA_HALF, LANES), 0)
    lane = lax.broadcasted_iota(jnp.int32, (NA_HALF, LANES), 1)
    masked = jnp.where(lane + c32 < 0, 0.0, NEG).astype(F32)[:, :NA_KEY_COLS]
    meta = jnp.where(lane + LANES * c32 < N_META + LANES * c32, 0.0, NEG).astype(F32)[:, :NA_META_SLOTS]
    tiles = {}
    for hf in range(2):
        c = c32 + NA_HALF * hf
        kc = lane + NA_KEY_SHIFT * hf
        cs = jnp.clip(c - NA_WIN_W // 2, 0, GRID_W - NA_WIN_W)
        valid = (kc >= cs) & (kc < cs + NA_WIN_W) & (lane < NA_KEY_COLS)
        shift = (LANES - (NA_WIN_W - 1) - NA_KEY_SHIFT * hf + NA_HALF * hf) % LANES
        for h in range(2):
            for dr in range(2 * NA_WIN_H - 1):
                w = jnp.broadcast_to(rpb_ref[h, dr:dr + 1, :], (NA_HALF, LANES))
                tiles[(hf, h, dr)] = jnp.where(valid, pltpu.roll(w, shift, 1, stride=1, stride_axis=0), NEG)
    for hf in range(2):
        for e in range(NA_UNION):
            d = 1 if e > NA_WIN_H // 2 else 0
            heads = []
            for h in range(2):
                parts = [tiles[(hf, h, i - e + NA_WIN_H - 1)][:, :NA_KEY_COLS] if 0 <= i - d < NA_WIN_H else masked
                         for i in range(NA_UNION)]
                heads.append(jnp.concatenate(parts + [meta], axis=1))
            o_ref[0, hf, e] = jnp.concatenate(heads, axis=0)


def _na_bias(rpb):
    n_dr = 2 * NA_WIN_H - 1
    rpb_lanes = jnp.pad(rpb.astype(F32), ((0, 0), (0, 0), (0, LANES - rpb.shape[-1])))
    shape = (NA_HEADS // 2, 2, NA_UNION, 2 * NA_HALF, NA_KEYS)
    return pl.pallas_call(
        _na_bias_kernel,
        grid=(NA_HEADS // 2,),
        in_specs=[pl.BlockSpec((2, n_dr, LANES), lambda p: (p, 0, 0))],
        out_specs=pl.BlockSpec((1,) + shape[1:], lambda p: (p, 0, 0, 0, 0)),
        out_shape=jax.ShapeDtypeStruct(shape, F32),
        compiler_params=pltpu.CompilerParams(dimension_semantics=("arbitrary",), vmem_limit_bytes=VMEM_LIMIT),
        name="na_bias",
    )(rpb_lanes)


def _na(proj, proj_meta_keys, bias, B, n):
    rows = n // GRID_W
    unroll = math.gcd(rows, NA_GROUP_ROWS)
    assert rows >= NA_UNION and unroll % 2 == 0
    pairs = NA_WIDTH // LANES
    seq_spec = lambda c0: pl.BlockSpec((n, LANES), lambda hp, b: (b, c0 // LANES + hp))
    meta_spec = lambda c0: pl.BlockSpec((NA_META_SLOTS, LANES), lambda hp, b: (0, c0 // LANES + hp))
    return pl.pallas_call(
        functools.partial(_na_kernel, rows=rows, unroll=unroll),
        grid=(pairs, B),
        in_specs=[
            seq_spec(COL_QA), seq_spec(COL_KA), seq_spec(COL_VA),
            meta_spec(COL_KA), meta_spec(COL_VA),
            pl.BlockSpec((1, 2, NA_UNION, 2 * NA_HALF, NA_KEYS), lambda hp, b: (hp, 0, 0, 0, 0)),
        ],
        out_specs=pl.BlockSpec((n, LANES), lambda hp, b: (b, hp)),
        out_shape=jax.ShapeDtypeStruct((B * n, NA_WIDTH), BF16),
        compiler_params=pltpu.CompilerParams(
            dimension_semantics=("arbitrary", "arbitrary"), vmem_limit_bytes=VMEM_LIMIT),
        name="na_attn",
    )(proj, proj, proj, proj_meta_keys, proj_meta_keys, bias)


def _ret_kernel(q_ref, k_ref, v_ref, g_ref, km_ref, vm_ref, lgf_ref, lgb_ref, gn_ref, o_ref,
                ds_ref, sf_ref, sb_ref, *, n_chunks, group):
    lgf = lgf_ref[0]
    lgb = lgb_ref[0]
    ii = lax.broadcasted_iota(jnp.int32, (CHUNK, CHUNK), 0).astype(F32)
    jj = lax.broadcasted_iota(jnp.int32, (CHUNK, CHUNK), 1).astype(F32)
    diff = ii - jj
    dmat = jnp.where(diff >= 0, jnp.exp(jnp.maximum(diff, 0.0) * lgf), jnp.exp(jnp.maximum(-diff, 0.0) * lgb))
    xi_f = jnp.exp((ii + 1.0) * lgf)
    zeta_f = jnp.exp((CHUNK - 1.0 - ii) * lgf)
    cd_f = jnp.exp(CHUNK * lgf)
    xi_b = jnp.exp((CHUNK - ii) * lgb)
    zeta_b = jnp.exp(ii * lgb)
    cd_b = jnp.exp(CHUNK * lgb)
    nt = (((1,), (1,)), ((), ()))
    tn_dims = (((0,), (0,)), ((), ()))
    gn = gn_ref[...]

    def chunk_slice(ch):
        return pl.ds(pl.multiple_of(ch * CHUNK, CHUNK), CHUNK)

    def increments(k, v):
        kf = k.astype(F32)
        kz = jnp.concatenate([(kf * zeta_f).astype(BF16), (kf * zeta_b).astype(BF16)], axis=1)
        return lax.dot_general(kz, v, tn_dims, preferred_element_type=F32)

    def stage1(gi, c):
        for u in range(group):
            ch = gi * group + u
            sl = chunk_slice(ch)
            ds_ref[ch] = increments(k_ref[sl, :], v_ref[sl, :])
        return c

    lax.fori_loop(0, n_chunks // group, stage1, 0)

    def scan_f(ch, S):
        sf_ref[ch] = S.astype(BF16)
        return S * cd_f + ds_ref[ch, :CHUNK, :]

    def scan_b(t, S):
        ch = n_chunks - 1 - t
        sb_ref[ch] = S.astype(BF16)
        return S * cd_b + ds_ref[ch, CHUNK:, :]

    S0 = increments(km_ref[...], vm_ref[...])[:CHUNK]
    lax.fori_loop(0, n_chunks, scan_f, S0, unroll=4)
    lax.fori_loop(0, n_chunks, scan_b, jnp.zeros((CHUNK, CHUNK), F32), unroll=4)

    def scores(ch):
        sl = chunk_slice(ch)
        q = q_ref[sl, :]
        s = lax.dot_general(q, k_ref[sl, :], nt, preferred_element_type=F32) * dmat
        qf = q.astype(F32)
        return jnp.concatenate([s.astype(BF16), (qf * xi_f).astype(BF16), (qf * xi_b).astype(BF16)], axis=1)

    def mix(ch, lhs):
        rhs = jnp.concatenate([v_ref[chunk_slice(ch), :], sf_ref[ch], sb_ref[ch]], axis=0)
        return jnp.dot(lhs, rhs, preferred_element_type=F32)

    def finish(ch, o):
        sl = chunk_slice(ch)
        mu = jnp.mean(o, axis=-1, keepdims=True)
        oc = o - mu
        var = jnp.mean(oc * oc, axis=-1, keepdims=True)
        on = oc * lax.rsqrt(var + LN_EPS) * gn
        g = g_ref[sl, :].astype(F32)
        o_ref[sl, :] = (g / (1.0 + jnp.exp(-g)) * on).astype(BF16)

    def stage3(gi, c):
        chs = [gi * group + u for u in range(group)]
        lhs = [scores(ch) for ch in chs]
        outs = [mix(ch, l) for ch, l in zip(chs, lhs)]
        for ch, o in zip(chs, outs):
            finish(ch, o)
        return c

    lax.fori_loop(0, n_chunks // group, stage3, 0)


def _retention(proj, proj_meta_pad, lgf, lgb, gn_g, B, n):
    assert n % CHUNK == 0
    n_chunks = n // CHUNK
    group = math.gcd(n_chunks, RET_GROUP_CHUNKS)
    seq_spec = lambda c0: pl.BlockSpec((n, LANES), lambda b, h: (b, c0 // LANES + h))
    meta_spec = lambda c0: pl.BlockSpec((CHUNK, LANES), lambda b, h: (0, c0 // LANES + h))
    lg_spec = pl.BlockSpec((1, 1, LANES), lambda b, h: (h, 0, 0))
    return pl.pallas_call(
        functools.partial(_ret_kernel, n_chunks=n_chunks, group=group),
        grid=(B, RET_HEADS),
        in_specs=[
            seq_spec(COL_QR), seq_spec(COL_KR), seq_spec(COL_VR), seq_spec(COL_GR),
            meta_spec(COL_KR), meta_spec(COL_VR),
            lg_spec, lg_spec,
            pl.BlockSpec((1, LANES), lambda b, h: (0, h)),
        ],
        out_specs=pl.BlockSpec((n, LANES), lambda b, h: (b, h)),
        out_shape=jax.ShapeDtypeStruct((B * n, RET_WIDTH), BF16),
        scratch_shapes=[pltpu.VMEM((n_chunks, 2 * CHUNK, LANES), F32),
                        pltpu.VMEM((n_chunks, CHUNK, LANES), BF16),
                        pltpu.VMEM((n_chunks, CHUNK, LANES), BF16)],
        compiler_params=pltpu.CompilerParams(
            dimension_semantics=("arbitrary", "arbitrary"), vmem_limit_bytes=VMEM_LIMIT),
        name="retention",
    )(proj, proj, proj, proj, proj_meta_pad, proj_meta_pad, lgf, lgb, gn_g)


def _outproj_kernel(ona_ref, oret_ref, x_ref, w_ref, gin_ref, bin_ref, g1_ref, b1_ref, o_ref, *, row_blocks):
    def norm_store(rows, mix):
        h = _layer_norm_rows(x_ref[rows, :], gin_ref[...], bin_ref[...])
        o_ref[rows, :] = _layer_norm_rows(ALPHA * h + mix, g1_ref[...], b1_ref[...])

    pending = None
    start = 0
    for size in row_blocks:
        rows = slice(start, start + size)
        start += size
        lhs = jnp.concatenate([ona_ref[rows, :], oret_ref[rows, :]], axis=1)
        mix = jnp.dot(lhs, w_ref[...], preferred_element_type=F32)
        if pending is not None:
            norm_store(*pending)
        pending = (rows, mix)
    norm_store(*pending)


def _outproj(o_na, o_ret, x2d, w_bf, gin, bin_, g1, b1, tm=512):
    T = x2d.shape[0]
    assert T % tm == 0 and tm % 4 == 0
    row_blocks = (tm // 2, tm // 4, tm // 4)
    row_vec = pl.BlockSpec((1, D_MODEL), lambda i: (0, 0))
    return pl.pallas_call(
        functools.partial(_outproj_kernel, row_blocks=row_blocks),
        grid=(T // tm,),
        in_specs=[
            pl.BlockSpec((tm, NA_WIDTH), lambda i: (i, 0)),
            pl.BlockSpec((tm, RET_WIDTH), lambda i: (i, 0)),
            pl.BlockSpec((tm, D_MODEL), lambda i: (i, 0)),
            pl.BlockSpec((D_MODEL, D_MODEL), lambda i: (0, 0), pipeline_mode=pl.Buffered(1)),
            row_vec, row_vec, row_vec, row_vec,
        ],
        out_specs=pl.BlockSpec((tm, D_MODEL), lambda i: (i, 0)),
        out_shape=jax.ShapeDtypeStruct((T, D_MODEL), F32),
        compiler_params=pltpu.CompilerParams(dimension_semantics=("arbitrary",), vmem_limit_bytes=VMEM_LIMIT),
        name="outproj_ln1",
    )(o_na, o_ret, x2d, w_bf, gin, bin_, g1, b1)


def _ffn_kernel(h_ref, wg_ref, wu_ref, wd_ref, g2_ref, b2_ref, o_ref, hb_ref, *, rb):
    j = pl.program_id(1)
    tm = h_ref.shape[0]

    @pl.when(j == 0)
    def _():
        hb_ref[...] = h_ref[...].astype(BF16)
        o_ref[...] = jnp.zeros_like(o_ref)

    hb = hb_ref[...]
    g = jnp.dot(hb, wg_ref[...], preferred_element_type=F32)
    u = jnp.dot(hb, wu_ref[...], preferred_element_type=F32)
    a = (g / (1.0 + jnp.exp(-g)) * u).astype(BF16)
    o_ref[...] += jnp.dot(a, wd_ref[...], preferred_element_type=F32)

    @pl.when(j == pl.num_programs(1) - 1)
    def _():
        def rows(i, c):
            sl = pl.ds(pl.multiple_of(i * rb, rb), rb)
            o_ref[sl, :] = _layer_norm_rows(ALPHA * h_ref[sl, :] + o_ref[sl, :], g2_ref[...], b2_ref[...])
            return c

        lax.fori_loop(0, tm // rb, rows, 0)


def _ffn(h1, wg_bf, wu_bf, wd_bf, g2, b2, tm=512, tf=512):
    T = h1.shape[0]
    assert T % tm == 0 and D_FF % tf == 0
    row_vec = pl.BlockSpec((1, D_MODEL), lambda i, j: (0, 0))
    return pl.pallas_call(
        functools.partial(_ffn_kernel, rb=128),
        grid=(T // tm, D_FF // tf),
        in_specs=[
            pl.BlockSpec((tm, D_MODEL), lambda i, j: (i, 0)),
            pl.BlockSpec((D_MODEL, tf), lambda i, j: (0, j)),
            pl.BlockSpec((D_MODEL, tf), lambda i, j: (0, j)),
            pl.BlockSpec((tf, D_MODEL), lambda i, j: (j, 0)),
            row_vec, row_vec,
        ],
        out_specs=pl.BlockSpec((tm, D_MODEL), lambda i, j: (i, 0)),
        out_shape=jax.ShapeDtypeStruct((T, D_MODEL), F32),
        scratch_shapes=[pltpu.VMEM((tm, D_MODEL), BF16)],
        compiler_params=pltpu.CompilerParams(
            dimension_semantics=("arbitrary", "arbitrary"), vmem_limit_bytes=VMEM_LIMIT),
        name="ffn_ln2",
    )(h1, wg_bf, wu_bf, wd_bf, g2, b2)


def _rope_tables(pos):
    half = RET_HEAD_DIM // 2
    inv = ROPE_BASE ** (-jnp.arange(half, dtype=F32) / half)
    n = pos.shape[0]
    if n % ROPE_STRIDE == 0:
        coarse = (pos[::ROPE_STRIDE] - pos[0])[:, None] * inv[None, :]
        fine = pos[:ROPE_STRIDE, None] * inv[None, :]
        cc, sc = jnp.cos(coarse)[:, None, :], jnp.sin(coarse)[:, None, :]
        cf, sf = jnp.cos(fine)[None, :, :], jnp.sin(fine)[None, :, :]
        c = (cc * cf - sc * sf).reshape(n, half)
        s = (sc * cf + cc * sf).reshape(n, half)
    else:
        ang = pos[:, None] * inv[None, :]
        c, s = jnp.cos(ang), jnp.sin(ang)
    cq = jnp.concatenate([c, c], axis=-1)
    sq = jnp.concatenate([-s, s], axis=-1)
    ks = RET_HEAD_DIM ** -0.5
    return (jnp.stack([cq, cq * ks]), jnp.stack([sq, sq * ks]))


def _pick_tm(n, cap):
    tm = cap
    while n % tm:
        tm //= 2
    return tm


def kernel(x_prompt, x_sample, meta_tokens, ln_in_g, ln_in_b, w_in, na_rpb, ret_decay_f, ret_decay_b,
           ret_gn_g, w_out, ln1_g, ln1_b, w_ffn_gate, w_ffn_up, w_ffn_down, ln2_g, ln2_b):
    row = lambda v: v.reshape(1, -1).astype(F32)
    gin, bin_ = row(ln_in_g), row(ln_in_b)
    w_in_bf = w_in[0].astype(BF16)
    w_out_bf = w_out[0].astype(BF16)
    wg_bf, wu_bf, wd_bf = w_ffn_gate[0].astype(BF16), w_ffn_up[0].astype(BF16), w_ffn_down[0].astype(BF16)
    bias = _na_bias(na_rpb[0])
    lg = lambda d: jnp.broadcast_to(jax.nn.log_sigmoid(d[0].astype(F32))[:, None, None], (RET_HEADS, 1, LANES))
    lgf, lgb = lg(ret_decay_f), lg(ret_decay_b)
    gn_g = row(ret_gn_g[0])

    meta_tabs = _rope_tables(jnp.arange(N_META, dtype=F32))
    proj_meta = _inproj(meta_tokens.astype(F32), gin, bin_, w_in_bf, meta_tabs, N_META, N_META)
    proj_meta_pad = jnp.pad(proj_meta, ((CHUNK - N_META, 0), (0, 0)))

    n_max = max(x_prompt.shape[1], x_sample.shape[1])
    all_tabs = _rope_tables(jnp.arange(n_max, dtype=F32) + float(N_META))

    def group(x):
        B, n, _ = x.shape
        x2d = x.reshape(B * n, D_MODEL)
        tabs = tuple(t[:, :n] for t in all_tabs)
        proj = _inproj(x2d, gin, bin_, w_in_bf, tabs, n, _pick_tm(n, 1024))
        o_na = _na(proj, _na_meta(proj_meta), bias, B, n)
        o_ret = _retention(proj, proj_meta_pad, lgf, lgb, gn_g, B, n)
        h1 = _outproj(o_na, o_ret, x2d, w_out_bf, gin, bin_, row(ln1_g[0]), row(ln1_b[0]), tm=_pick_tm(B * n, 512))
        y = _ffn(h1, wg_bf, wu_bf, wd_bf, row(ln2_g[0]), row(ln2_b[0]), tm=_pick_tm(B * n, 1024))
        return y.reshape(B, n, D_MODEL)

    return (group(x_prompt), group(x_sample))


def _na_meta(proj_meta):
    return jnp.pad(proj_meta, ((0, NA_META_SLOTS - N_META), (0, 0)))
```

```python
import functools
import math

import jax
import jax.numpy as jnp
from jax import lax
from jax.experimental import pallas as pl
from jax.experimental.pallas import tpu as pltpu

D_MODEL = 2048
N_META = 16
GRID_W = 64
NA_HEADS = 16
NA_HEAD_DIM = 64
NA_WIDTH = NA_HEADS * NA_HEAD_DIM
NA_WIN_H = 8
NA_WIN_W = 16
NA_HALF = GRID_W // 2
NA_KEY_COLS = 48
NA_KEY_SHIFT = GRID_W - NA_KEY_COLS
NA_UNION = NA_WIN_H + 1
NA_KEYS = 512
NA_META_SLOTS = NA_KEYS - NA_UNION * NA_KEY_COLS
RET_HEADS = 8
RET_HEAD_DIM = 128
RET_WIDTH = RET_HEADS * RET_HEAD_DIM
CHUNK = 128
ROPE_BASE = 10000.0
ROPE_STRIDE = 128
IN_WIDTH = 3 * NA_WIDTH + 4 * RET_WIDTH
D_FF = 5632
LN_EPS = 1e-5
ALPHA = 2.0 ** 0.25

COL_QA, COL_KA, COL_VA = 0, NA_WIDTH, 2 * NA_WIDTH
COL_QR = 3 * NA_WIDTH
COL_KR = COL_QR + RET_WIDTH
COL_VR = COL_KR + RET_WIDTH
COL_GR = COL_VR + RET_WIDTH

NA_GROUP_ROWS = 32
RET_GROUP_CHUNKS = 32
MXU_LEAD = 3

LANES = 128
NEG = -1e30
VMEM_LIMIT = 56 * 1024 * 1024

F32 = jnp.float32
BF16 = jnp.bfloat16


def _layer_norm_rows(x, g, b):
    mu = jnp.mean(x, axis=-1, keepdims=True)
    xc = x - mu
    var = jnp.mean(xc * xc, axis=-1, keepdims=True)
    return xc * lax.rsqrt(var + LN_EPS) * g + b


def _inproj_kernel(x_ref, g_ref, b_ref, w_ref, c_ref, s_ref, o_ref, hb_ref, *, tn, rb):
    j = pl.program_id(1)
    tm = x_ref.shape[0]

    def store(acc, rows, tile):
        for t in range(tn // LANES):
            col = tile * tn + t * LANES
            xs = acc[:, t * LANES:(t + 1) * LANES]
            if COL_QR <= col < COL_VR:
                which = 0 if col < COL_KR else 1
                xs = xs * c_ref[which, rows, :] + pltpu.roll(xs, LANES // 2, 1) * s_ref[which, rows, :]
            o_ref[rows, t * LANES:(t + 1) * LANES] = xs.astype(BF16)

    @pl.when(j == 0)
    def _():
        for sb in range(tm // rb):
            rows = slice(sb * rb, (sb + 1) * rb)
            hb_ref[rows, :] = _layer_norm_rows(x_ref[rows, :], g_ref[...], b_ref[...]).astype(BF16)
            store(jnp.dot(hb_ref[rows, :], w_ref[...], preferred_element_type=F32), rows, 0)

    for tile in range(1, IN_WIDTH // tn):
        @pl.when(j == tile)
        def _(tile=tile):
            store(jnp.dot(hb_ref[...], w_ref[...], preferred_element_type=F32), slice(None), tile)


def _inproj(x2d, ln_g, ln_b, w_bf, tabs, n_seq, tm, tn=1792):
    T = x2d.shape[0]
    assert T % tm == 0 and n_seq % tm == 0 and IN_WIDTH % tn == 0 and tn % LANES == 0
    blocks_per_seq = n_seq // tm
    tab_spec = pl.BlockSpec((2, tm, LANES), lambda i, j: (0, i % blocks_per_seq, 0))
    rb = min(tm, 256)
    return pl.pallas_call(
        functools.partial(_inproj_kernel, tn=tn, rb=rb),
        grid=(T // tm, IN_WIDTH // tn),
        in_specs=[
            pl.BlockSpec((tm, D_MODEL), lambda i, j: (i, 0)),
            pl.BlockSpec((1, D_MODEL), lambda i, j: (0, 0)),
            pl.BlockSpec((1, D_MODEL), lambda i, j: (0, 0)),
            pl.BlockSpec((D_MODEL, tn), lambda i, j: (0, j)),
            tab_spec, tab_spec,
        ],
        out_specs=pl.BlockSpec((tm, tn), lambda i, j: (i, j)),
        out_shape=jax.ShapeDtypeStruct((T, IN_WIDTH), BF16),
        scratch_shapes=[pltpu.VMEM((tm, D_MODEL), BF16)],
        compiler_params=pltpu.CompilerParams(
            dimension_semantics=("arbitrary", "arbitrary"), vmem_limit_bytes=VMEM_LIMIT),
        name="inproj",
    )(x2d, ln_g, ln_b, w_bf, *tabs)


def _na_kernel(q_ref, k_ref, v_ref, km_ref, vm_ref, bias_ref, o_ref, *, rows, unroll):
    lane = lax.broadcasted_iota(jnp.int32, (1, LANES), 1)
    scale = NA_HEAD_DIM ** -0.5
    qmask = [jnp.where((lane // NA_HEAD_DIM) == hh, scale, 0.0).astype(BF16) for hh in range(2)]
    km = km_ref[...]
    vm = vm_ref[...]
    nt = (((1,), (1,)), ((), ()))

    def window(ref, meta, b0, hf):
        parts = [ref[pl.ds(pl.multiple_of((b0 + i) * GRID_W + NA_KEY_SHIFT * hf, NA_KEY_SHIFT), NA_KEY_COLS), :]
                 for i in range(NA_UNION)]
        return jnp.concatenate(parts + [meta], axis=0)

    units = [(u, hf) for u in range(unroll // 2) for hf in range(2)]

    def locate(g, idx):
        u, hf = units[idx]
        r0 = g * unroll + 2 * u
        b0 = jnp.clip(r0 - NA_WIN_H // 2, 0, rows - NA_UNION)
        qsl = [pl.ds(pl.multiple_of((r0 + t) * GRID_W + NA_HALF * hf, NA_HALF), NA_HALF) for t in range(2)]
        return r0 - b0, qsl, b0, hf

    def scores(g, idx):
        e0, qsl, b0, hf = locate(g, idx)
        q2 = jnp.concatenate([q_ref[sl, :] * qmask[hh] for sl in qsl for hh in range(2)], axis=0)
        s = lax.dot_general(q2, window(k_ref, km, b0, hf), nt, preferred_element_type=F32)
        bias = jnp.concatenate([bias_ref[0, hf, pl.ds(e0 + t, 1)][0] for t in range(2)], axis=0)
        return s + bias

    def finish(s, g, idx):
        _, qsl, b0, hf = locate(g, idx)
        m = jnp.max(s, axis=-1, keepdims=True)
        p = jnp.exp(s - m)
        den = jnp.sum(p, axis=-1, keepdims=True)
        o2 = jnp.dot(p.astype(BF16), window(v_ref, vm, b0, hf), preferred_element_type=F32) / den
        for t in range(2):
            blk = o2[2 * NA_HALF * t:2 * NA_HALF * (t + 1)]
            o_ref[qsl[t], :] = jnp.where(lane < NA_HEAD_DIM, blk[:NA_HALF], blk[NA_HALF:]).astype(BF16)

    def body(g, c):
        lead = min(MXU_LEAD, len(units))
        staged = [scores(g, idx) for idx in range(lead)]
        for idx in range(len(units)):
            if idx + lead < len(units):
                staged.append(scores(g, idx + lead))
            finish(staged[idx], g, idx)
        return c

    lax.fori_loop(0, rows // unroll, body, 0)


def _na_bias_kernel(rpb_ref, o_ref):
    c32 = lax.broadcasted_iota(jnp.int32, (NA_HALF, LANES), 0)
    lane = lax.broadcasted_iota(jnp.int32, (NA_HALF, LANES), 1)
    masked = jnp.where(lane + c32 < 0, 0.0, NEG).astype(F32)[:, :NA_KEY_COLS]
    meta = jnp.where(lane + LANES * c32 < N_META + LANES * c32, 0.0, NEG).astype(F32)[:, :NA_META_SLOTS]
    tiles = {}
    for hf in range(2):
        c = c32 + NA_HALF * hf
        kc = lane + NA_KEY_SHIFT * hf
        cs = jnp.clip(c - NA_WIN_W // 2, 0, GRID_W - NA_WIN_W)
        valid = (kc >= cs) & (kc < cs + NA_WIN_W) & (lane < NA_KEY_COLS)
        shift = (LANES - (NA_WIN_W - 1) - NA_KEY_SHIFT * hf + NA_HALF * hf) % LANES
        for h in range(2):
            for dr in range(2 * NA_WIN_H - 1):
                w = jnp.broadcast_to(rpb_ref[h, dr:dr + 1, :], (NA_HALF, LANES))
                tiles[(hf, h, dr)] = jnp.where(valid, pltpu.roll(w, shift, 1, stride=1, stride_axis=0), NEG)
    for hf in range(2):
        for e in range(NA_UNION):
            d = 1 if e > NA_WIN_H // 2 else 0
            heads = []
            for h in range(2):
                parts = [tiles[(hf, h, i - e + NA_WIN_H - 1)][:, :NA_KEY_COLS] if 0 <= i - d < NA_WIN_H else masked
                         for i in range(NA_UNION)]
                heads.append(jnp.concatenate(parts + [meta], axis=1))
            o_ref[0, hf, e] = jnp.concatenate(heads, axis=0)


def _na_bias(rpb):
    n_dr = 2 * NA_WIN_H - 1
    rpb_lanes = jnp.pad(rpb.astype(F32), ((0, 0), (0, 0), (0, LANES - rpb.shape[-1])))
    shape = (NA_HEADS // 2, 2, NA_UNION, 2 * NA_HALF, NA_KEYS)
    return pl.pallas_call(
        _na_bias_kernel,
        grid=(NA_HEADS // 2,),
        in_specs=[pl.BlockSpec((2, n_dr, LANES), lambda p: (p, 0, 0))],
        out_specs=pl.BlockSpec((1,) + shape[1:], lambda p: (p, 0, 0, 0, 0)),
        out_shape=jax.ShapeDtypeStruct(shape, F32),
        compiler_params=pltpu.CompilerParams(dimension_semantics=("arbitrary",), vmem_limit_bytes=VMEM_LIMIT),
        name="na_bias",
    )(rpb_lanes)


def _na(proj, proj_meta_keys, bias, B, n):
    rows = n // GRID_W
    unroll = math.gcd(rows, NA_GROUP_ROWS)
    assert rows >= NA_UNION and unroll % 2 == 0
    pairs = NA_WIDTH // LANES
    seq_spec = lambda c0: pl.BlockSpec((n, LANES), lambda hp, b: (b, c0 // LANES + hp))
    meta_spec = lambda c0: pl.BlockSpec((NA_META_SLOTS, LANES), lambda hp, b: (0, c0 // LANES + hp))
    return pl.pallas_call(
        functools.partial(_na_kernel, rows=rows, unroll=unroll),
        grid=(pairs, B),
        in_specs=[
            seq_spec(COL_QA), seq_spec(COL_KA), seq_spec(COL_VA),
            meta_spec(COL_KA), meta_spec(COL_VA),
            pl.BlockSpec((1, 2, NA_UNION, 2 * NA_HALF, NA_KEYS), lambda hp, b: (hp, 0, 0, 0, 0)),
        ],
        out_specs=pl.BlockSpec((n, LANES), lambda hp, b: (b, hp)),
        out_shape=jax.ShapeDtypeStruct((B * n, NA_WIDTH), BF16),
        compiler_params=pltpu.CompilerParams(
            dimension_semantics=("arbitrary", "arbitrary"), vmem_limit_bytes=VMEM_LIMIT),
        name="na_attn",
    )(proj, proj, proj, proj_meta_keys, proj_meta_keys, bias)


def _ret_kernel(q_ref, k_ref, v_ref, g_ref, km_ref, vm_ref, lgf_ref, lgb_ref, gn_ref, o_ref,
                ds_ref, sf_ref, sb_ref, *, n_chunks, group):
    lgf = lgf_ref[0]
    lgb = lgb_ref[0]
    ii = lax.broadcasted_iota(jnp.int32, (CHUNK, CHUNK), 0).astype(F32)
    jj = lax.broadcasted_iota(jnp.int32, (CHUNK, CHUNK), 1).astype(F32)
    diff = ii - jj
    dmat = jnp.where(diff >= 0, jnp.exp(jnp.maximum(diff, 0.0) * lgf), jnp.exp(jnp.maximum(-diff, 0.0) * lgb))
    xi_f = jnp.exp((ii + 1.0) * lgf)
    zeta_f = jnp.exp((CHUNK - 1.0 - ii) * lgf)
    cd_f = jnp.exp(CHUNK * lgf)
    xi_b = jnp.exp((CHUNK - ii) * lgb)
    zeta_b = jnp.exp(ii * lgb)
    cd_b = jnp.exp(CHUNK * lgb)
    nt = (((1,), (1,)), ((), ()))
    tn_dims = (((0,), (0,)), ((), ()))
    gn = gn_ref[...]

    def chunk_slice(ch):
        return pl.ds(pl.multiple_of(ch * CHUNK, CHUNK), CHUNK)

    def increments(k, v):
        kf = k.astype(F32)
        kz = jnp.concatenate([(kf * zeta_f).astype(BF16), (kf * zeta_b).astype(BF16)], axis=1)
        return lax.dot_general(kz, v, tn_dims, preferred_element_type=F32)

    def stage1(gi, c):
        for u in range(group):
            ch = gi * group + u
            sl = chunk_slice(ch)
            ds_ref[ch] = increments(k_ref[sl, :], v_ref[sl, :])
        return c

    lax.fori_loop(0, n_chunks // group, stage1, 0)

    def scan_f(ch, S):
        sf_ref[ch] = S.astype(BF16)
        return S * cd_f + ds_ref[ch, :CHUNK, :]

    def scan_b(t, S):
        ch = n_chunks - 1 - t
        sb_ref[ch] = S.astype(BF16)
        return S * cd_b + ds_ref[ch, CHUNK:, :]

    S0 = increments(km_ref[...], vm_ref[...])[:CHUNK]
    lax.fori_loop(0, n_chunks, scan_f, S0, unroll=4)
    lax.fori_loop(0, n_chunks, scan_b, jnp.zeros((CHUNK, CHUNK), F32), unroll=4)

    def scores(ch):
        sl = chunk_slice(ch)
        q = q_ref[sl, :]
        s = lax.dot_general(q, k_ref[sl, :], nt, preferred_element_type=F32) * dmat
        qf = q.astype(F32)
        return jnp.concatenate([s.astype(BF16), (qf * xi_f).astype(BF16), (qf * xi_b).astype(BF16)], axis=1)

    def mix(ch, lhs):
        rhs = jnp.concatenate([v_ref[chunk_slice(ch), :], sf_ref[ch], sb_ref[ch]], axis=0)
        return jnp.dot(lhs, rhs, preferred_element_type=F32)

    def finish(ch, o):
        sl = chunk_slice(ch)
        mu = jnp.mean(o, axis=-1, keepdims=True)
        oc = o - mu
        var = jnp.mean(oc * oc, axis=-1, keepdims=True)
        on = oc * lax.rsqrt(var + LN_EPS) * gn
        g = g_ref[sl, :].astype(F32)
        o_ref[sl, :] = (g / (1.0 + jnp.exp(-g)) * on).astype(BF16)

    def stage3(gi, c):
        chs = [gi * group + u for u in range(group)]
        lead = min(MXU_LEAD, group)
        lhs = [scores(ch) for ch in chs[:lead]]
        for u, ch in enumerate(chs):
            if u + lead < group:
                lhs.append(scores(chs[u + lead]))
            finish(ch, mix(ch, lhs[u]))
        return c

    lax.fori_loop(0, n_chunks // group, stage3, 0)


def _retention(proj, proj_meta_pad, lgf, lgb, gn_g, B, n):
    assert n % CHUNK == 0
    n_chunks = n // CHUNK
    group = math.gcd(n_chunks, RET_GROUP_CHUNKS)
    seq_spec = lambda c0: pl.BlockSpec((n, LANES), lambda b, h: (b, c0 // LANES + h))
    meta_spec = lambda c0: pl.BlockSpec((CHUNK, LANES), lambda b, h: (0, c0 // LANES + h))
    lg_spec = pl.BlockSpec((1, 1, LANES), lambda b, h: (h, 0, 0))
    return pl.pallas_call(
        functools.partial(_ret_kernel, n_chunks=n_chunks, group=group),
        grid=(B, RET_HEADS),
        in_specs=[
            seq_spec(COL_QR), seq_spec(COL_KR), seq_spec(COL_VR), seq_spec(COL_GR),
            meta_spec(COL_KR), meta_spec(COL_VR),
            lg_spec, lg_spec,
            pl.BlockSpec((1, LANES), lambda b, h: (0, h)),
        ],
        out_specs=pl.BlockSpec((n, LANES), lambda b, h: (b, h)),
        out_shape=jax.ShapeDtypeStruct((B * n, RET_WIDTH), BF16),
        scratch_shapes=[pltpu.VMEM((n_chunks, 2 * CHUNK, LANES), F32),
                        pltpu.VMEM((n_chunks, CHUNK, LANES), BF16),
                        pltpu.VMEM((n_chunks, CHUNK, LANES), BF16)],
        compiler_params=pltpu.CompilerParams(
            dimension_semantics=("arbitrary", "arbitrary"), vmem_limit_bytes=VMEM_LIMIT),
        name="retention",
    )(proj, proj, proj, proj, proj_meta_pad, proj_meta_pad, lgf, lgb, gn_g)


def _outproj_kernel(ona_ref, oret_ref, x_ref, w_ref, gin_ref, bin_ref, g1_ref, b1_ref, o_ref, *, row_blocks):
    def norm_store(rows, mix):
        h = _layer_norm_rows(x_ref[rows, :], gin_ref[...], bin_ref[...])
        o_ref[rows, :] = _layer_norm_rows(ALPHA * h + mix, g1_ref[...], b1_ref[...])

    pending = None
    start = 0
    for size in row_blocks:
        rows = slice(start, start + size)
        start += size
        lhs = jnp.concatenate([ona_ref[rows, :], oret_ref[rows, :]], axis=1)
        mix = jnp.dot(lhs, w_ref[...], preferred_element_type=F32)
        if pending is not None:
            norm_store(*pending)
        pending = (rows, mix)
    norm_store(*pending)


def _outproj(o_na, o_ret, x2d, w_bf, gin, bin_, g1, b1, tm=512):
    T = x2d.shape[0]
    assert T % tm == 0 and tm % 4 == 0
    row_blocks = (tm // 2, tm // 4, tm // 4)
    row_vec = pl.BlockSpec((1, D_MODEL), lambda i: (0, 0))
    return pl.pallas_call(
        functools.partial(_outproj_kernel, row_blocks=row_blocks),
        grid=(T // tm,),
        in_specs=[
            pl.BlockSpec((tm, NA_WIDTH), lambda i: (i, 0)),
            pl.BlockSpec((tm, RET_WIDTH), lambda i: (i, 0)),
            pl.BlockSpec((tm, D_MODEL), lambda i: (i, 0)),
            pl.BlockSpec((D_MODEL, D_MODEL), lambda i: (0, 0), pipeline_mode=pl.Buffered(1)),
            row_vec, row_vec, row_vec, row_vec,
        ],
        out_specs=pl.BlockSpec((tm, D_MODEL), lambda i: (i, 0)),
        out_shape=jax.ShapeDtypeStruct((T, D_MODEL), F32),
        compiler_params=pltpu.CompilerParams(dimension_semantics=("arbitrary",), vmem_limit_bytes=VMEM_LIMIT),
        name="outproj_ln1",
    )(o_na, o_ret, x2d, w_bf, gin, bin_, g1, b1)


def _ffn_kernel(h_ref, wg_ref, wu_ref, wd_ref, g2_ref, b2_ref, o_ref, hb_ref, *, rb):
    j = pl.program_id(1)
    tm = h_ref.shape[0]

    @pl.when(j == 0)
    def _():
        hb_ref[...] = h_ref[...].astype(BF16)
        o_ref[...] = jnp.zeros_like(o_ref)

    hb = hb_ref[...]
    g = jnp.dot(hb, wg_ref[...], preferred_element_type=F32)
    u = jnp.dot(hb, wu_ref[...], preferred_element_type=F32)
    a = (g / (1.0 + jnp.exp(-g)) * u).astype(BF16)
    o_ref[...] += jnp.dot(a, wd_ref[...], preferred_element_type=F32)

    @pl.when(j == pl.num_programs(1) - 1)
    def _():
        def rows(i, c):
            sl = pl.ds(pl.multiple_of(i * rb, rb), rb)
            o_ref[sl, :] = _layer_norm_rows(ALPHA * h_ref[sl, :] + o_ref[sl, :], g2_ref[...], b2_ref[...])
            return c

        lax.fori_loop(0, tm // rb, rows, 0)


def _ffn(h1, wg_bf, wu_bf, wd_bf, g2, b2, tm=512, tf=512):
    T = h1.shape[0]
    assert T % tm == 0 and D_FF % tf == 0
    row_vec = pl.BlockSpec((1, D_MODEL), lambda i, j: (0, 0))
    return pl.pallas_call(
        functools.partial(_ffn_kernel, rb=128),
        grid=(T // tm, D_FF // tf),
        in_specs=[
            pl.BlockSpec((tm, D_MODEL), lambda i, j: (i, 0)),
            pl.BlockSpec((D_MODEL, tf), lambda i, j: (0, j)),
            pl.BlockSpec((D_MODEL, tf), lambda i, j: (0, j)),
            pl.BlockSpec((tf, D_MODEL), lambda i, j: (j, 0)),
            row_vec, row_vec,
        ],
        out_specs=pl.BlockSpec((tm, D_MODEL), lambda i, j: (i, 0)),
        out_shape=jax.ShapeDtypeStruct((T, D_MODEL), F32),
        scratch_shapes=[pltpu.VMEM((tm, D_MODEL), BF16)],
        compiler_params=pltpu.CompilerParams(
            dimension_semantics=("arbitrary", "arbitrary"), vmem_limit_bytes=VMEM_LIMIT),
        name="ffn_ln2",
    )(h1, wg_bf, wu_bf, wd_bf, g2, b2)


def _rope_tables(pos):
    half = RET_HEAD_DIM // 2
    inv = ROPE_BASE ** (-jnp.arange(half, dtype=F32) / half)
    n = pos.shape[0]
    if n % ROPE_STRIDE == 0:
        coarse = (pos[::ROPE_STRIDE] - pos[0])[:, None] * inv[None, :]
        fine = pos[:ROPE_STRIDE, None] * inv[None, :]
        cc, sc = jnp.cos(coarse)[:, None, :], jnp.sin(coarse)[:, None, :]
        cf, sf = jnp.cos(fine)[None, :, :], jnp.sin(fine)[None, :, :]
        c = (cc * cf - sc * sf).reshape(n, half)
        s = (sc * cf + cc * sf).reshape(n, half)
    else:
        ang = pos[:, None] * inv[None, :]
        c, s = jnp.cos(ang), jnp.sin(ang)
    cq = jnp.concatenate([c, c], axis=-1)
    sq = jnp.concatenate([-s, s], axis=-1)
    ks = RET_HEAD_DIM ** -0.5
    return (jnp.stack([cq, cq * ks]), jnp.stack([sq, sq * ks]))


def _pick_tm(n, cap):
    tm = cap
    while n % tm:
        tm //= 2
    return tm


def kernel(x_prompt, x_sample, meta_tokens, ln_in_g, ln_in_b, w_in, na_rpb, ret_decay_f, ret_decay_b,
           ret_gn_g, w_out, ln1_g, ln1_b, w_ffn_gate, w_ffn_up, w_ffn_down, ln2_g, ln2_b):
    row = lambda v: v.reshape(1, -1).astype(F32)
    gin, bin_ = row(ln_in_g), row(ln_in_b)
    w_in_bf = w_in[0].astype(BF16)
    w_out_bf = w_out[0].astype(BF16)
    wg_bf, wu_bf, wd_bf = w_ffn_gate[0].astype(BF16), w_ffn_up[0].astype(BF16), w_ffn_down[0].astype(BF16)
    bias = _na_bias(na_rpb[0])
    lg = lambda d: jnp.broadcast_to(jax.nn.log_sigmoid(d[0].astype(F32))[:, None, None], (RET_HEADS, 1, LANES))
    lgf, lgb = lg(ret_decay_f), lg(ret_decay_b)
    gn_g = row(ret_gn_g[0])

    meta_tabs = _rope_tables(jnp.arange(N_META, dtype=F32))
    proj_meta = _inproj(meta_tokens.astype(F32), gin, bin_, w_in_bf, meta_tabs, N_META, N_META)
    proj_meta_pad = jnp.pad(proj_meta, ((CHUNK - N_META, 0), (0, 0)))

    n_max = max(x_prompt.shape[1], x_sample.shape[1])
    all_tabs = _rope_tables(jnp.arange(n_max, dtype=F32) + float(N_META))

    def group(x):
        B, n, _ = x.shape
        x2d = x.reshape(B * n, D_MODEL)
        tabs = tuple(t[:, :n] for t in all_tabs)
        proj = _inproj(x2d, gin, bin_, w_in_bf, tabs, n, _pick_tm(n, 1024))
        o_na = _na(proj, _na_meta(proj_meta), bias, B, n)
        o_ret = _retention(proj, proj_meta_pad, lgf, lgb, gn_g, B, n)
        h1 = _outproj(o_na, o_ret, x2d, w_out_bf, gin, bin_, row(ln1_g[0]), row(ln1_b[0]), tm=_pick_tm(B * n, 512))
        y = _ffn(h1, wg_bf, wu_bf, wd_bf, row(ln2_g[0]), row(ln2_b[0]), tm=_pick_tm(B * n, 1024))
        return y.reshape(B, n, D_MODEL)

    return (group(x_prompt), group(x_sample))


def _na_meta(proj_meta):
    return jnp.pad(proj_meta, ((0, NA_META_SLOTS - N_META), (0, 0)))
```

```python
import functools
import math

import jax
import jax.numpy as jnp
from jax import lax
from jax.experimental import pallas as pl
from jax.experimental.pallas import tpu as pltpu

D_MODEL = 2048
N_META = 16
GRID_W = 64
NA_HEADS = 16
NA_HEAD_DIM = 64
NA_WIDTH = NA_HEADS * NA_HEAD_DIM
NA_WIN_H = 8
NA_WIN_W = 16
NA_HALF = GRID_W // 2
NA_KEY_COLS = 48
NA_KEY_SHIFT = GRID_W - NA_KEY_COLS
NA_UNION = NA_WIN_H + 1
NA_KEYS = 512
NA_META_SLOTS = NA_KEYS - NA_UNION * NA_KEY_COLS
RET_HEADS = 8
RET_HEAD_DIM = 128
RET_WIDTH = RET_HEADS * RET_HEAD_DIM
CHUNK = 128
ROPE_BASE = 10000.0
ROPE_STRIDE = 128
IN_WIDTH = 3 * NA_WIDTH + 4 * RET_WIDTH
D_FF = 5632
LN_EPS = 1e-5
ALPHA = 2.0 ** 0.25

COL_QA, COL_KA, COL_VA = 0, NA_WIDTH, 2 * NA_WIDTH
COL_QR = 3 * NA_WIDTH
COL_KR = COL_QR + RET_WIDTH
COL_VR = COL_KR + RET_WIDTH
COL_GR = COL_VR + RET_WIDTH

NA_GROUP_ROWS = 32
RET_GROUP_CHUNKS = 32
MXU_LEAD = 3

LANES = 128
NEG = -1e30
VMEM_LIMIT = 56 * 1024 * 1024

F32 = jnp.float32
BF16 = jnp.bfloat16


def _layer_norm_rows(x, g, b):
    mu = jnp.mean(x, axis=-1, keepdims=True)
    xc = x - mu
    var = jnp.mean(xc * xc, axis=-1, keepdims=True)
    return xc * lax.rsqrt(var + LN_EPS) * g + b


def _inproj_kernel(x_ref, g_ref, b_ref, w_ref, c_ref, s_ref, o_ref, hb_ref, *, tn, rb):
    j = pl.program_id(1)
    tm = x_ref.shape[0]

    def store(acc, rows, tile):
        for t in range(tn // LANES):
            col = tile * tn + t * LANES
            xs = acc[:, t * LANES:(t + 1) * LANES]
            if COL_QR <= col < COL_VR:
                which = 0 if col < COL_KR else 1
                xs = xs * c_ref[which, rows, :] + pltpu.roll(xs, LANES // 2, 1) * s_ref[which, rows, :]
            o_ref[rows, t * LANES:(t + 1) * LANES] = xs.astype(BF16)

    @pl.when(j == 0)
    def _():
        for sb in range(tm // rb):
            rows = slice(sb * rb, (sb + 1) * rb)
            hb_ref[rows, :] = _layer_norm_rows(x_ref[rows, :], g_ref[...], b_ref[...]).astype(BF16)
            store(jnp.dot(hb_ref[rows, :], w_ref[...], preferred_element_type=F32), rows, 0)

    for tile in range(1, IN_WIDTH // tn):
        @pl.when(j == tile)
        def _(tile=tile):
            store(jnp.dot(hb_ref[...], w_ref[...], preferred_element_type=F32), slice(None), tile)


def _inproj(x2d, ln_g, ln_b, w_bf, tabs, n_seq, tm, tn=1792):
    T = x2d.shape[0]
    assert T % tm == 0 and n_seq % tm == 0 and IN_WIDTH % tn == 0 and tn % LANES == 0
    blocks_per_seq = n_seq // tm
    tab_spec = pl.BlockSpec((2, tm, LANES), lambda i, j: (0, i % blocks_per_seq, 0))
    rb = min(tm, 256)
    return pl.pallas_call(
        functools.partial(_inproj_kernel, tn=tn, rb=rb),
        grid=(T // tm, IN_WIDTH // tn),
        in_specs=[
            pl.BlockSpec((tm, D_MODEL), lambda i, j: (i, 0)),
            pl.BlockSpec((1, D_MODEL), lambda i, j: (0, 0)),
            pl.BlockSpec((1, D_MODEL), lambda i, j: (0, 0)),
            pl.BlockSpec((D_MODEL, tn), lambda i, j: (0, j)),
            tab_spec, tab_spec,
        ],
        out_specs=pl.BlockSpec((tm, tn), lambda i, j: (i, j)),
        out_shape=jax.ShapeDtypeStruct((T, IN_WIDTH), BF16),
        scratch_shapes=[pltpu.VMEM((tm, D_MODEL), BF16)],
        compiler_params=pltpu.CompilerParams(
            dimension_semantics=("arbitrary", "arbitrary"), vmem_limit_bytes=VMEM_LIMIT),
        name="inproj",
    )(x2d, ln_g, ln_b, w_bf, *tabs)


def _na_kernel(q_ref, k_ref, v_ref, km_ref, vm_ref, bias_ref, o_ref, *, rows, unroll):
    lane = lax.broadcasted_iota(jnp.int32, (1, LANES), 1)
    scale = NA_HEAD_DIM ** -0.5
    qmask = [jnp.where((lane // NA_HEAD_DIM) == hh, scale, 0.0).astype(BF16) for hh in range(2)]
    km = km_ref[...]
    vm = vm_ref[...]
    nt = (((1,), (1,)), ((), ()))

    def window(ref, meta, b0, hf):
        parts = [ref[pl.ds(pl.multiple_of((b0 + i) * GRID_W + NA_KEY_SHIFT * hf, NA_KEY_SHIFT), NA_KEY_COLS), :]
                 for i in range(NA_UNION)]
        return jnp.concatenate(parts + [meta], axis=0)

    units = [(u, hf) for u in range(unroll // 2) for hf in range(2)]

    def locate(g, idx):
        u, hf = units[idx]
        r0 = g * unroll + 2 * u
        b0 = jnp.clip(r0 - NA_WIN_H // 2, 0, rows - NA_UNION)
        qsl = [pl.ds(pl.multiple_of((r0 + t) * GRID_W + NA_HALF * hf, NA_HALF), NA_HALF) for t in range(2)]
        return r0 - b0, qsl, b0, hf

    def scores(g, idx):
        e0, qsl, b0, hf = locate(g, idx)
        q2 = jnp.concatenate([q_ref[sl, :] * qmask[hh] for sl in qsl for hh in range(2)], axis=0)
        s = lax.dot_general(q2, window(k_ref, km, b0, hf), nt, preferred_element_type=F32)
        bias = jnp.concatenate([bias_ref[0, hf, pl.ds(e0 + t, 1)][0] for t in range(2)], axis=0)
        return s + bias

    def finish(s, g, idx):
        _, qsl, b0, hf = locate(g, idx)
        m = jnp.max(s, axis=-1, keepdims=True)
        p = jnp.exp(s - m)
        den = jnp.sum(p, axis=-1, keepdims=True)
        o2 = jnp.dot(p.astype(BF16), window(v_ref, vm, b0, hf), preferred_element_type=F32) / den
        for t in range(2):
            blk = o2[2 * NA_HALF * t:2 * NA_HALF * (t + 1)]
            o_ref[qsl[t], :] = jnp.where(lane < NA_HEAD_DIM, blk[:NA_HALF], blk[NA_HALF:]).astype(BF16)

    def body(g, c):
        lead = min(MXU_LEAD, len(units))
        staged = [scores(g, idx) for idx in range(lead)]
        for idx in range(len(units)):
            if idx + lead < len(units):
                staged.append(scores(g, idx + lead))
            finish(staged[idx], g, idx)
        return c

    lax.fori_loop(0, rows // unroll, body, 0)


def _na_bias_kernel(rpb_ref, o_ref):
    c32 = lax.broadcasted_iota(jnp.int32, (NA_HALF, LANES), 0)
    lane = lax.broadcasted_iota(jnp.int32, (NA_HALF, LANES), 1)
    masked = jnp.where(lane + c32 < 0, 0.0, NEG).astype(F32)[:, :NA_KEY_COLS]
    meta = jnp.where(lane + LANES * c32 < N_META + LANES * c32, 0.0, NEG).astype(F32)[:, :NA_META_SLOTS]
    tiles = {}
    for hf in range(2):
        c = c32 + NA_HALF * hf
        kc = lane + NA_KEY_SHIFT * hf
        cs = jnp.clip(c - NA_WIN_W // 2, 0, GRID_W - NA_WIN_W)
        valid = (kc >= cs) & (kc < cs + NA_WIN_W) & (lane < NA_KEY_COLS)
        shift = (LANES - (NA_WIN_W - 1) - NA_KEY_SHIFT * hf + NA_HALF * hf) % LANES
        for h in range(2):
            for dr in range(2 * NA_WIN_H - 1):
                w = jnp.broadcast_to(rpb_ref[h, dr:dr + 1, :], (NA_HALF, LANES))
                tiles[(hf, h, dr)] = jnp.where(valid, pltpu.roll(w, shift, 1, stride=1, stride_axis=0), NEG)
    for hf in range(2):
        for e in range(NA_UNION):
            d = 1 if e > NA_WIN_H // 2 else 0
            heads = []
            for h in range(2):
                parts = [tiles[(hf, h, i - e + NA_WIN_H - 1)][:, :NA_KEY_COLS] if 0 <= i - d < NA_WIN_H else masked
                         for i in range(NA_UNION)]
                heads.append(jnp.concatenate(parts + [meta], axis=1))
            o_ref[0, hf, e] = jnp.concatenate(heads, axis=0)


def _na_bias(rpb):
    n_dr = 2 * NA_WIN_H - 1
    rpb_lanes = jnp.pad(rpb.astype(F32), ((0, 0), (0, 0), (0, LANES - rpb.shape[-1])))
    shape = (NA_HEADS // 2, 2, NA_UNION, 2 * NA_HALF, NA_KEYS)
    return pl.pallas_call(
        _na_bias_kernel,
        grid=(NA_HEADS // 2,),
        in_specs=[pl.BlockSpec((2, n_dr, LANES), lambda p: (p, 0, 0))],
        out_specs=pl.BlockSpec((1,) + shape[1:], lambda p: (p, 0, 0, 0, 0)),
        out_shape=jax.ShapeDtypeStruct(shape, F32),
        compiler_params=pltpu.CompilerParams(dimension_semantics=("arbitrary",), vmem_limit_bytes=VMEM_LIMIT),
        name="na_bias",
    )(rpb_lanes)


def _na(proj, proj_meta_keys, bias, B, n):
    rows = n // GRID_W
    unroll = math.gcd(rows, NA_GROUP_ROWS)
    assert rows >= NA_UNION and unroll % 2 == 0
    pairs = NA_WIDTH // LANES
    seq_spec = lambda c0: pl.BlockSpec((n, LANES), lambda hp, b: (b, c0 // LANES + hp))
    meta_spec = lambda c0: pl.BlockSpec((NA_META_SLOTS, LANES), lambda hp, b: (0, c0 // LANES + hp))
    return pl.pallas_call(
        functools.partial(_na_kernel, rows=rows, unroll=unroll),
        grid=(pairs, B),
        in_specs=[
            seq_spec(COL_QA), seq_spec(COL_KA), seq_spec(COL_VA),
            meta_spec(COL_KA), meta_spec(COL_VA),
            pl.BlockSpec((1, 2, NA_UNION, 2 * NA_HALF, NA_KEYS), lambda hp, b: (hp, 0, 0, 0, 0)),
        ],
        out_specs=pl.BlockSpec((n, LANES), lambda hp, b: (b, hp)),
        out_shape=jax.ShapeDtypeStruct((B * n, NA_WIDTH), BF16),
        compiler_params=pltpu.CompilerParams(
            dimension_semantics=("arbitrary", "arbitrary"), vmem_limit_bytes=VMEM_LIMIT),
        name="na_attn",
    )(proj, proj, proj, proj_meta_keys, proj_meta_keys, bias)


def _ret_kernel(q_ref, k_ref, v_ref, g_ref, km_ref, vm_ref, lgf_ref, lgb_ref, gn_ref, o_ref,
                ds_ref, sf_ref, sb_ref, *, n_chunks, group):
    lgf = lgf_ref[0]
    lgb = lgb_ref[0]
    ii = lax.broadcasted_iota(jnp.int32, (CHUNK, CHUNK), 0).astype(F32)
    jj = lax.broadcasted_iota(jnp.int32, (CHUNK, CHUNK), 1).astype(F32)
    diff = ii - jj
    dmat = jnp.where(diff >= 0, jnp.exp(jnp.maximum(diff, 0.0) * lgf), jnp.exp(jnp.maximum(-diff, 0.0) * lgb))
    xi_f = jnp.exp((ii + 1.0) * lgf)
    zeta_f = jnp.exp((CHUNK - 1.0 - ii) * lgf)
    cd_f = jnp.exp(CHUNK * lgf)
    xi_b = jnp.exp((CHUNK - ii) * lgb)
    zeta_b = jnp.exp(ii * lgb)
    cd_b = jnp.exp(CHUNK * lgb)
    nt = (((1,), (1,)), ((), ()))
    tn_dims = (((0,), (0,)), ((), ()))
    gn = gn_ref[...]

    def chunk_slice(ch):
        return pl.ds(pl.multiple_of(ch * CHUNK, CHUNK), CHUNK)

    def increments(k, v):
        kf = k.astype(F32)
        kz = jnp.concatenate([(kf * zeta_f).astype(BF16), (kf * zeta_b).astype(BF16)], axis=1)
        return lax.dot_general(kz, v, tn_dims, preferred_element_type=F32)

    def stage1(gi, c):
        for u in range(group):
            ch = gi * group + u
            sl = chunk_slice(ch)
            ds_ref[ch] = increments(k_ref[sl, :], v_ref[sl, :])
        return c

    lax.fori_loop(0, n_chunks // group, stage1, 0)

    def scan_f(ch, S):
        sf_ref[ch] = S.astype(BF16)
        return S * cd_f + ds_ref[ch, :CHUNK, :]

    def scan_b(t, S):
        ch = n_chunks - 1 - t
        sb_ref[ch] = S.astype(BF16)
        return S * cd_b + ds_ref[ch, CHUNK:, :]

    S0 = increments(km_ref[...], vm_ref[...])[:CHUNK]
    lax.fori_loop(0, n_chunks, scan_f, S0, unroll=4)
    lax.fori_loop(0, n_chunks, scan_b, jnp.zeros((CHUNK, CHUNK), F32), unroll=4)

    def scores(ch):
        sl = chunk_slice(ch)
        q = q_ref[sl, :]
        s = lax.dot_general(q, k_ref[sl, :], nt, preferred_element_type=F32) * dmat
        qf = q.astype(F32)
        return jnp.concatenate([s.astype(BF16), (qf * xi_f).astype(BF16), (qf * xi_b).astype(BF16)], axis=1)

    def mix(ch, lhs):
        rhs = jnp.concatenate([v_ref[chunk_slice(ch), :], sf_ref[ch], sb_ref[ch]], axis=0)
        return jnp.dot(lhs, rhs, preferred_element_type=F32)

    def finish(ch, o):
        sl = chunk_slice(ch)
        mu = jnp.mean(o, axis=-1, keepdims=True)
        oc = o - mu
        var = jnp.mean(oc * oc, axis=-1, keepdims=True)
        on = oc * lax.rsqrt(var + LN_EPS) * gn
        g = g_ref[sl, :].astype(F32)
        o_ref[sl, :] = (g / (1.0 + jnp.exp(-g)) * on).astype(BF16)

    def stage3(gi, c):
        chs = [gi * group + u for u in range(group)]
        lead = min(MXU_LEAD, group)
        lhs = [scores(ch) for ch in chs[:lead]]
        for u, ch in enumerate(chs):
            if u + lead < group:
                lhs.append(scores(chs[u + lead]))
            finish(ch, mix(ch, lhs[u]))
        return c

    lax.fori_loop(0, n_chunks // group, stage3, 0)


def _retention(proj, proj_meta_pad, lgf, lgb, gn_g, B, n):
    assert n % CHUNK == 0
    n_chunks = n // CHUNK
    group = math.gcd(n_chunks, RET_GROUP_CHUNKS)
    seq_spec = lambda c0: pl.BlockSpec((n, LANES), lambda b, h: (b, c0 // LANES + h))
    meta_spec = lambda c0: pl.BlockSpec((CHUNK, LANES), lambda b, h: (0, c0 // LANES + h))
    lg_spec = pl.BlockSpec((1, 1, LANES), lambda b, h: (h, 0, 0))
    return pl.pallas_call(
        functools.partial(_ret_kernel, n_chunks=n_chunks, group=group),
        grid=(B, RET_HEADS),
        in_specs=[
            seq_spec(COL_QR), seq_spec(COL_KR), seq_spec(COL_VR), seq_spec(COL_GR),
            meta_spec(COL_KR), meta_spec(COL_VR),
            lg_spec, lg_spec,
            pl.BlockSpec((1, LANES), lambda b, h: (0, h)),
        ],
        out_specs=pl.BlockSpec((n, LANES), lambda b, h: (b, h)),
        out_shape=jax.ShapeDtypeStruct((B * n, RET_WIDTH), BF16),
        scratch_shapes=[pltpu.VMEM((n_chunks, 2 * CHUNK, LANES), F32),
                        pltpu.VMEM((n_chunks, CHUNK, LANES), BF16),
                        pltpu.VMEM((n_chunks, CHUNK, LANES), BF16)],
        compiler_params=pltpu.CompilerParams(
            dimension_semantics=("arbitrary", "arbitrary"), vmem_limit_bytes=VMEM_LIMIT),
        name="retention",
    )(proj, proj, proj, proj, proj_meta_pad, proj_meta_pad, lgf, lgb, gn_g)


def _outproj_kernel(ona_ref, oret_ref, x_ref, w_ref, gin_ref, bin_ref, g1_ref, b1_ref, o_ref, *, row_blocks):
    def norm_store(rows, mix):
        h = _layer_norm_rows(x_ref[rows, :], gin_ref[...], bin_ref[...])
        o_ref[rows, :] = _layer_norm_rows(ALPHA * h + mix, g1_ref[...], b1_ref[...])

    pending = None
    start = 0
    for size in row_blocks:
        rows = slice(start, start + size)
        start += size
        lhs = jnp.concatenate([ona_ref[rows, :], oret_ref[rows, :]], axis=1)
        mix = jnp.dot(lhs, w_ref[...], preferred_element_type=F32)
        if pending is not None:
            norm_store(*pending)
        pending = (rows, mix)
    norm_store(*pending)


def _outproj(o_na, o_ret, x2d, w_bf, gin, bin_, g1, b1, tm=512):
    T = x2d.shape[0]
    assert T % tm == 0 and tm % 4 == 0
    row_blocks = (tm // 2, tm // 4, tm // 4)
    row_vec = pl.BlockSpec((1, D_MODEL), lambda i: (0, 0))
    return pl.pallas_call(
        functools.partial(_outproj_kernel, row_blocks=row_blocks),
        grid=(T // tm,),
        in_specs=[
            pl.BlockSpec((tm, NA_WIDTH), lambda i: (i, 0)),
            pl.BlockSpec((tm, RET_WIDTH), lambda i: (i, 0)),
            pl.BlockSpec((tm, D_MODEL), lambda i: (i, 0)),
            pl.BlockSpec((D_MODEL, D_MODEL), lambda i: (0, 0), pipeline_mode=pl.Buffered(1)),
            row_vec, row_vec, row_vec, row_vec,
        ],
        out_specs=pl.BlockSpec((tm, D_MODEL), lambda i: (i, 0)),
        out_shape=jax.ShapeDtypeStruct((T, D_MODEL), F32),
        compiler_params=pltpu.CompilerParams(dimension_semantics=("arbitrary",), vmem_limit_bytes=VMEM_LIMIT),
        name="outproj_ln1",
    )(o_na, o_ret, x2d, w_bf, gin, bin_, g1, b1)


def _ffn_kernel(h_ref, wg_ref, wu_ref, wd_ref, g2_ref, b2_ref, o_ref, hb_ref, *, rb):
    j = pl.program_id(1)
    tm = h_ref.shape[0]

    @pl.when(j == 0)
    def _():
        hb_ref[...] = h_ref[...].astype(BF16)
        o_ref[...] = jnp.zeros_like(o_ref)

    def activations():
        hb = hb_ref[...]
        g = jnp.dot(hb, wg_ref[...], preferred_element_type=F32)
        u = jnp.dot(hb, wu_ref[...], preferred_element_type=F32)
        return (g / (1.0 + jnp.exp(-g)) * u).astype(BF16)

    last = pl.num_programs(1) - 1

    @pl.when(j != last)
    def _():
        o_ref[...] += jnp.dot(activations(), wd_ref[...], preferred_element_type=F32)

    @pl.when(j == last)
    def _():
        a = activations()

        def norm_store(rows, y):
            o_ref[rows, :] = _layer_norm_rows(ALPHA * h_ref[rows, :] + y, g2_ref[...], b2_ref[...])

        pending = None
        for sb in range(tm // rb):
            rows = slice(sb * rb, (sb + 1) * rb)
            y = o_ref[rows, :] + jnp.dot(a[rows, :], wd_ref[...], preferred_element_type=F32)
            if pending is not None:
                norm_store(*pending)
            pending = (rows, y)
        norm_store(*pending)


def _ffn(h1, wg_bf, wu_bf, wd_bf, g2, b2, tm=512, tf=512):
    T = h1.shape[0]
    assert T % tm == 0 and D_FF % tf == 0
    row_vec = pl.BlockSpec((1, D_MODEL), lambda i, j: (0, 0))
    return pl.pallas_call(
        functools.partial(_ffn_kernel, rb=256),
        grid=(T // tm, D_FF // tf),
        in_specs=[
            pl.BlockSpec((tm, D_MODEL), lambda i, j: (i, 0)),
            pl.BlockSpec((D_MODEL, tf), lambda i, j: (0, j)),
            pl.BlockSpec((D_MODEL, tf), lambda i, j: (0, j)),
            pl.BlockSpec((tf, D_MODEL), lambda i, j: (j, 0)),
            row_vec, row_vec,
        ],
        out_specs=pl.BlockSpec((tm, D_MODEL), lambda i, j: (i, 0)),
        out_shape=jax.ShapeDtypeStruct((T, D_MODEL), F32),
        scratch_shapes=[pltpu.VMEM((tm, D_MODEL), BF16)],
        compiler_params=pltpu.CompilerParams(
            dimension_semantics=("arbitrary", "arbitrary"), vmem_limit_bytes=VMEM_LIMIT),
        name="ffn_ln2",
    )(h1, wg_bf, wu_bf, wd_bf, g2, b2)


def _rope_tables(pos):
    half = RET_HEAD_DIM // 2
    inv = ROPE_BASE ** (-jnp.arange(half, dtype=F32) / half)
    n = pos.shape[0]
    if n % ROPE_STRIDE == 0:
        coarse = (pos[::ROPE_STRIDE] - pos[0])[:, None] * inv[None, :]
        fine = pos[:ROPE_STRIDE, None] * inv[None, :]
        cc, sc = jnp.cos(coarse)[:, None, :], jnp.sin(coarse)[:, None, :]
        cf, sf = jnp.cos(fine)[None, :, :], jnp.sin(fine)[None, :, :]
        c = (cc * cf - sc * sf).reshape(n, half)
        s = (sc * cf + cc * sf).reshape(n, half)
    else:
        ang = pos[:, None] * inv[None, :]
        c, s = jnp.cos(ang), jnp.sin(ang)
    cq = jnp.concatenate([c, c], axis=-1)
    sq = jnp.concatenate([-s, s], axis=-1)
    ks = RET_HEAD_DIM ** -0.5
    return (jnp.stack([cq, cq * ks]), jnp.stack([sq, sq * ks]))


def _pick_tm(n, cap):
    tm = cap
    while n % tm:
        tm //= 2
    return tm


def kernel(x_prompt, x_sample, meta_tokens, ln_in_g, ln_in_b, w_in, na_rpb, ret_decay_f, ret_decay_b,
           ret_gn_g, w_out, ln1_g, ln1_b, w_ffn_gate, w_ffn_up, w_ffn_down, ln2_g, ln2_b):
    row = lambda v: v.reshape(1, -1).astype(F32)
    gin, bin_ = row(ln_in_g), row(ln_in_b)
    w_in_bf = w_in[0].astype(BF16)
    w_out_bf = w_out[0].astype(BF16)
    wg_bf, wu_bf, wd_bf = w_ffn_gate[0].astype(BF16), w_ffn_up[0].astype(BF16), w_ffn_down[0].astype(BF16)
    bias = _na_bias(na_rpb[0])
    lg = lambda d: jnp.broadcast_to(jax.nn.log_sigmoid(d[0].astype(F32))[:, None, None], (RET_HEADS, 1, LANES))
    lgf, lgb = lg(ret_decay_f), lg(ret_decay_b)
    gn_g = row(ret_gn_g[0])

    meta_tabs = _rope_tables(jnp.arange(N_META, dtype=F32))
    proj_meta = _inproj(meta_tokens.astype(F32), gin, bin_, w_in_bf, meta_tabs, N_META, N_META)
    proj_meta_pad = jnp.pad(proj_meta, ((CHUNK - N_META, 0), (0, 0)))

    n_max = max(x_prompt.shape[1], x_sample.shape[1])
    all_tabs = _rope_tables(jnp.arange(n_max, dtype=F32) + float(N_META))

    def group(x):
        B, n, _ = x.shape
        x2d = x.reshape(B * n, D_MODEL)
        tabs = tuple(t[:, :n] for t in all_tabs)
        proj = _inproj(x2d, gin, bin_, w_in_bf, tabs, n, _pick_tm(n, 1024))
        o_na = _na(proj, _na_meta(proj_meta), bias, B, n)
        o_ret = _retention(proj, proj_meta_pad, lgf, lgb, gn_g, B, n)
        h1 = _outproj(o_na, o_ret, x2d, w_out_bf, gin, bin_, row(ln1_g[0]), row(ln1_b[0]), tm=_pick_tm(B * n, 512))
        y = _ffn(h1, wg_bf, wu_bf, wd_bf, row(ln2_g[0]), row(ln2_b[0]), tm=_pick_tm(B * n, 1024))
        return y.reshape(B, n, D_MODEL)

    return (group(x_prompt), group(x_sample))


def _na_meta(proj_meta):
    return jnp.pad(proj_meta, ((0, NA_META_SLOTS - N_META), (0, 0)))
```

```python
import functools
import math

import jax
import jax.numpy as jnp
from jax import lax
from jax.experimental import pallas as pl
from jax.experimental.pallas import tpu as pltpu

D_MODEL = 2048
N_META = 16
GRID_W = 64
NA_HEADS = 16
NA_HEAD_DIM = 64
NA_WIDTH = NA_HEADS * NA_HEAD_DIM
NA_WIN_H = 8
NA_WIN_W = 16
NA_HALF = GRID_W // 2
NA_KEY_COLS = 48
NA_KEY_SHIFT = GRID_W - NA_KEY_COLS
NA_UNION = NA_WIN_H + 1
NA_KEYS = 512
NA_META_SLOTS = NA_KEYS - NA_UNION * NA_KEY_COLS
RET_HEADS = 8
RET_HEAD_DIM = 128
RET_WIDTH = RET_HEADS * RET_HEAD_DIM
CHUNK = 128
ROPE_BASE = 10000.0
ROPE_STRIDE = 128
IN_WIDTH = 3 * NA_WIDTH + 4 * RET_WIDTH
D_FF = 5632
LN_EPS = 1e-5
ALPHA = 2.0 ** 0.25

COL_QA, COL_KA, COL_VA = 0, NA_WIDTH, 2 * NA_WIDTH
COL_QR = 3 * NA_WIDTH
COL_KR = COL_QR + RET_WIDTH
COL_VR = COL_KR + RET_WIDTH
COL_GR = COL_VR + RET_WIDTH

NA_GROUP_ROWS = 64
RET_GROUP_CHUNKS = 32
MXU_LEAD = 3

LANES = 128
NEG = -1e30
VMEM_LIMIT = 56 * 1024 * 1024

F32 = jnp.float32
BF16 = jnp.bfloat16


def _layer_norm_rows(x, g, b):
    mu = jnp.mean(x, axis=-1, keepdims=True)
    xc = x - mu
    var = jnp.mean(xc * xc, axis=-1, keepdims=True)
    return xc * lax.rsqrt(var + LN_EPS) * g + b


def _inproj_kernel(x_ref, g_ref, b_ref, w_ref, c_ref, s_ref, o_ref, hb_ref, *, tn, rb):
    j = pl.program_id(1)
    tm = x_ref.shape[0]

    def store(acc, rows, tile):
        for t in range(tn // LANES):
            col = tile * tn + t * LANES
            xs = acc[:, t * LANES:(t + 1) * LANES]
            if COL_QR <= col < COL_VR:
                which = 0 if col < COL_KR else 1
                xs = xs * c_ref[which, rows, :] + pltpu.roll(xs, LANES // 2, 1) * s_ref[which, rows, :]
            o_ref[rows, t * LANES:(t + 1) * LANES] = xs.astype(BF16)

    @pl.when(j == 0)
    def _():
        for sb in range(tm // rb):
            rows = slice(sb * rb, (sb + 1) * rb)
            hb_ref[rows, :] = _layer_norm_rows(x_ref[rows, :], g_ref[...], b_ref[...]).astype(BF16)
            store(jnp.dot(hb_ref[rows, :], w_ref[...], preferred_element_type=F32), rows, 0)

    for tile in range(1, IN_WIDTH // tn):
        @pl.when(j == tile)
        def _(tile=tile):
            store(jnp.dot(hb_ref[...], w_ref[...], preferred_element_type=F32), slice(None), tile)


def _inproj(x2d, ln_g, ln_b, w_bf, tabs, n_seq, tm, tn=1792):
    T = x2d.shape[0]
    assert T % tm == 0 and n_seq % tm == 0 and IN_WIDTH % tn == 0 and tn % LANES == 0
    blocks_per_seq = n_seq // tm
    tab_spec = pl.BlockSpec((2, tm, LANES), lambda i, j: (0, i % blocks_per_seq, 0))
    rb = min(tm, 256)
    return pl.pallas_call(
        functools.partial(_inproj_kernel, tn=tn, rb=rb),
        grid=(T // tm, IN_WIDTH // tn),
        in_specs=[
            pl.BlockSpec((tm, D_MODEL), lambda i, j: (i, 0)),
            pl.BlockSpec((1, D_MODEL), lambda i, j: (0, 0)),
            pl.BlockSpec((1, D_MODEL), lambda i, j: (0, 0)),
            pl.BlockSpec((D_MODEL, tn), lambda i, j: (0, j)),
            tab_spec, tab_spec,
        ],
        out_specs=pl.BlockSpec((tm, tn), lambda i, j: (i, j)),
        out_shape=jax.ShapeDtypeStruct((T, IN_WIDTH), BF16),
        scratch_shapes=[pltpu.VMEM((tm, D_MODEL), BF16)],
        compiler_params=pltpu.CompilerParams(
            dimension_semantics=("arbitrary", "arbitrary"), vmem_limit_bytes=VMEM_LIMIT),
        name="inproj",
    )(x2d, ln_g, ln_b, w_bf, *tabs)


def _na_kernel(q_ref, k_ref, v_ref, km_ref, vm_ref, bias_ref, o_ref, *, rows, unroll):
    lane = lax.broadcasted_iota(jnp.int32, (1, LANES), 1)
    scale = NA_HEAD_DIM ** -0.5
    qmask = [jnp.where((lane // NA_HEAD_DIM) == hh, scale, 0.0).astype(BF16) for hh in range(2)]
    km = km_ref[...]
    vm = vm_ref[...]
    nt = (((1,), (1,)), ((), ()))

    def window(ref, meta, b0, hf):
        parts = [ref[pl.ds(pl.multiple_of((b0 + i) * GRID_W + NA_KEY_SHIFT * hf, NA_KEY_SHIFT), NA_KEY_COLS), :]
                 for i in range(NA_UNION)]
        return jnp.concatenate(parts + [meta], axis=0)

    units = [(u, hf) for u in range(unroll // 2) for hf in range(2)]

    def locate(g, idx):
        u, hf = units[idx]
        r0 = g * unroll + 2 * u
        b0 = jnp.clip(r0 - NA_WIN_H // 2, 0, rows - NA_UNION)
        qsl = [pl.ds(pl.multiple_of((r0 + t) * GRID_W + NA_HALF * hf, NA_HALF), NA_HALF) for t in range(2)]
        return r0 - b0, qsl, b0, hf

    def scores(g, idx):
        e0, qsl, b0, hf = locate(g, idx)
        q2 = jnp.concatenate([q_ref[sl, :] * qmask[hh] for sl in qsl for hh in range(2)], axis=0)
        s = lax.dot_general(q2, window(k_ref, km, b0, hf), nt, preferred_element_type=F32)
        bias = jnp.concatenate([bias_ref[0, hf, pl.ds(e0 + t, 1)][0] for t in range(2)], axis=0)
        return s + bias

    def finish(s, g, idx):
        _, qsl, b0, hf = locate(g, idx)
        m = jnp.max(s, axis=-1, keepdims=True)
        p = jnp.exp(s - m)
        den = jnp.sum(p, axis=-1, keepdims=True)
        o2 = jnp.dot(p.astype(BF16), window(v_ref, vm, b0, hf), preferred_element_type=F32) / den
        for t in range(2):
            blk = o2[2 * NA_HALF * t:2 * NA_HALF * (t + 1)]
            o_ref[qsl[t], :] = jnp.where(lane < NA_HEAD_DIM, blk[:NA_HALF], blk[NA_HALF:]).astype(BF16)

    def body(g, c):
        lead = min(MXU_LEAD, len(units))
        staged = [scores(g, idx) for idx in range(lead)]
        for idx in range(len(units)):
            if idx + lead < len(units):
                staged.append(scores(g, idx + lead))
            finish(staged[idx], g, idx)
        return c

    lax.fori_loop(0, rows // unroll, body, 0)


def _na_bias_kernel(rpb_ref, o_ref):
    c32 = lax.broadcasted_iota(jnp.int32, (NA_HALF, LANES), 0)
    lane = lax.broadcasted_iota(jnp.int32, (NA_HALF, LANES), 1)
    masked = jnp.where(lane + c32 < 0, 0.0, NEG).astype(F32)[:, :NA_KEY_COLS]
    meta = jnp.where(lane + LANES * c32 < N_META + LANES * c32, 0.0, NEG).astype(F32)[:, :NA_META_SLOTS]
    tiles = {}
    for hf in range(2):
        c = c32 + NA_HALF * hf
        kc = lane + NA_KEY_SHIFT * hf
        cs = jnp.clip(c - NA_WIN_W // 2, 0, GRID_W - NA_WIN_W)
        valid = (kc >= cs) & (kc < cs + NA_WIN_W) & (lane < NA_KEY_COLS)
        shift = (LANES - (NA_WIN_W - 1) - NA_KEY_SHIFT * hf + NA_HALF * hf) % LANES
        for h in range(2):
            for dr in range(2 * NA_WIN_H - 1):
                w = jnp.broadcast_to(rpb_ref[h, dr:dr + 1, :], (NA_HALF, LANES))
                tiles[(hf, h, dr)] = jnp.where(valid, pltpu.roll(w, shift, 1, stride=1, stride_axis=0), NEG)
    for hf in range(2):
        for e in range(NA_UNION):
            d = 1 if e > NA_WIN_H // 2 else 0
            heads = []
            for h in range(2):
                parts = [tiles[(hf, h, i - e + NA_WIN_H - 1)][:, :NA_KEY_COLS] if 0 <= i - d < NA_WIN_H else masked
                         for i in range(NA_UNION)]
                heads.append(jnp.concatenate(parts + [meta], axis=1))
            o_ref[0, hf, e] = jnp.concatenate(heads, axis=0)


def _na_bias(rpb):
    n_dr = 2 * NA_WIN_H - 1
    rpb_lanes = jnp.pad(rpb.astype(F32), ((0, 0), (0, 0), (0, LANES - rpb.shape[-1])))
    shape = (NA_HEADS // 2, 2, NA_UNION, 2 * NA_HALF, NA_KEYS)
    return pl.pallas_call(
        _na_bias_kernel,
        grid=(NA_HEADS // 2,),
        in_specs=[pl.BlockSpec((2, n_dr, LANES), lambda p: (p, 0, 0))],
        out_specs=pl.BlockSpec((1,) + shape[1:], lambda p: (p, 0, 0, 0, 0)),
        out_shape=jax.ShapeDtypeStruct(shape, F32),
        compiler_params=pltpu.CompilerParams(dimension_semantics=("arbitrary",), vmem_limit_bytes=VMEM_LIMIT),
        name="na_bias",
    )(rpb_lanes)


def _na(proj, proj_meta_keys, bias, B, n):
    rows = n // GRID_W
    unroll = math.gcd(rows, NA_GROUP_ROWS)
    assert rows >= NA_UNION and unroll % 2 == 0
    pairs = NA_WIDTH // LANES
    seq_spec = lambda c0: pl.BlockSpec((n, LANES), lambda hp, b: (b, c0 // LANES + hp))
    meta_spec = lambda c0: pl.BlockSpec((NA_META_SLOTS, LANES), lambda hp, b: (0, c0 // LANES + hp))
    return pl.pallas_call(
        functools.partial(_na_kernel, rows=rows, unroll=unroll),
        grid=(pairs, B),
        in_specs=[
            seq_spec(COL_QA), seq_spec(COL_KA), seq_spec(COL_VA),
            meta_spec(COL_KA), meta_spec(COL_VA),
            pl.BlockSpec((1, 2, NA_UNION, 2 * NA_HALF, NA_KEYS), lambda hp, b: (hp, 0, 0, 0, 0)),
        ],
        out_specs=pl.BlockSpec((n, LANES), lambda hp, b: (b, hp)),
        out_shape=jax.ShapeDtypeStruct((B * n, NA_WIDTH), BF16),
        compiler_params=pltpu.CompilerParams(
            dimension_semantics=("arbitrary", "arbitrary"), vmem_limit_bytes=VMEM_LIMIT),
        name="na_attn",
    )(proj, proj, proj, proj_meta_keys, proj_meta_keys, bias)


def _ret_kernel(q_ref, k_ref, v_ref, g_ref, km_ref, vm_ref, lgf_ref, lgb_ref, gn_ref, o_ref,
                ds_ref, sf_ref, sb_ref, *, n_chunks, group):
    lgf = lgf_ref[0]
    lgb = lgb_ref[0]
    ii = lax.broadcasted_iota(jnp.int32, (CHUNK, CHUNK), 0).astype(F32)
    jj = lax.broadcasted_iota(jnp.int32, (CHUNK, CHUNK), 1).astype(F32)
    diff = ii - jj
    dmat = jnp.where(diff >= 0, jnp.exp(jnp.maximum(diff, 0.0) * lgf), jnp.exp(jnp.maximum(-diff, 0.0) * lgb))
    xi_f = jnp.exp((ii + 1.0) * lgf)
    zeta_f = jnp.exp((CHUNK - 1.0 - ii) * lgf)
    cd_f = jnp.exp(CHUNK * lgf)
    xi_b = jnp.exp((CHUNK - ii) * lgb)
    zeta_b = jnp.exp(ii * lgb)
    cd_b = jnp.exp(CHUNK * lgb)
    nt = (((1,), (1,)), ((), ()))
    tn_dims = (((0,), (0,)), ((), ()))
    gn = gn_ref[...]

    def chunk_slice(ch):
        return pl.ds(pl.multiple_of(ch * CHUNK, CHUNK), CHUNK)

    def increments(k, v):
        kf = k.astype(F32)
        kz = jnp.concatenate([(kf * zeta_f).astype(BF16), (kf * zeta_b).astype(BF16)], axis=1)
        return lax.dot_general(v, kz, tn_dims, preferred_element_type=F32)

    def stage1(gi, c):
        for u in range(group):
            ch = gi * group + u
            sl = chunk_slice(ch)
            ds_ref[ch] = increments(k_ref[sl, :], v_ref[sl, :])
        return c

    lax.fori_loop(0, n_chunks // group, stage1, 0)

    def scan_f(ch, S):
        sf_ref[ch] = S.astype(BF16)
        return S * cd_f + ds_ref[ch, :, :CHUNK]

    def scan_b(t, S):
        ch = n_chunks - 1 - t
        sb_ref[ch] = S.astype(BF16)
        return S * cd_b + ds_ref[ch, :, CHUNK:]

    S0 = increments(km_ref[...], vm_ref[...])[:, :CHUNK]
    lax.fori_loop(0, n_chunks, scan_f, S0, unroll=4)
    lax.fori_loop(0, n_chunks, scan_b, jnp.zeros((CHUNK, CHUNK), F32), unroll=4)

    def scores(ch):
        sl = chunk_slice(ch)
        q = q_ref[sl, :]
        s = lax.dot_general(q, k_ref[sl, :], nt, preferred_element_type=F32) * dmat
        qf = q.astype(F32)
        return jnp.concatenate([s.astype(BF16), (qf * xi_f).astype(BF16), (qf * xi_b).astype(BF16)], axis=1)

    def mix(ch, lhs):
        states = jnp.concatenate([sf_ref[ch], sb_ref[ch]], axis=1)
        return (jnp.dot(lhs[:, :CHUNK], v_ref[chunk_slice(ch), :], preferred_element_type=F32)
                + lax.dot_general(lhs[:, CHUNK:], states, nt, preferred_element_type=F32))

    def finish(ch, o):
        sl = chunk_slice(ch)
        mu = jnp.mean(o, axis=-1, keepdims=True)
        oc = o - mu
        var = jnp.mean(oc * oc, axis=-1, keepdims=True)
        on = oc * lax.rsqrt(var + LN_EPS) * gn
        g = g_ref[sl, :].astype(F32)
        o_ref[sl, :] = (g / (1.0 + jnp.exp(-g)) * on).astype(BF16)

    def stage3(gi, c):
        chs = [gi * group + u for u in range(group)]
        lead = min(MXU_LEAD, group)
        lhs = [scores(ch) for ch in chs[:lead]]
        for u, ch in enumerate(chs):
            if u + lead < group:
                lhs.append(scores(chs[u + lead]))
            finish(ch, mix(ch, lhs[u]))
        return c

    lax.fori_loop(0, n_chunks // group, stage3, 0)


def _retention(proj, proj_meta_pad, lgf, lgb, gn_g, B, n):
    assert n % CHUNK == 0
    n_chunks = n // CHUNK
    group = math.gcd(n_chunks, RET_GROUP_CHUNKS)
    seq_spec = lambda c0: pl.BlockSpec((n, LANES), lambda b, h: (b, c0 // LANES + h))
    meta_spec = lambda c0: pl.BlockSpec((CHUNK, LANES), lambda b, h: (0, c0 // LANES + h))
    lg_spec = pl.BlockSpec((1, 1, LANES), lambda b, h: (h, 0, 0))
    return pl.pallas_call(
        functools.partial(_ret_kernel, n_chunks=n_chunks, group=group),
        grid=(B, RET_HEADS),
        in_specs=[
            seq_spec(COL_QR), seq_spec(COL_KR), seq_spec(COL_VR), seq_spec(COL_GR),
            meta_spec(COL_KR), meta_spec(COL_VR),
            lg_spec, lg_spec,
            pl.BlockSpec((1, LANES), lambda b, h: (0, h)),
        ],
        out_specs=pl.BlockSpec((n, LANES), lambda b, h: (b, h)),
        out_shape=jax.ShapeDtypeStruct((B * n, RET_WIDTH), BF16),
        scratch_shapes=[pltpu.VMEM((n_chunks, CHUNK, 2 * LANES), F32),
                        pltpu.VMEM((n_chunks, CHUNK, LANES), BF16),
                        pltpu.VMEM((n_chunks, CHUNK, LANES), BF16)],
        compiler_params=pltpu.CompilerParams(
            dimension_semantics=("arbitrary", "arbitrary"), vmem_limit_bytes=VMEM_LIMIT),
        name="retention",
    )(proj, proj, proj, proj, proj_meta_pad, proj_meta_pad, lgf, lgb, gn_g)


def _outproj_kernel(ona_ref, oret_ref, x_ref, w_ref, gin_ref, bin_ref, g1_ref, b1_ref, o_ref, *, row_blocks):
    def norm_store(rows, mix):
        h = _layer_norm_rows(x_ref[rows, :], gin_ref[...], bin_ref[...])
        o_ref[rows, :] = _layer_norm_rows(ALPHA * h + mix, g1_ref[...], b1_ref[...])

    pending = None
    start = 0
    for size in row_blocks:
        rows = slice(start, start + size)
        start += size
        lhs = jnp.concatenate([ona_ref[rows, :], oret_ref[rows, :]], axis=1)
        mix = jnp.dot(lhs, w_ref[...], preferred_element_type=F32)
        if pending is not None:
            norm_store(*pending)
        pending = (rows, mix)
    norm_store(*pending)


def _outproj(o_na, o_ret, x2d, w_bf, gin, bin_, g1, b1, tm=512):
    T = x2d.shape[0]
    assert T % tm == 0 and tm % 4 == 0
    row_blocks = (tm // 2, tm // 4, tm // 4)
    row_vec = pl.BlockSpec((1, D_MODEL), lambda i: (0, 0))
    return pl.pallas_call(
        functools.partial(_outproj_kernel, row_blocks=row_blocks),
        grid=(T // tm,),
        in_specs=[
            pl.BlockSpec((tm, NA_WIDTH), lambda i: (i, 0)),
            pl.BlockSpec((tm, RET_WIDTH), lambda i: (i, 0)),
            pl.BlockSpec((tm, D_MODEL), lambda i: (i, 0)),
            pl.BlockSpec((D_MODEL, D_MODEL), lambda i: (0, 0), pipeline_mode=pl.Buffered(1)),
            row_vec, row_vec, row_vec, row_vec,
        ],
        out_specs=pl.BlockSpec((tm, D_MODEL), lambda i: (i, 0)),
        out_shape=jax.ShapeDtypeStruct((T, D_MODEL), F32),
        compiler_params=pltpu.CompilerParams(dimension_semantics=("arbitrary",), vmem_limit_bytes=VMEM_LIMIT),
        name="outproj_ln1",
    )(o_na, o_ret, x2d, w_bf, gin, bin_, g1, b1)


def _ffn_kernel(h_ref, wg_ref, wu_ref, wd_ref, g2_ref, b2_ref, o_ref, hb_ref, *, rb):
    j = pl.program_id(1)
    tm = h_ref.shape[0]

    @pl.when(j == 0)
    def _():
        hb_ref[...] = h_ref[...].astype(BF16)
        o_ref[...] = jnp.zeros_like(o_ref)

    hb = hb_ref[...]
    g = jnp.dot(hb, wg_ref[...], preferred_element_type=F32)
    u = jnp.dot(hb, wu_ref[...], preferred_element_type=F32)
    a = (g / (1.0 + jnp.exp(-g)) * u).astype(BF16)
    o_ref[...] += jnp.dot(a, wd_ref[...], preferred_element_type=F32)

    @pl.when(j == pl.num_programs(1) - 1)
    def _():
        def rows(i, c):
            sl = pl.ds(pl.multiple_of(i * rb, rb), rb)
            o_ref[sl, :] = _layer_norm_rows(ALPHA * h_ref[sl, :] + o_ref[sl, :], g2_ref[...], b2_ref[...])
            return c

        lax.fori_loop(0, tm // rb, rows, 0)


def _ffn(h1, wg_bf, wu_bf, wd_bf, g2, b2, tm=512, tf=512):
    T = h1.shape[0]
    assert T % tm == 0 and D_FF % tf == 0
    row_vec = pl.BlockSpec((1, D_MODEL), lambda i, j: (0, 0))
    return pl.pallas_call(
        functools.partial(_ffn_kernel, rb=128),
        grid=(T // tm, D_FF // tf),
        in_specs=[
            pl.BlockSpec((tm, D_MODEL), lambda i, j: (i, 0)),
            pl.BlockSpec((D_MODEL, tf), lambda i, j: (0, j)),
            pl.BlockSpec((D_MODEL, tf), lambda i, j: (0, j)),
            pl.BlockSpec((tf, D_MODEL), lambda i, j: (j, 0)),
            row_vec, row_vec,
        ],
        out_specs=pl.BlockSpec((tm, D_MODEL), lambda i, j: (i, 0)),
        out_shape=jax.ShapeDtypeStruct((T, D_MODEL), F32),
        scratch_shapes=[pltpu.VMEM((tm, D_MODEL), BF16)],
        compiler_params=pltpu.CompilerParams(
            dimension_semantics=("arbitrary", "arbitrary"), vmem_limit_bytes=VMEM_LIMIT),
        name="ffn_ln2",
    )(h1, wg_bf, wu_bf, wd_bf, g2, b2)


def _rope_tables(pos):
    half = RET_HEAD_DIM // 2
    inv = ROPE_BASE ** (-jnp.arange(half, dtype=F32) / half)
    n = pos.shape[0]
    if n % ROPE_STRIDE == 0:
        coarse = (pos[::ROPE_STRIDE] - pos[0])[:, None] * inv[None, :]
        fine = pos[:ROPE_STRIDE, None] * inv[None, :]
        cc, sc = jnp.cos(coarse)[:, None, :], jnp.sin(coarse)[:, None, :]
        cf, sf = jnp.cos(fine)[None, :, :], jnp.sin(fine)[None, :, :]
        c = (cc * cf - sc * sf).reshape(n, half)
        s = (sc * cf + cc * sf).reshape(n, half)
    else:
        ang = pos[:, None] * inv[None, :]
        c, s = jnp.cos(ang), jnp.sin(ang)
    cq = jnp.concatenate([c, c], axis=-1)
    sq = jnp.concatenate([-s, s], axis=-1)
    ks = RET_HEAD_DIM ** -0.5
    return (jnp.stack([cq, cq * ks]), jnp.stack([sq, sq * ks]))


def _pick_tm(n, cap):
    tm = cap
    while n % tm:
        tm //= 2
    return tm


def kernel(x_prompt, x_sample, meta_tokens, ln_in_g, ln_in_b, w_in, na_rpb, ret_decay_f, ret_decay_b,
           ret_gn_g, w_out, ln1_g, ln1_b, w_ffn_gate, w_ffn_up, w_ffn_down, ln2_g, ln2_b):
    row = lambda v: v.reshape(1, -1).astype(F32)
    gin, bin_ = row(ln_in_g), row(ln_in_b)
    w_in_bf = w_in[0].astype(BF16)
    w_out_bf = w_out[0].astype(BF16)
    wg_bf, wu_bf, wd_bf = w_ffn_gate[0].astype(BF16), w_ffn_up[0].astype(BF16), w_ffn_down[0].astype(BF16)
    bias = _na_bias(na_rpb[0])
    lg = lambda d: jnp.broadcast_to(jax.nn.log_sigmoid(d[0].astype(F32))[:, None, None], (RET_HEADS, 1, LANES))
    lgf, lgb = lg(ret_decay_f), lg(ret_decay_b)
    gn_g = row(ret_gn_g[0])

    meta_tabs = _rope_tables(jnp.arange(N_META, dtype=F32))
    proj_meta = _inproj(meta_tokens.astype(F32), gin, bin_, w_in_bf, meta_tabs, N_META, N_META)
    proj_meta_pad = jnp.pad(proj_meta, ((CHUNK - N_META, 0), (0, 0)))

    n_max = max(x_prompt.shape[1], x_sample.shape[1])
    all_tabs = _rope_tables(jnp.arange(n_max, dtype=F32) + float(N_META))

    def group(x):
        B, n, _ = x.shape
        x2d = x.reshape(B * n, D_MODEL)
        tabs = tuple(t[:, :n] for t in all_tabs)
        proj = _inproj(x2d, gin, bin_, w_in_bf, tabs, n, _pick_tm(n, 1024))
        o_na = _na(proj, _na_meta(proj_meta), bias, B, n)
        o_ret = _retention(proj, proj_meta_pad, lgf, lgb, gn_g, B, n)
        h1 = _outproj(o_na, o_ret, x2d, w_out_bf, gin, bin_, row(ln1_g[0]), row(ln1_b[0]), tm=_pick_tm(B * n, 512))
        y = _ffn(h1, wg_bf, wu_bf, wd_bf, row(ln2_g[0]), row(ln2_b[0]), tm=_pick_tm(B * n, 1024))
        return y.reshape(B, n, D_MODEL)

    return (group(x_prompt), group(x_sample))


def _na_meta(proj_meta):
    return jnp.pad(proj_meta, ((0, NA_META_SLOTS - N_META), (0, 0)))
```
